```python
import math
import jax, jax.numpy as jnp
from jax import lax
import numpy as np

D_MODEL = 1024
BATCH = 8
SEQ = 2048
DEPTH = 2
DEC_BATCH = 32
DEC_SEQ = 4
PAST_LEN = 8192
PAGE_SIZE = 128

F32 = jnp.float32
HEAD_DIM = 64
N_Q_HEADS = 8
N_KV_HEADS = 2
HEADS_PER_GROUP = N_Q_HEADS // N_KV_HEADS
NSA_WIDTH = N_Q_HEADS * HEAD_DIM
KV_WIDTH = N_KV_HEADS * HEAD_DIM
CONV_DIM = D_MODEL - NSA_WIDTH
CONV_WIDTH = 3
MIX_WIDTH = NSA_WIDTH + CONV_DIM
N_BRANCH = 3
N_KV_ROLES = 4
ROPE_DIM = HEAD_DIM // 4
ROPE_THETA = 500000.0
CMP_LEN = 32
CMP_STRIDE = 16
SLC_BLOCK = 64
N_SELECT = 16
WINDOW = 512
Q_BLOCK = 128
PEER_HEADS = 8
PEER_QDIM = 256
PEER_NKEYS = 128
PEER_EXPERTS = PEER_NKEYS * PEER_NKEYS
PEER_TOPK = 16
PEER_CHUNK = 256
PEER_V_SCALE = 0.2
RMS_EPS = 1e-6
NEG_INF = -1e30
FORCE_BONUS = 1e4

kernel_name = 'nsa_conv_peer_hybrid_step'


def rms_norm(x, g):
    xf = x.astype(F32)
    y = xf * lax.rsqrt(jnp.mean(xf * xf, axis=-1, keepdims=True) + RMS_EPS)
    return (y * g.astype(F32)).astype(x.dtype)


def partial_rope(x, pos):
    half = ROPE_DIM // 2
    inv_freq = ROPE_THETA ** (-jnp.arange(half, dtype=F32) / half)
    ang = pos.astype(F32)[:, None] * inv_freq[None, :]
    cos = jnp.cos(ang)[None, :, None, :]
    sin = jnp.sin(ang)[None, :, None, :]
    xf = x.astype(F32)
    x1 = xf[..., :half]
    x2 = xf[..., half:ROPE_DIM]
    out = jnp.concatenate([x1 * cos - x2 * sin, x2 * cos + x1 * sin, xf[..., ROPE_DIM:]], axis=-1)
    return out.astype(x.dtype)


def masked_softmax(s, mask):
    s = jnp.where(mask, s, NEG_INF)
    p = jnp.where(mask, jnp.exp(s - jnp.max(s, axis=-1, keepdims=True)), 0.0)
    return p / jnp.maximum(jnp.sum(p, axis=-1, keepdims=True), 1e-30)


def compress_rows(rows, pe, w1, w2):
    b, t = rows.shape[0], rows.shape[1]
    r = CMP_LEN // CMP_STRIDE
    n_cmp = (t - CMP_LEN) // CMP_STRIDE + 1
    n_chunks = n_cmp + r - 1
    chunks = rows[:, :n_chunks * CMP_STRIDE].reshape(b, n_chunks, CMP_STRIDE, N_KV_HEADS, HEAD_DIM)
    pe_r = pe.reshape(r, CMP_STRIDE, HEAD_DIM)
    w1_r = w1.reshape(r, CMP_STRIDE, HEAD_DIM, HEAD_DIM)
    hid = sum(jnp.einsum('bcsgd,sde->bcge', chunks[:, m:m + n_cmp] + pe_r[m][:, None, :], w1_r[m])
              for m in range(r))
    return jnp.einsum('bcge,ef->bcgf', jax.nn.gelu(hid), w2)


def nsa_core(q, pos0, kc, vc, ks_blocks, vs_blocks, win_k, win_v):
    b, nq = q.shape[0], q.shape[1]
    qb_len = min(Q_BLOCK, nq)
    nb = -(-nq // qb_len)
    qg = q.reshape(b, nq, N_KV_HEADS, HEADS_PER_GROUP, HEAD_DIM)
    qg = jnp.pad(qg, ((0, 0), (0, nb * qb_len - nq), (0, 0), (0, 0), (0, 0)))
    q_blocks = jnp.moveaxis(qg.reshape(b, nb, qb_len, N_KV_HEADS, HEADS_PER_GROUP, HEAD_DIM), 1, 0)
    win_len = WINDOW + qb_len
    pad_w = WINDOW + nb * qb_len - win_k.shape[1]
    win_k = jnp.pad(win_k, ((0, 0), (0, pad_w), (0, 0), (0, 0)))
    win_v = jnp.pad(win_v, ((0, 0), (0, pad_w), (0, 0), (0, 0)))
    n_cmp = kc.shape[1]
    n_slc = ks_blocks.shape[2]
    k_sel = min(N_SELECT, n_slc)
    n_keys = k_sel * SLC_BLOCK
    r = CMP_LEN // CMP_STRIDE
    sr = SLC_BLOCK // CMP_STRIDE
    cmp_end = jnp.arange(n_cmp) * CMP_STRIDE + CMP_LEN - 1
    slc_id = jnp.arange(n_slc)
    bi = jnp.arange(b)[:, None, None, None]
    gi = jnp.arange(N_KV_HEADS)[None, :, None, None]
    scale = HEAD_DIM ** -0.5

    def block_fn(args):
        qb, blk = args
        start = blk * qb_len
        qpos = pos0 + start + jnp.arange(qb_len)
        s = jnp.einsum('bqghd,bngd->bghqn', qb, kc, preferred_element_type=F32) * scale
        p = masked_softmax(s, cmp_end[None, :] <= qpos[:, None])
        o_cmp = jnp.einsum('bghqn,bngd->bqghd', p.astype(vc.dtype), vc)
        pg = jnp.pad(jnp.sum(p, axis=2), ((0, 0), (0, 0), (0, 0), (r - 1, sr * n_slc - n_cmp)))
        p_slc = sum(pg[..., (m - n + r - 1)::sr][..., :n_slc] for m in range(sr) for n in range(r))
        cur = (qpos // SLC_BLOCK)[:, None]
        valid = slc_id[None, :] * SLC_BLOCK <= qpos[:, None]
        forced = (slc_id[None, :] == 0) | (slc_id[None, :] == cur) | (slc_id[None, :] == cur - 1)
        score = jnp.where(valid, p_slc + jnp.where(forced, FORCE_BONUS, 0.0), NEG_INF)
        _, sel = lax.top_k(score, k_sel)
        k_rows = ks_blocks[bi, gi, sel].reshape(b, N_KV_HEADS, qb_len, n_keys, HEAD_DIM)
        v_rows = vs_blocks[bi, gi, sel].reshape(b, N_KV_HEADS, qb_len, n_keys, HEAD_DIM)
        kpos = (sel[..., None] * SLC_BLOCK + jnp.arange(SLC_BLOCK)).reshape(b, N_KV_HEADS, qb_len, n_keys)
        s = jnp.einsum('bqghd,bgqkd->bghqk', qb, k_rows, preferred_element_type=F32) * scale
        p = masked_softmax(s, (kpos <= qpos[:, None])[:, :, None])
        o_slc = jnp.einsum('bghqk,bgqkd->bqghd', p.astype(v_rows.dtype), v_rows)
        wk = lax.dynamic_slice_in_dim(win_k, start, win_len, axis=1)
        wv = lax.dynamic_slice_in_dim(win_v, start, win_len, axis=1)
        wpos = pos0 - WINDOW + start + jnp.arange(win_len)
        wmask = ((wpos[None, :] <= qpos[:, None]) & (wpos[None, :] > qpos[:, None] - WINDOW)
                 & (wpos[None, :] >= 0))
        s = jnp.einsum('bqghd,bkgd->bghqk', qb, wk, preferred_element_type=F32) * scale
        p = masked_softmax(s, wmask)
        o_win = jnp.einsum('bghqk,bkgd->bqghd', p.astype(wv.dtype), wv)
        return jnp.stack([o_cmp, o_slc, o_win], axis=-2)

    out = lax.map(block_fn, (q_blocks, jnp.arange(nb)))
    out = jnp.moveaxis(out, 0, 1).reshape(b, nb * qb_len, N_Q_HEADS, N_BRANCH, HEAD_DIM)
    return out[:, :nq]


def peer_ffn(h, wq, subkeys, u_tab, v_tab):
    b, nq, d = h.shape
    n = b * nq
    hf = h.reshape(n, d)
    q = (hf @ wq).reshape(n, PEER_HEADS, 2, PEER_QDIM // 2)
    s = jnp.einsum('nhpd,hpkd->nhpk', q, subkeys, preferred_element_type=F32)
    v_half, i_half = lax.top_k(s, PEER_TOPK)
    cand = v_half[:, :, 0, :, None] + v_half[:, :, 1, None, :]
    top_s, top_c = lax.top_k(cand.reshape(n, PEER_HEADS, PEER_TOPK * PEER_TOPK), PEER_TOPK)
    i1 = jnp.take_along_axis(i_half[:, :, 0], top_c // PEER_TOPK, axis=-1)
    i2 = jnp.take_along_axis(i_half[:, :, 1], top_c % PEER_TOPK, axis=-1)
    experts = i1 * PEER_NKEYS + i2
    gates = jax.nn.softmax(top_s, axis=-1)
    chunk = min(PEER_CHUNK, n)
    nc = -(-n // chunk)
    pad = nc * chunk - n
    hp = jnp.pad(hf, ((0, pad), (0, 0))).reshape(nc, chunk, d)
    ep = jnp.pad(experts, ((0, pad), (0, 0), (0, 0))).reshape(nc, chunk, PEER_HEADS, PEER_TOPK)
    gp = jnp.pad(gates, ((0, pad), (0, 0), (0, 0))).reshape(nc, chunk, PEER_HEADS, PEER_TOPK)

    def chunk_fn(args):
        hc, ec, gc = args
        act = jax.nn.gelu(jnp.einsum('nd,nhkd->nhk', hc, u_tab[ec], preferred_element_type=F32))
        return jnp.einsum('nhk,nhkd->nd', (gc * act).astype(hc.dtype), v_tab[ec])

    out = lax.map(chunk_fn, (hp, ep, gp)).reshape(nc * chunk, d)[:n]
    return out.reshape(b, nq, d)


def decoder_layer(x, pos0, past_rows, win_prefix, n_win_prev, conv_prefix, lp):
    (norm1, w_in, q_g, k_g, pe, w1, w2, conv_w, out_g, w_out, norm2, pwq, psk, pu, pv) = lp
    b, nq, _ = x.shape
    qpos = pos0 + jnp.arange(nq)
    h = rms_norm(x, norm1)
    z = h @ w_in
    c1 = NSA_WIDTH
    c2 = c1 + 6 * KV_WIDTH
    c3 = c2 + N_Q_HEADS * N_BRANCH
    c4 = c3 + CONV_DIM
    c5 = c4 + CONV_DIM
    q, kv, gl, gb, gc, hc = jnp.split(z, [c1, c2, c3, c4, c5], axis=-1)
    q = partial_rope(rms_norm(q.reshape(b, nq, N_Q_HEADS, HEAD_DIM), q_g), qpos)
    kv = kv.reshape(b, nq, 6, N_KV_HEADS, HEAD_DIM)
    k_all = rms_norm(kv[:, :, 0::2], k_g[:, None, :])
    k_all = partial_rope(k_all.reshape(b, nq, N_BRANCH * N_KV_HEADS, HEAD_DIM), qpos)
    k_all = k_all.reshape(b, nq, N_BRANCH, N_KV_HEADS, HEAD_DIM)
    v_all = kv[:, :, 1::2]
    new_rows = jnp.stack([k_all[:, :, 0], v_all[:, :, 0], k_all[:, :, 1], v_all[:, :, 1]], axis=2)
    new_win = jnp.stack([k_all[:, :, 2], v_all[:, :, 2]], axis=2)
    rows = jnp.concatenate([past_rows, new_rows], axis=1)
    t_all = rows.shape[1]
    kc = compress_rows(rows[:, :, 0], pe[0], w1[0], w2[0])
    vc = compress_rows(rows[:, :, 1], pe[1], w1[1], w2[1])
    n_slc = -(-t_all // SLC_BLOCK)
    sel = jnp.pad(rows[:, :, 2:4], ((0, 0), (0, n_slc * SLC_BLOCK - t_all), (0, 0), (0, 0), (0, 0)))
    sel = sel.reshape(b, n_slc, SLC_BLOCK, 2, N_KV_HEADS, HEAD_DIM).transpose(3, 0, 4, 1, 2, 5)
    win_src = jnp.concatenate([win_prefix, new_win], axis=1)
    o_br = nsa_core(q, pos0, kc, vc, sel[0], sel[1], win_src[:, :, 0], win_src[:, :, 1])
    gates = jax.nn.sigmoid(gl.astype(F32)).reshape(b, nq, N_Q_HEADS, N_BRANCH)
    o_attn = jnp.einsum('bqhc,bqhcd->bqhd', gates.astype(o_br.dtype), o_br).reshape(b, nq, NSA_WIDTH)
    u = gc * hc
    u_ext = jnp.concatenate([conv_prefix, u], axis=1)
    y_conv = sum(conv_w[j] * u_ext[:, j:j + nq] for j in range(CONV_WIDTH))
    o_conv = gb * y_conv
    n_groups = MIX_WIDTH // HEAD_DIM
    mix = jnp.concatenate([o_attn, o_conv], axis=-1).reshape(b, nq, n_groups, HEAD_DIM)
    mix = rms_norm(mix, out_g.reshape(n_groups, HEAD_DIM)).reshape(b, nq, MIX_WIDTH)
    x = x + mix @ w_out
    x = x + peer_ffn(rms_norm(x, norm2), pwq, psk, pu, pv)
    n_keep = min(WINDOW, n_win_prev + nq)
    return x, new_rows, win_src[:, win_src.shape[1] - n_keep:], u_ext[:, nq:]


def setup_inputs(seed: int = 0) -> dict:
    key = jax.random.key(seed)
    k = jax.random.split(key, 21)

    def normal(kk, shape, scale):
        return jax.random.normal(kk, shape, F32) * scale

    def gain(kk, shape):
        return 1.0 + 0.02 * jax.random.normal(kk, shape, F32)

    n_pages = PAST_LEN // PAGE_SIZE
    n_used = DEC_BATCH * n_pages
    n_phys = n_used + max(1, n_used // 4)
    page_table = jax.random.permutation(k[0], n_phys)[:n_used].reshape(DEC_BATCH, n_pages).astype(jnp.int32)
    win_buf = min(WINDOW, PAST_LEN)
    in_cols = NSA_WIDTH + 6 * KV_WIDTH + N_Q_HEADS * N_BRANCH + 3 * CONV_DIM
    return {
        'x_prompt': normal(k[1], (BATCH, SEQ, D_MODEL), 1.0),
        'x_sample': normal(k[2], (DEC_BATCH, DEC_SEQ, D_MODEL), 1.0),
        'cache_nsa_kv': normal(k[3], (n_phys, PAGE_SIZE, DEPTH, N_KV_ROLES, N_KV_HEADS, HEAD_DIM), 1.0),
        'state_win_kv': normal(k[4], (DEPTH, DEC_BATCH, win_buf, 2, N_KV_HEADS, HEAD_DIM), 1.0),
        'state_conv': normal(k[5], (DEPTH, DEC_BATCH, CONV_WIDTH - 1, CONV_DIM), 1.0),
        'page_table': page_table,
        'norm1_g': gain(k[6], (DEPTH, D_MODEL)),
        'w_in': normal(k[7], (DEPTH, D_MODEL, in_cols), D_MODEL ** -0.5),
        'q_norm_g': gain(k[8], (DEPTH, HEAD_DIM)),
        'k_norm_g': gain(k[9], (DEPTH, N_BRANCH, HEAD_DIM)),
        'cmp_pe': normal(k[10], (DEPTH, 2, CMP_LEN, HEAD_DIM), 0.1),
        'cmp_w1': normal(k[11], (DEPTH, 2, CMP_LEN, HEAD_DIM, HEAD_DIM), (CMP_LEN * HEAD_DIM) ** -0.5),
        'cmp_w2': normal(k[12], (DEPTH, 2, HEAD_DIM, HEAD_DIM), HEAD_DIM ** -0.5),
        'conv_w': normal(k[13], (DEPTH, CONV_WIDTH, CONV_DIM), CONV_WIDTH ** -0.5),
        'out_norm_g': gain(k[14], (DEPTH, MIX_WIDTH)),
        'w_out': normal(k[15], (DEPTH, MIX_WIDTH, D_MODEL), MIX_WIDTH ** -0.5),
        'norm2_g': gain(k[16], (DEPTH, D_MODEL)),
        'peer_wq': normal(k[17], (DEPTH, D_MODEL, PEER_HEADS * PEER_QDIM), D_MODEL ** -0.5),
        'peer_subkeys': normal(k[18], (DEPTH, PEER_HEADS, 2, PEER_NKEYS, PEER_QDIM // 2), (PEER_QDIM // 2) ** -0.5),
        'peer_u': normal(k[19], (DEPTH, PEER_EXPERTS, D_MODEL), D_MODEL ** -0.5),
        'peer_v': normal(k[20], (DEPTH, PEER_EXPERTS, D_MODEL), PEER_V_SCALE),
    }


def reference(x_prompt, x_sample, cache_nsa_kv, state_win_kv, state_conv, page_table,
              norm1_g, w_in, q_norm_g, k_norm_g, cmp_pe, cmp_w1, cmp_w2, conv_w, out_norm_g, w_out,
              norm2_g, peer_wq, peer_subkeys, peer_u, peer_v):
    b_p = x_prompt.shape[0]
    b_s = x_sample.shape[0]
    past_len = page_table.shape[1] * PAGE_SIZE
    win_prev = state_win_kv.shape[2]
    y_prompt = x_prompt
    y_sample = x_sample
    rows_p, rows_s, win_p, win_s, conv_p, conv_s = [], [], [], [], [], []
    empty_past = jnp.zeros((b_p, 0, N_KV_ROLES, N_KV_HEADS, HEAD_DIM), x_prompt.dtype)
    zero_win = jnp.zeros((b_p, WINDOW, 2, N_KV_HEADS, HEAD_DIM), x_prompt.dtype)
    zero_conv = jnp.zeros((b_p, CONV_WIDTH - 1, CONV_DIM), x_prompt.dtype)
    for l in range(DEPTH):
        lp = (norm1_g[l], w_in[l], q_norm_g[l], k_norm_g[l], cmp_pe[l], cmp_w1[l], cmp_w2[l], conv_w[l],
              out_norm_g[l], w_out[l], norm2_g[l], peer_wq[l], peer_subkeys[l], peer_u[l], peer_v[l])
        y_prompt, r_p, w_p, c_p = decoder_layer(y_prompt, 0, empty_past, zero_win, 0, zero_conv, lp)
        past = cache_nsa_kv[page_table, :, l].reshape(b_s, past_len, N_KV_ROLES, N_KV_HEADS, HEAD_DIM)
        w_pref = jnp.pad(state_win_kv[l], ((0, 0), (WINDOW - win_prev, 0), (0, 0), (0, 0), (0, 0)))
        y_sample, r_s, w_s, c_s = decoder_layer(y_sample, past_len, past, w_pref, win_prev, state_conv[l], lp)
        rows_p.append(r_p)
        rows_s.append(r_s)
        win_p.append(w_p)
        win_s.append(w_s)
        conv_p.append(c_p)
        conv_s.append(c_s)
    kv_rows_prompt = jnp.stack(rows_p, axis=2)
    kv_rows_sample = jnp.stack(rows_s, axis=2)
    win_prompt = jnp.stack(win_p, axis=0)
    win_sample = jnp.stack(win_s, axis=0)
    conv_prompt = jnp.stack(conv_p, axis=0)
    conv_sample = jnp.stack(conv_s, axis=0)
    return (y_prompt, y_sample, kv_rows_prompt, kv_rows_sample, win_prompt, win_sample, conv_prompt, conv_sample)
```

```python
import functools
import math

import numpy as np
import jax
import jax.numpy as jnp
from jax import lax
from jax.experimental import pallas as pl
from jax.experimental.pallas import tpu as pltpu

F32 = jnp.float32
BF16 = jnp.bfloat16

LANES = 128
SUBLANES = 8
VMEM_LIMIT = 56 * 1024 * 1024

D_MODEL = 1024
HEAD_DIM = 64
N_Q_HEADS = 8
N_KV_HEADS = 2
HEADS_PER_GROUP = N_Q_HEADS // N_KV_HEADS
NSA_WIDTH = N_Q_HEADS * HEAD_DIM
KV_WIDTH = N_KV_HEADS * HEAD_DIM
CONV_DIM = D_MODEL - NSA_WIDTH
CONV_WIDTH = 3
N_BRANCH = 3
ROPE_DIM = HEAD_DIM // 4
ROPE_HALF = ROPE_DIM // 2
ROPE_THETA = 500000.0
CMP_LEN = 32
CMP_STRIDE = 16
SLC_BLOCK = 64
N_SELECT = 16
WINDOW = 512
Q_BLOCK = 128
PAGE_SIZE = 128
PEER_HEADS = 8
PEER_QDIM = 256
PEER_NKEYS = 128
PEER_TOPK = 16
RMS_EPS = 1e-6
NEG_INF = -1e30
FORCE_BONUS = 1e4

Q_PAD = N_Q_HEADS * LANES
IN_COLS = Q_PAD + 6 * KV_WIDTH + 3 * CONV_DIM + LANES
MIX_PAD = Q_PAD + CONV_DIM
SAMPLE_T = 8

_NT = (((1,), (1,)), ((), ()))


def _cparams(sem):
    return pltpu.CompilerParams(dimension_semantics=sem, vmem_limit_bytes=VMEM_LIMIT)


def _dot(a, b):
    return jnp.dot(a, b, preferred_element_type=F32)


def _dot_nt(a, b):
    return lax.dot_general(a, b, _NT, preferred_element_type=F32)


def _group_norm(x, gmat, gain):
    ss = _dot((x * x).astype(BF16), gmat)
    return x * lax.rsqrt(ss + RMS_EPS) * gain


def _in_kernel(x_ref, g1_ref, w_ref, gm_ref, qg_ref, kg_ref, c_ref, s1_ref, s2_ref,
               q_ref, rows_ref, win_ref, gate_ref, gb_ref, u_ref):
    x = x_ref[...]
    ms = jnp.mean(x * x, axis=-1, keepdims=True)
    h = (x * lax.rsqrt(ms + RMS_EPS) * g1_ref[...]).astype(BF16)
    z = _dot(h, w_ref[...])
    gmat = gm_ref[...]
    cos = c_ref[...]
    sin_up = s1_ref[...]
    sin_dn = s2_ref[...]

    def norm_rope(zb, gain):
        y = _group_norm(zb, gmat, gain)
        return (y * cos + pltpu.roll(y, LANES - ROPE_HALF, 1) * sin_up
                + pltpu.roll(y, ROPE_HALF, 1) * sin_dn)

    for hb in range(N_Q_HEADS):
        q_ref[:, hb * LANES:(hb + 1) * LANES] = norm_rope(z[:, hb * LANES:(hb + 1) * LANES], qg_ref[...])
    o = Q_PAD
    kv = [z[:, o + r * LANES:o + (r + 1) * LANES] for r in range(6)]
    rows_ref[:, 0 * LANES:1 * LANES] = norm_rope(kv[0], kg_ref[0:1, :])
    rows_ref[:, 1 * LANES:2 * LANES] = kv[1]
    rows_ref[:, 2 * LANES:3 * LANES] = norm_rope(kv[2], kg_ref[1:2, :])
    rows_ref[:, 3 * LANES:4 * LANES] = kv[3]
    win_ref[:, 0:LANES] = norm_rope(kv[4], kg_ref[2:3, :])
    win_ref[:, LANES:2 * LANES] = kv[5]
    o += 6 * LANES
    gb_ref[...] = z[:, o:o + CONV_DIM]
    u_ref[...] = z[:, o + CONV_DIM:o + 2 * CONV_DIM] * z[:, o + 2 * CONV_DIM:o + 3 * CONV_DIM]
    o += 3 * CONV_DIM
    gate_ref[...] = jax.nn.sigmoid(z[:, o:o + LANES])


def _in_proj(x, lw, tabs, tm):
    n = x.shape[0]
    cos, sin_up, sin_dn = tabs
    npos_tiles = cos.shape[0] // tm
    row = lambda i: (i, 0)
    fixed = lambda i: (0, 0)
    pos = lambda i: (i % npos_tiles, 0)
    out_shape = (
        jax.ShapeDtypeStruct((n, Q_PAD), F32),
        jax.ShapeDtypeStruct((n, 4 * LANES), F32),
        jax.ShapeDtypeStruct((n, 2 * LANES), F32),
        jax.ShapeDtypeStruct((n, LANES), F32),
        jax.ShapeDtypeStruct((n, CONV_DIM), F32),
        jax.ShapeDtypeStruct((n, CONV_DIM), F32),
    )
    return pl.pallas_call(
        _in_kernel,
        out_shape=out_shape,
        grid=(n // tm,),
        in_specs=[
            pl.BlockSpec((tm, D_MODEL), row),
            pl.BlockSpec((1, D_MODEL), fixed),
            pl.BlockSpec((D_MODEL, IN_COLS), fixed),
            pl.BlockSpec((LANES, LANES), fixed),
            pl.BlockSpec((1, LANES), fixed),
            pl.BlockSpec((SUBLANES, LANES), fixed),
            pl.BlockSpec((tm, LANES), pos),
            pl.BlockSpec((tm, LANES), pos),
            pl.BlockSpec((tm, LANES), pos),
        ],
        out_specs=(
            pl.BlockSpec((tm, Q_PAD), row),
            pl.BlockSpec((tm, 4 * LANES), row),
            pl.BlockSpec((tm, 2 * LANES), row),
            pl.BlockSpec((tm, LANES), row),
            pl.BlockSpec((tm, CONV_DIM), row),
            pl.BlockSpec((tm, CONV_DIM), row),
        ),
        compiler_params=_cparams(("parallel",)),
        name="in_proj",
    )(x, lw["norm1"], lw["w_in"], lw["gmat"], lw["q_g"], lw["k_g"], cos, sin_up, sin_dn)


def _compress(src_ref, nch, w1_ref, pe_ref, w2_ref, role):
    acc0 = jnp.zeros((nch, LANES), F32)
    acc1 = jnp.zeros((nch, LANES), F32)
    for s2 in range(CMP_STRIDE // 2):
        xs = []
        for s in (2 * s2, 2 * s2 + 1):
            xs.append(src_ref[pl.ds(s, nch, stride=CMP_STRIDE), :])
        x0 = jnp.concatenate([xs[0] + pe_ref[role, 0, 2 * s2:2 * s2 + 1, :],
                              xs[1] + pe_ref[role, 0, 2 * s2 + 1:2 * s2 + 2, :]], axis=1)
        x1 = jnp.concatenate([xs[0] + pe_ref[role, 1, 2 * s2:2 * s2 + 1, :],
                              xs[1] + pe_ref[role, 1, 2 * s2 + 1:2 * s2 + 2, :]], axis=1)
        acc0 = acc0 + _dot(x0.astype(BF16), w1_ref[role, 0, s2])
        acc1 = acc1 + _dot(x1.astype(BF16), w1_ref[role, 1, s2])
    hid = acc0 + pltpu.roll(acc1, nch - 1, 0)
    return _dot(jax.nn.gelu(hid).astype(BF16), w2_ref[role])


def _cmp_kernel(k_ref, v_ref, w1_ref, pe_ref, w2_ref, kc_ref, vc_ref, *, nch):
    kc_ref[0] = _compress(k_ref, nch, w1_ref, pe_ref, w2_ref, 0)
    vc_ref[0] = _compress(v_ref, nch, w1_ref, pe_ref, w2_ref, 1)


def _compress_prompt(rows, lw, b, t):
    nch = t // CMP_STRIDE
    fixed = lambda i: (0,) * 5
    return pl.pallas_call(
        functools.partial(_cmp_kernel, nch=nch),
        out_shape=(jax.ShapeDtypeStruct((b, nch, LANES), F32),) * 2,
        grid=(b,),
        in_specs=[
            pl.BlockSpec((t, LANES), lambda i: (i, 0)),
            pl.BlockSpec((t, LANES), lambda i: (i, 1)),
            pl.BlockSpec(lw["cmp_w1"].shape, fixed),
            pl.BlockSpec(lw["cmp_pe"].shape, lambda i: (0,) * 4),
            pl.BlockSpec(lw["cmp_w2"].shape, lambda i: (0,) * 3),
        ],
        out_specs=(pl.BlockSpec((1, nch, LANES), lambda i: (i, 0, 0)),) * 2,
        compiler_params=_cparams(("parallel",)),
        name="compress_prompt",
    )(rows, rows, lw["cmp_w1"], lw["cmp_pe"], lw["cmp_w2"])


def _masked_softmax(s, mask):
    s = jnp.where(mask, s, NEG_INF)
    p = jnp.where(mask, jnp.exp(s - jnp.max(s, axis=-1, keepdims=True)), 0.0)
    return p / jnp.maximum(jnp.sum(p, axis=-1, keepdims=True), 1e-30)


def _split3(x):
    h1 = x.astype(BF16)
    r1 = x - h1.astype(F32)
    h2 = r1.astype(BF16)
    h3 = (r1 - h2.astype(F32)).astype(BF16)
    return h1, h2, h3


def _attn_core(q, qpos, nq, kc, vc, k_slc, v_slc, slc_len, k_win, v_win, wpos0, gates,
               msel, eexp, n_slc):
    nrow = N_Q_HEADS * nq
    grow = HEADS_PER_GROUP * nq
    qb = q.astype(BF16)
    qpos_rows = jnp.concatenate([qpos] * N_Q_HEADS, axis=0)

    nch = kc.shape[0]
    s = _dot_nt(qb, kc.astype(BF16))
    cend = lax.broadcasted_iota(jnp.int32, (nrow, nch), 1) * CMP_STRIDE + (CMP_LEN - 1)
    p = _masked_softmax(s, cend <= qpos_rows)
    o_cmp = _dot(p.astype(BF16), vc.astype(BF16))

    jn = msel.shape[1]
    jid = lax.broadcasted_iota(jnp.int32, (nq, jn), 1)
    cur = qpos // SLC_BLOCK
    valid = jid * SLC_BLOCK <= qpos
    forced = (jid == 0) | (jid == cur) | (jid == cur - 1)
    allowed = []
    for g in range(N_KV_HEADS):
        pg = p[g * grow:g * grow + nq]
        for hh in range(1, HEADS_PER_GROUP):
            pg = pg + p[g * grow + hh * nq:g * grow + (hh + 1) * nq]
        p_slc = sum(_dot(part, msel) for part in _split3(pg))
        score = jnp.where(valid, p_slc + jnp.where(forced, FORCE_BONUS, 0.0), NEG_INF)
        score = jnp.where(jid < n_slc, score, -3e38)
        rank = jnp.zeros((nq, jn), F32)
        for i in range(n_slc):
            col = score[:, i:i + 1]
            ahead = (col > score) | ((col == score) & (jid > i))
            rank = rank + jnp.where(ahead, 1.0, 0.0)
        sel = jnp.where(rank < float(min(N_SELECT, n_slc)), 1.0, 0.0).astype(BF16)
        allowed.append(_dot(sel, eexp))

    kpos = lax.broadcasted_iota(jnp.int32, (nq, slc_len), 1)
    causal = kpos <= qpos
    ksb = k_slc.astype(BF16)
    vsb = v_slc.astype(BF16)
    o_slc = []
    for g in range(N_KV_HEADS):
        mask_g = causal & (allowed[g] > 0.5)
        mask = jnp.concatenate([mask_g] * HEADS_PER_GROUP, axis=0)
        s = _dot_nt(qb[g * grow:(g + 1) * grow], ksb)
        o_slc.append(_dot(_masked_softmax(s, mask).astype(BF16), vsb))
    o_slc = jnp.concatenate(o_slc, axis=0)

    wlen = k_win.shape[0]
    wpos = wpos0 + lax.broadcasted_iota(jnp.int32, (nrow, wlen), 1)
    wmask = (wpos <= qpos_rows) & (wpos > qpos_rows - WINDOW)
    s = _dot_nt(qb, k_win.astype(BF16))
    o_win = _dot(_masked_softmax(s, wmask).astype(BF16), v_win.astype(BF16))

    lane = lax.broadcasted_iota(jnp.int32, (nq, LANES), 1)
    outs = []
    for h in range(N_Q_HEADS):
        r0 = h * nq
        o = (gates[:, 3 * h:3 * h + 1] * o_cmp[r0:r0 + nq]
             + gates[:, 3 * h + 1:3 * h + 2] * o_slc[r0:r0 + nq]
             + gates[:, 3 * h + 2:3 * h + 3] * o_win[r0:r0 + nq])
        g = h // HEADS_PER_GROUP
        outs.append(jnp.where((lane >= g * HEAD_DIM) & (lane < (g + 1) * HEAD_DIM), o, 0.0))
    return outs


def _attn_prompt_kernel(q_ref, rows_ref, win_ref, kc_ref, vc_ref, gate_ref, msel_ref, eexp_ref,
                        o_ref, *, t, n_slc):
    start = pl.program_id(1) * Q_BLOCK
    scale = HEAD_DIM ** -0.5
    q = jnp.concatenate([q_ref[:, h * LANES:(h + 1) * LANES] for h in range(N_Q_HEADS)], axis=0) * scale
    qpos = start + lax.broadcasted_iota(jnp.int32, (Q_BLOCK, 1), 0)
    w0 = pl.multiple_of(jnp.maximum(start - WINDOW, 0), Q_BLOCK)
    wlen = WINDOW + Q_BLOCK
    outs = _attn_core(
        q, qpos, Q_BLOCK, kc_ref[0], vc_ref[0],
        rows_ref[:, 2 * LANES:3 * LANES], rows_ref[:, 3 * LANES:4 * LANES], t,
        win_ref[pl.ds(w0, wlen), 0:LANES], win_ref[pl.ds(w0, wlen), LANES:2 * LANES], w0,
        gate_ref[...], msel_ref[...], eexp_ref[...], n_slc)
    for h in range(N_Q_HEADS):
        o_ref[:, h * LANES:(h + 1) * LANES] = outs[h]


def _attn_prompt(q, rows, win, kc, vc, gates, consts, b, t):
    nb = t // Q_BLOCK
    nch = t // CMP_STRIDE
    msel, eexp, n_slc = consts
    tile = lambda i, j: (i * nb + j, 0)
    batch = lambda i, j: (i, 0)
    return pl.pallas_call(
        functools.partial(_attn_prompt_kernel, t=t, n_slc=n_slc),
        out_shape=jax.ShapeDtypeStruct((b * t, Q_PAD), F32),
        grid=(b, nb),
        in_specs=[
            pl.BlockSpec((Q_BLOCK, Q_PAD), tile),
            pl.BlockSpec((t, 4 * LANES), batch),
            pl.BlockSpec((t, 2 * LANES), batch),
            pl.BlockSpec((1, nch, LANES), lambda i, j: (i, 0, 0)),
            pl.BlockSpec((1, nch, LANES), lambda i, j: (i, 0, 0)),
            pl.BlockSpec((Q_BLOCK, LANES), tile),
            pl.BlockSpec(msel.shape, lambda i, j: (0, 0)),
            pl.BlockSpec(eexp.shape, lambda i, j: (0, 0)),
        ],
        out_specs=pl.BlockSpec((Q_BLOCK, Q_PAD), tile),
        compiler_params=_cparams(("parallel", "parallel")),
        name="attn_prompt",
    )(q, rows, win, kc, vc, gates, msel, eexp)


def _attn_sample_kernel(pt_ref, q_ref, new_ref, nwin_ref, state_ref, gate_ref, cache_ref,
                        w1_ref, pe_ref, w2_ref, msel_ref, eexp_ref, o_ref, past_ref, sem,
                        *, layer, n_pages, n_slc):
    b = pl.program_id(0)
    past_len = n_pages * PAGE_SIZE
    slc_len = past_len + LANES

    def page_copy(pg, role):
        return pltpu.make_async_copy(
            cache_ref.at[pt_ref[b, pg], :, layer * 4 + role, :],
            past_ref.at[role, pl.ds(pg * PAGE_SIZE, PAGE_SIZE), :],
            sem.at[role])

    for pg in range(n_pages):
        for role in range(4):
            page_copy(pg, role).start()
    new = new_ref[0]
    for role in range(4):
        past_ref[role, pl.ds(past_len, SAMPLE_T), :] = new[:, role * LANES:(role + 1) * LANES]
        past_ref[role, pl.ds(past_len + SAMPLE_T, LANES - SAMPLE_T), :] = jnp.zeros(
            (LANES - SAMPLE_T, LANES), F32)
    for pg in range(n_pages):
        for role in range(4):
            page_copy(pg, role).wait()

    nch = past_len // CMP_STRIDE
    kc = _compress(past_ref.at[0], nch, w1_ref, pe_ref, w2_ref, 0)
    vc = _compress(past_ref.at[1], nch, w1_ref, pe_ref, w2_ref, 1)

    scale = HEAD_DIM ** -0.5
    qv = q_ref[0]
    q = jnp.concatenate([qv[:, h * LANES:(h + 1) * LANES] for h in range(N_Q_HEADS)], axis=0) * scale
    qpos = past_len + lax.broadcasted_iota(jnp.int32, (SAMPLE_T, 1), 0)
    pad = jnp.zeros((LANES - SAMPLE_T, LANES), F32)
    nw = nwin_ref[0]
    st = state_ref[0, 0]
    k_win = jnp.concatenate([st[:, 0:LANES], nw[:, 0:LANES], pad], axis=0)
    v_win = jnp.concatenate([st[:, LANES:2 * LANES], nw[:, LANES:2 * LANES], pad], axis=0)
    outs = _attn_core(
        q, qpos, SAMPLE_T, kc, vc,
        past_ref[2], past_ref[3], slc_len,
        k_win, v_win, past_len - WINDOW,
        gate_ref[0], msel_ref[...], eexp_ref[...], n_slc)
    for h in range(N_Q_HEADS):
        o_ref[0, :, h * LANES:(h + 1) * LANES] = outs[h]


def _attn_sample(page_table, q, new_rows, new_win, state_win, gates, cache, lw, consts, layer):
    bs, n_pages = page_table.shape
    msel, eexp, n_slc = consts
    past_len = n_pages * PAGE_SIZE
    per_b = lambda i, pt: (i, 0, 0)
    grid_spec = pltpu.PrefetchScalarGridSpec(
        num_scalar_prefetch=1,
        grid=(bs,),
        in_specs=[
            pl.BlockSpec((1, SAMPLE_T, Q_PAD), per_b),
            pl.BlockSpec((1, SAMPLE_T, 4 * LANES), per_b),
            pl.BlockSpec((1, SAMPLE_T, 2 * LANES), per_b),
            pl.BlockSpec((1, 1, WINDOW, 2 * LANES), lambda i, pt: (layer, i, 0, 0)),
            pl.BlockSpec((1, SAMPLE_T, LANES), per_b),
            pl.BlockSpec(memory_space=pl.ANY),
            pl.BlockSpec(lw["cmp_w1"].shape, lambda i, pt: (0,) * 5),
            pl.BlockSpec(lw["cmp_pe"].shape, lambda i, pt: (0,) * 4),
            pl.BlockSpec(lw["cmp_w2"].shape, lambda i, pt: (0,) * 3),
            pl.BlockSpec(msel.shape, lambda i, pt: (0, 0)),
            pl.BlockSpec(eexp.shape, lambda i, pt: (0, 0)),
        ],
        out_specs=pl.BlockSpec((1, SAMPLE_T, Q_PAD), per_b),
        scratch_shapes=[
            pltpu.VMEM((4, past_len + LANES, LANES), F32),
            pltpu.SemaphoreType.DMA((4,)),
        ],
    )
    return pl.pallas_call(
        functools.partial(_attn_sample_kernel, layer=layer, n_pages=n_pages, n_slc=n_slc),
        out_shape=jax.ShapeDtypeStruct((bs, SAMPLE_T, Q_PAD), F32),
        grid_spec=grid_spec,
        compiler_params=_cparams(("arbitrary",)),
        name="attn_sample",
    )(page_table, q, new_rows, new_win, state_win, gates, cache,
      lw["cmp_w1"], lw["cmp_pe"], lw["cmp_w2"], msel, eexp)


def _out_kernel(oa_ref, gb_ref, u_ref, p1_ref, p2_ref, cw_ref, x_ref, gm_ref, og_ref, w_ref, n2_ref,
                x1_ref, h2_ref):
    gmat = gm_ref[...]
    y_conv = cw_ref[0:1, :] * p2_ref[...] + cw_ref[1:2, :] * p1_ref[...] + cw_ref[2:3, :] * u_ref[...]
    o_conv = gb_ref[...] * y_conv
    parts = []
    for k in range(N_Q_HEADS):
        parts.append(_group_norm(oa_ref[:, k * LANES:(k + 1) * LANES], gmat,
                                 og_ref[:, k * LANES:(k + 1) * LANES]))
    for k in range(CONV_DIM // LANES):
        parts.append(_group_norm(o_conv[:, k * LANES:(k + 1) * LANES], gmat,
                                 og_ref[:, Q_PAD + k * LANES:Q_PAD + (k + 1) * LANES]))
    mix = jnp.concatenate(parts, axis=1).astype(BF16)
    x1 = x_ref[...] + _dot(mix, w_ref[...])
    x1_ref[...] = x1
    ms = jnp.mean(x1 * x1, axis=-1, keepdims=True)
    h2_ref[...] = (x1 * lax.rsqrt(ms + RMS_EPS) * n2_ref[...]).astype(BF16)


def _out_proj(o_attn, gb, u, prev1, prev2, x, lw, tm):
    n = x.shape[0]
    row = lambda i: (i, 0)
    fixed = lambda i: (0, 0)
    return pl.pallas_call(
        _out_kernel,
        out_shape=(jax.ShapeDtypeStruct((n, D_MODEL), F32), jax.ShapeDtypeStruct((n, D_MODEL), BF16)),
        grid=(n // tm,),
        in_specs=[
            pl.BlockSpec((tm, Q_PAD), row),
            pl.BlockSpec((tm, CONV_DIM), row),
            pl.BlockSpec((tm, CONV_DIM), row),
            pl.BlockSpec((tm, CONV_DIM), row),
            pl.BlockSpec((tm, CONV_DIM), row),
            pl.BlockSpec((SUBLANES, CONV_DIM), fixed),
            pl.BlockSpec((tm, D_MODEL), row),
            pl.BlockSpec((LANES, LANES), fixed),
            pl.BlockSpec((1, MIX_PAD), fixed),
            pl.BlockSpec((MIX_PAD, D_MODEL), fixed),
            pl.BlockSpec((1, D_MODEL), fixed),
        ],
        out_specs=(pl.BlockSpec((tm, D_MODEL), row), pl.BlockSpec((tm, D_MODEL), row)),
        compiler_params=_cparams(("parallel",)),
        name="out_proj",
    )(o_attn, gb, u, prev1, prev2, lw["conv_w"], x, lw["gmat"], lw["out_g"], lw["w_out"], lw["norm2"])


_CAND_ROWS = 80


def _route_kernel(h_ref, wq_ref, sk_ref, a1_ref, cnt_ref, b_ref, r2_ref,
                  qt_ref, s_ref, so_ref, rank_ref, v_ref, idx_ref, cand_ref):
    tn = h_ref.shape[0]
    k = PEER_TOPK
    qt_ref[...] = _dot_nt(wq_ref[...], h_ref[...])
    rowid = lax.broadcasted_iota(jnp.int32, (PEER_NKEYS, tn), 0).astype(F32)

    crow = lax.broadcasted_iota(jnp.int32, (_CAND_ROWS, tn), 0)
    cr = jnp.where(crow < 16, 0, jnp.where(crow < 72, 1 + (crow - 16) // 8, crow - 64))
    cj = jnp.where(crow < 16, crow, jnp.where(crow < 72, (crow - 16) % 8, 0))
    cvalid = (cr + 1) * (cj + 1) <= k
    cflat = (cr * k + cj).astype(F32)
    rid16 = lax.broadcasted_iota(jnp.int32, (k, tn), 0).astype(F32)
    neg = -jnp.inf

    def head_body(h, carry):
        for p in range(2):
            base = pl.multiple_of(h * PEER_QDIM + p * (PEER_QDIM // 2), PEER_QDIM // 2)
            s0 = _dot(sk_ref[2 * h + p], qt_ref[pl.ds(base, PEER_QDIM // 2), :].astype(BF16))
            so_ref[p] = s0
            s_ref[...] = s0
            rank_ref[p] = jnp.full((PEER_NKEYS, tn), float(k), F32)
            for r in range(k):
                s = s_ref[...]
                m = jnp.max(s, axis=0, keepdims=True)
                idx = jnp.min(jnp.where(s == m, rowid, float(PEER_NKEYS)), axis=0, keepdims=True)
                hit = rowid == idx
                s_ref[...] = jnp.where(hit, neg, s)
                rank_ref[p] = jnp.where(hit, float(r), rank_ref[p])
                v_ref[p, r:r + 1, :] = m
                idx_ref[p, r:r + 1, :] = idx
        v1 = v_ref[0]
        v2 = v_ref[1]
        cand_ref[0:16, :] = v1[0:1] + v2
        for r in range(1, 8):
            cand_ref[8 + 8 * r:16 + 8 * r, :] = v1[r:r + 1] + v2[0:8]
        cand_ref[72:80, :] = v1[8:16] + v2[0:1]
        cand_ref[...] = jnp.where(cvalid, cand_ref[...], neg)
        cmax = v1[0:1] + v2[0:1]
        cnt = jnp.zeros((k, tn), F32)
        zsum = jnp.zeros((1, tn), F32)
        for _ in range(k):
            c = cand_ref[...]
            m = jnp.max(c, axis=0, keepdims=True)
            f = jnp.min(jnp.where(c == m, cflat, 1e9), axis=0, keepdims=True)
            cand_ref[...] = jnp.where(cflat == f, neg, c)
            cnt = cnt + jnp.where(rid16 == jnp.floor(f * (1.0 / k)), 1.0, 0.0)
            zsum = zsum + jnp.exp(m - cmax)
        r1 = rank_ref[0]
        r2 = rank_ref[1]
        a1_ref[h] = jnp.where(r1 < float(k), jnp.exp(so_ref[0] - v1[0:1]), 0.0)
        b_ref[h] = jnp.where(r2 < float(k), jnp.exp(so_ref[1] - v2[0:1]), 0.0) / zsum
        r2_ref[h] = r2
        cdense = jnp.zeros((PEER_NKEYS, tn), F32)
        for r in range(k):
            cdense = jnp.where(r1 == float(r), cnt[r:r + 1], cdense)
        cnt_ref[h] = cdense
        return carry

    lax.fori_loop(0, PEER_HEADS, head_body, 0)


def _route(h2, lw, tn):
    n = h2.shape[0]
    shape = jax.ShapeDtypeStruct((PEER_HEADS, PEER_NKEYS, n), F32)
    spec = pl.BlockSpec((PEER_HEADS, PEER_NKEYS, tn), lambda i: (0, 0, i))
    return pl.pallas_call(
        _route_kernel,
        out_shape=(shape,) * 4,
        grid=(n // tn,),
        in_specs=[
            pl.BlockSpec((tn, D_MODEL), lambda i: (i, 0)),
            pl.BlockSpec((PEER_HEADS * PEER_QDIM, D_MODEL), lambda i: (0, 0)),
            pl.BlockSpec((2 * PEER_HEADS, PEER_NKEYS, PEER_QDIM // 2), lambda i: (0, 0, 0)),
        ],
        out_specs=(spec,) * 4,
        scratch_shapes=[
            pltpu.VMEM((PEER_HEADS * PEER_QDIM, tn), F32),
            pltpu.VMEM((PEER_NKEYS, tn), F32),
            pltpu.VMEM((2, PEER_NKEYS, tn), F32),
            pltpu.VMEM((2, PEER_NKEYS, tn), F32),
            pltpu.VMEM((2, PEER_TOPK, tn), F32),
            pltpu.VMEM((2, PEER_TOPK, tn), F32),
            pltpu.VMEM((_CAND_ROWS, tn), F32),
        ],
        compiler_params=_cparams(("parallel",)),
        name="peer_route",
    )(h2, lw["peer_wq_t"], lw["peer_sk"])


def _peer_kernel(h_ref, u_ref, vt_ref, a1_ref, cnt_ref, b_ref, r2_ref, x_ref, y_ref,
                 acc_ref, st_ref, at_ref, *, eb):
    e = pl.program_id(1)
    tn = h_ref.shape[0]

    @pl.when(e == 0)
    def _():
        acc_ref[...] = jnp.zeros_like(acc_ref)

    st_ref[...] = _dot_nt(u_ref[...], h_ref[...])
    for a in range(eb // PEER_NKEYS):
        for c in range(tn // LANES):
            cols = slice(c * LANES, (c + 1) * LANES)
            rows = slice(a * PEER_NKEYS, (a + 1) * PEER_NKEYS)
            w = jnp.zeros((PEER_NKEYS, LANES), F32)
            for h in range(PEER_HEADS):
                cnt = cnt_ref[h, a:a + 1, cols]
                a1 = a1_ref[h, a:a + 1, cols]
                w = w + a1 * jnp.where(r2_ref[h, :, cols] < cnt, b_ref[h, :, cols], 0.0)
            at_ref[rows, cols] = (w * jax.nn.gelu(st_ref[rows, cols])).astype(BF16)
    acc_ref[...] += _dot(vt_ref[...], at_ref[...])

    @pl.when(e == pl.num_programs(1) - 1)
    def _():
        y_ref[...] = x_ref[...] + acc_ref[...].T


def _peer(h2, routed, x1, lw, tn, eb):
    n = h2.shape[0]
    n_exp = PEER_NKEYS * PEER_NKEYS
    a1, cnt, bb, r2 = routed
    tok = lambda i, e: (i, 0)
    key_rows = pl.BlockSpec((PEER_HEADS, eb // PEER_NKEYS, tn), lambda i, e: (0, e, i))
    key_full = pl.BlockSpec((PEER_HEADS, PEER_NKEYS, tn), lambda i, e: (0, 0, i))
    return pl.pallas_call(
        functools.partial(_peer_kernel, eb=eb),
        out_shape=jax.ShapeDtypeStruct((n, D_MODEL), F32),
        grid=(n // tn, n_exp // eb),
        in_specs=[
            pl.BlockSpec((tn, D_MODEL), tok),
            pl.BlockSpec((eb, D_MODEL), lambda i, e: (e, 0)),
            pl.BlockSpec((D_MODEL, eb), lambda i, e: (0, e)),
            key_rows, key_rows, key_full, key_full,
            pl.BlockSpec((tn, D_MODEL), tok),
        ],
        out_specs=pl.BlockSpec((tn, D_MODEL), tok),
        scratch_shapes=[
            pltpu.VMEM((D_MODEL, tn), F32),
            pltpu.VMEM((eb, tn), F32),
            pltpu.VMEM((eb, tn), BF16),
        ],
        compiler_params=_cparams(("parallel", "arbitrary")),
        name="peer_experts",
    )(h2, lw["peer_u"], lw["peer_vt"], a1, cnt, bb, r2, x1)


def _rope_tables(pos):
    inv_freq = ROPE_THETA ** (-jnp.arange(ROPE_HALF, dtype=F32) / ROPE_HALF)
    ang = pos.astype(F32)[:, None] * inv_freq[None, :]
    cos, sin = jnp.cos(ang), jnp.sin(ang)
    n = pos.shape[0]
    ones = jnp.ones((n, HEAD_DIM - ROPE_DIM), F32)
    zeros = jnp.zeros((n, HEAD_DIM - ROPE_DIM), F32)
    zh = jnp.zeros((n, ROPE_HALF), F32)
    c = jnp.concatenate([cos, cos, ones], axis=1)
    s_up = jnp.concatenate([-sin, zh, zeros], axis=1)
    s_dn = jnp.concatenate([zh, sin, zeros], axis=1)
    return tuple(jnp.tile(a, (1, 2)) for a in (c, s_up, s_dn))


def _attn_consts(n_cmp, nch, n_slc, slc_len):
    jn = -(-n_slc // LANES) * LANES
    sr = SLC_BLOCK // CMP_STRIDE
    r = CMP_LEN // CMP_STRIDE
    msel = np.zeros((nch, jn), np.float32)
    for j in range(n_slc):
        for m in range(sr):
            for n in range(r):
                i = sr * j + m - n
                if 0 <= i < n_cmp:
                    msel[i, j] += 1.0
    eexp = np.zeros((jn, slc_len), np.float32)
    keys = np.arange(slc_len)
    eexp[keys // SLC_BLOCK, keys] = 1.0
    return jnp.asarray(msel, BF16), jnp.asarray(eexp, BF16), n_slc


def _layer_weights(l, norm1_g, w_in, q_norm_g, k_norm_g, cmp_pe, cmp_w1, cmp_w2, conv_w, out_norm_g,
                   w_out, norm2_g, peer_wq, peer_subkeys, peer_u, peer_v):
    c1 = NSA_WIDTH
    c2 = c1 + 6 * KV_WIDTH
    c3 = c2 + N_Q_HEADS * N_BRANCH
    w = w_in[l]
    half = jnp.zeros((D_MODEL, HEAD_DIM), F32)
    q_cols = []
    for h in range(N_Q_HEADS):
        wh = w[:, h * HEAD_DIM:(h + 1) * HEAD_DIM]
        q_cols += [wh, half] if h < HEADS_PER_GROUP else [half, wh]
    gl_pad = jnp.zeros((D_MODEL, LANES - N_Q_HEADS * N_BRANCH), F32)
    w_all = jnp.concatenate(q_cols + [w[:, c1:c2], w[:, c3:], w[:, c2:c3], gl_pad], axis=1)

    blk = np.kron(np.eye(2, dtype=np.float32), np.full((HEAD_DIM, HEAD_DIM), 1.0 / HEAD_DIM, np.float32))
    kg = jnp.tile(k_norm_g[l], (1, 2))
    kg = jnp.concatenate([kg, jnp.zeros((SUBLANES - N_BRANCH, LANES), F32)], axis=0)

    r = CMP_LEN // CMP_STRIDE
    w1 = cmp_w1[l].reshape(2, r, CMP_STRIDE, HEAD_DIM, HEAD_DIM)
    eye2 = jnp.eye(2, dtype=F32)
    w1 = jnp.einsum('xmsde,gk->xmsgdke', w1, eye2).reshape(2, r, CMP_STRIDE // 2, 2 * LANES, LANES)
    pe = jnp.tile(cmp_pe[l].reshape(2, r, CMP_STRIDE, HEAD_DIM), (1, 1, 1, 2))
    w2 = jnp.einsum('xde,gk->xgdke', cmp_w2[l], eye2).reshape(2, LANES, LANES)

    og = out_norm_g[l]
    zero_h = jnp.zeros((HEAD_DIM,), F32)
    og_parts, wo_rows = [], []
    zero_rows = jnp.zeros((HEAD_DIM, D_MODEL), F32)
    for h in range(N_Q_HEADS):
        gh = og[h * HEAD_DIM:(h + 1) * HEAD_DIM]
        wr = w_out[l][h * HEAD_DIM:(h + 1) * HEAD_DIM]
        og_parts += [gh, zero_h] if h < HEADS_PER_GROUP else [zero_h, gh]
        wo_rows += [wr, zero_rows] if h < HEADS_PER_GROUP else [zero_rows, wr]
    og_pad = jnp.concatenate(og_parts + [og[NSA_WIDTH:]])[None, :]
    wo_pad = jnp.concatenate(wo_rows + [w_out[l][NSA_WIDTH:]], axis=0)

    cw = jnp.concatenate([conv_w[l], jnp.zeros((SUBLANES - CONV_WIDTH, CONV_DIM), F32)], axis=0)
    return {
        "norm1": norm1_g[l][None, :],
        "w_in": w_all.astype(BF16),
        "gmat": jnp.asarray(blk, BF16),
        "q_g": jnp.tile(q_norm_g[l], 2)[None, :],
        "k_g": kg,
        "cmp_w1": w1.astype(BF16),
        "cmp_pe": pe,
        "cmp_w2": w2.astype(BF16),
        "conv_w": cw,
        "out_g": og_pad,
        "w_out": wo_pad.astype(BF16),
        "norm2": norm2_g[l][None, :],
        "peer_wq_t": peer_wq[l].T.astype(BF16),
        "peer_sk": peer_subkeys[l].reshape(2 * PEER_HEADS, PEER_NKEYS, PEER_QDIM // 2).astype(BF16),
        "peer_u": peer_u[l].astype(BF16),
        "peer_vt": peer_v[l].T.astype(BF16),
    }


def _conv_prev(u, prefix, b, t):
    ext = jnp.concatenate([prefix, u.reshape(b, t, CONV_DIM)], axis=1)
    return ext[:, 1:t + 1].reshape(b * t, CONV_DIM), ext[:, 0:t].reshape(b * t, CONV_DIM)


def _row_tile(n):
    for tm in (256, 128):
        if n % tm == 0:
            return tm
    raise ValueError(f"token count {n} must be a multiple of 128")


def _peer_tile(n):
    for tn in (512, 256, 128):
        if n % tn == 0:
            return tn
    raise ValueError(f"token count {n} must be a multiple of 128")


def kernel(x_prompt, x_sample, cache_nsa_kv, state_win_kv, state_conv, page_table,
           norm1_g, w_in, q_norm_g, k_norm_g, cmp_pe, cmp_w1, cmp_w2, conv_w, out_norm_g, w_out,
           norm2_g, peer_wq, peer_subkeys, peer_u, peer_v):
    bp, t, _ = x_prompt.shape
    bs, ts, _ = x_sample.shape
    depth = norm1_g.shape[0]
    n_pages = page_table.shape[1]
    past_len = n_pages * PAGE_SIZE
    assert t % Q_BLOCK == 0 and t >= WINDOW + Q_BLOCK and ts <= SAMPLE_T
    assert state_win_kv.shape[2] == WINDOW and past_len % SLC_BLOCK == 0

    xp = x_prompt.reshape(bp * t, D_MODEL)
    xs = jnp.pad(x_sample, ((0, 0), (0, SAMPLE_T - ts), (0, 0))).reshape(bs * SAMPLE_T, D_MODEL)
    cache = cache_nsa_kv.reshape(cache_nsa_kv.shape[0], PAGE_SIZE, depth * 4, LANES)
    state_win = state_win_kv.reshape(depth, bs, WINDOW, 2 * LANES)

    tabs_p = _rope_tables(jnp.arange(t))
    tabs_s = _rope_tables(jnp.tile(past_len + jnp.arange(SAMPLE_T), bs))
    n_cmp_p = (t - CMP_LEN) // CMP_STRIDE + 1
    consts_p = _attn_consts(n_cmp_p, t // CMP_STRIDE, t // SLC_BLOCK, t)
    t_all = past_len + ts
    n_cmp_s = (t_all - CMP_LEN) // CMP_STRIDE + 1
    assert n_cmp_s <= past_len // CMP_STRIDE
    consts_s = _attn_consts(n_cmp_s, past_len // CMP_STRIDE, -(-t_all // SLC_BLOCK), past_len + LANES)
    zero_prefix = jnp.zeros((bp, CONV_WIDTH - 1, CONV_DIM), F32)

    rows_p, rows_s, win_p, win_s, conv_p, conv_s = [], [], [], [], [], []
    for l in range(depth):
        lw = _layer_weights(l, norm1_g, w_in, q_norm_g, k_norm_g, cmp_pe, cmp_w1, cmp_w2, conv_w,
                            out_norm_g, w_out, norm2_g, peer_wq, peer_subkeys, peer_u, peer_v)
        q, rows, win, gates, gb, u = _in_proj(xp, lw, tabs_p, _row_tile(bp * t))
        kc, vc = _compress_prompt(rows, lw, bp, t)
        o_attn = _attn_prompt(q, rows, win, kc, vc, gates, consts_p, bp, t)
        p1, p2 = _conv_prev(u, zero_prefix, bp, t)
        x1, h2 = _out_proj(o_attn, gb, u, p1, p2, xp, lw, _row_tile(bp * t))
        routed = _route(h2, lw, LANES)
        xp = _peer(h2, routed, x1, lw, _peer_tile(bp * t), 1024)
        rows_p.append(rows.reshape(bp, t, 4, N_KV_HEADS, HEAD_DIM))
        win_p.append(win.reshape(bp, t, 2, N_KV_HEADS, HEAD_DIM)[:, t - WINDOW:])
        conv_p.append(u.reshape(bp, t, CONV_DIM)[:, t - (CONV_WIDTH - 1):])

        q, rows, win, gates, gb, u = _in_proj(xs, lw, tabs_s, _row_tile(bs * SAMPLE_T))
        o_attn = _attn_sample(
            page_table, q.reshape(bs, SAMPLE_T, Q_PAD), rows.reshape(bs, SAMPLE_T, 4 * LANES),
            win.reshape(bs, SAMPLE_T, 2 * LANES), state_win, gates.reshape(bs, SAMPLE_T, LANES),
            cache, lw, consts_s, l).reshape(bs * SAMPLE_T, Q_PAD)
        p1, p2 = _conv_prev(u, state_conv[l], bs, SAMPLE_T)
        x1, h2 = _out_proj(o_attn, gb, u, p1, p2, xs, lw, _row_tile(bs * SAMPLE_T))
        routed = _route(h2, lw, LANES)
        xs = _peer(h2, routed, x1, lw, _peer_tile(bs * SAMPLE_T), 1024)
        rows_s.append(rows.reshape(bs, SAMPLE_T, 4, N_KV_HEADS, HEAD_DIM)[:, :ts])
        new_win = win.reshape(bs, SAMPLE_T, 2, N_KV_HEADS, HEAD_DIM)[:, :ts]
        old_win = state_win_kv[l]
        win_s.append(jnp.concatenate([old_win, new_win], axis=1)[:, ts:])
        conv_s.append(u.reshape(bs, SAMPLE_T, CONV_DIM)[:, ts - (CONV_WIDTH - 1):ts])

    y_prompt = xp.reshape(bp, t, D_MODEL)
    y_sample = xs.reshape(bs, SAMPLE_T, D_MODEL)[:, :ts]
    return (y_prompt, y_sample,
            jnp.stack(rows_p, axis=2), jnp.stack(rows_s, axis=2),
            jnp.stack(win_p, axis=0), jnp.stack(win_s, axis=0),
            jnp.stack(conv_p, axis=0), jnp.stack(conv_s, axis=0))
```

```python
import functools
import math

import numpy as np
import jax
import jax.numpy as jnp
from jax import lax
from jax.experimental import pallas as pl
from jax.experimental.pallas import tpu as pltpu

F32 = jnp.float32
BF16 = jnp.bfloat16

LANES = 128
SUBLANES = 8
VMEM_LIMIT = 56 * 1024 * 1024

D_MODEL = 1024
HEAD_DIM = 64
N_Q_HEADS = 8
N_KV_HEADS = 2
HEADS_PER_GROUP = N_Q_HEADS // N_KV_HEADS
NSA_WIDTH = N_Q_HEADS * HEAD_DIM
KV_WIDTH = N_KV_HEADS * HEAD_DIM
CONV_DIM = D_MODEL - NSA_WIDTH
CONV_WIDTH = 3
N_BRANCH = 3
ROPE_DIM = HEAD_DIM // 4
ROPE_HALF = ROPE_DIM // 2
ROPE_THETA = 500000.0
CMP_LEN = 32
CMP_STRIDE = 16
SLC_BLOCK = 64
N_SELECT = 16
WINDOW = 512
Q_BLOCK = 128
PAGE_SIZE = 128
PEER_HEADS = 8
PEER_QDIM = 256
PEER_NKEYS = 128
PEER_TOPK = 16
RMS_EPS = 1e-6
NEG_INF = -1e30
FORCE_BONUS = 1e4

Q_PAD = N_Q_HEADS * LANES
IN_COLS = Q_PAD + 6 * KV_WIDTH + 3 * CONV_DIM + LANES
MIX_PAD = Q_PAD + CONV_DIM
SAMPLE_T = 8
SLC_KEY_STEP = 512

_NT = (((1,), (1,)), ((), ()))


def _cparams(sem):
    return pltpu.CompilerParams(dimension_semantics=sem, vmem_limit_bytes=VMEM_LIMIT)


def _dot(a, b):
    return jnp.dot(a, b, preferred_element_type=F32)


def _dot_nt(a, b):
    return lax.dot_general(a, b, _NT, preferred_element_type=F32)


def _group_norm(x, gmat, gain):
    ss = _dot((x * x).astype(BF16), gmat)
    return x * lax.rsqrt(ss + RMS_EPS) * gain


def _in_kernel(x_ref, g1_ref, w_ref, gm_ref, qg_ref, kg_ref, c_ref, s1_ref, s2_ref,
               q_ref, rows_ref, win_ref, gate_ref, gb_ref, u_ref):
    x = x_ref[...]
    ms = jnp.mean(x * x, axis=-1, keepdims=True)
    h = (x * lax.rsqrt(ms + RMS_EPS) * g1_ref[...]).astype(BF16)
    z = _dot(h, w_ref[...])
    gmat = gm_ref[...]
    cos = c_ref[...]
    sin_up = s1_ref[...]
    sin_dn = s2_ref[...]

    def norm_rope(zb, gain):
        y = _group_norm(zb, gmat, gain)
        return (y * cos + pltpu.roll(y, LANES - ROPE_HALF, 1) * sin_up
                + pltpu.roll(y, ROPE_HALF, 1) * sin_dn)

    for hb in range(N_Q_HEADS):
        q_ref[:, hb * LANES:(hb + 1) * LANES] = norm_rope(z[:, hb * LANES:(hb + 1) * LANES], qg_ref[...])
    o = Q_PAD
    kv = [z[:, o + r * LANES:o + (r + 1) * LANES] for r in range(6)]
    rows_ref[:, 0 * LANES:1 * LANES] = norm_rope(kv[0], kg_ref[0:1, :])
    rows_ref[:, 1 * LANES:2 * LANES] = kv[1]
    rows_ref[:, 2 * LANES:3 * LANES] = norm_rope(kv[2], kg_ref[1:2, :])
    rows_ref[:, 3 * LANES:4 * LANES] = kv[3]
    win_ref[:, 0:LANES] = norm_rope(kv[4], kg_ref[2:3, :])
    win_ref[:, LANES:2 * LANES] = kv[5]
    o += 6 * LANES
    gb_ref[...] = z[:, o:o + CONV_DIM]
    u_ref[...] = z[:, o + CONV_DIM:o + 2 * CONV_DIM] * z[:, o + 2 * CONV_DIM:o + 3 * CONV_DIM]
    o += 3 * CONV_DIM
    gate_ref[...] = jax.nn.sigmoid(z[:, o:o + LANES])


def _in_proj(x, lw, tabs, tm):
    n = x.shape[0]
    cos, sin_up, sin_dn = tabs
    npos_tiles = cos.shape[0] // tm
    row = lambda i: (i, 0)
    fixed = lambda i: (0, 0)
    pos = lambda i: (i % npos_tiles, 0)
    out_shape = (
        jax.ShapeDtypeStruct((n, Q_PAD), F32),
        jax.ShapeDtypeStruct((n, 4 * LANES), F32),
        jax.ShapeDtypeStruct((n, 2 * LANES), F32),
        jax.ShapeDtypeStruct((n, LANES), F32),
        jax.ShapeDtypeStruct((n, CONV_DIM), F32),
        jax.ShapeDtypeStruct((n, CONV_DIM), F32),
    )
    return pl.pallas_call(
        _in_kernel,
        out_shape=out_shape,
        grid=(n // tm,),
        in_specs=[
            pl.BlockSpec((tm, D_MODEL), row),
            pl.BlockSpec((1, D_MODEL), fixed),
            pl.BlockSpec((D_MODEL, IN_COLS), fixed),
            pl.BlockSpec((LANES, LANES), fixed),
            pl.BlockSpec((1, LANES), fixed),
            pl.BlockSpec((SUBLANES, LANES), fixed),
            pl.BlockSpec((tm, LANES), pos),
            pl.BlockSpec((tm, LANES), pos),
            pl.BlockSpec((tm, LANES), pos),
        ],
        out_specs=(
            pl.BlockSpec((tm, Q_PAD), row),
            pl.BlockSpec((tm, 4 * LANES), row),
            pl.BlockSpec((tm, 2 * LANES), row),
            pl.BlockSpec((tm, LANES), row),
            pl.BlockSpec((tm, CONV_DIM), row),
            pl.BlockSpec((tm, CONV_DIM), row),
        ),
        compiler_params=_cparams(("parallel",)),
        name="in_proj",
    )(x, lw["norm1"], lw["w_in"], lw["gmat"], lw["q_g"], lw["k_g"], cos, sin_up, sin_dn)


def _compress(src_ref, nch, w1_ref, pe_ref, w2_ref, role):
    acc0 = jnp.zeros((nch, LANES), F32)
    acc1 = jnp.zeros((nch, LANES), F32)
    for s2 in range(CMP_STRIDE // 2):
        xs = []
        for s in (2 * s2, 2 * s2 + 1):
            xs.append(src_ref[pl.ds(s, nch, stride=CMP_STRIDE), :])
        x0 = jnp.concatenate([xs[0] + pe_ref[role, 0, 2 * s2:2 * s2 + 1, :],
                              xs[1] + pe_ref[role, 0, 2 * s2 + 1:2 * s2 + 2, :]], axis=1)
        x1 = jnp.concatenate([xs[0] + pe_ref[role, 1, 2 * s2:2 * s2 + 1, :],
                              xs[1] + pe_ref[role, 1, 2 * s2 + 1:2 * s2 + 2, :]], axis=1)
        acc0 = acc0 + _dot(x0.astype(BF16), w1_ref[role, 0, s2])
        acc1 = acc1 + _dot(x1.astype(BF16), w1_ref[role, 1, s2])
    hid = acc0 + pltpu.roll(acc1, nch - 1, 0)
    return _dot(jax.nn.gelu(hid).astype(BF16), w2_ref[role])


def _cmp_kernel(k_ref, v_ref, w1_ref, pe_ref, w2_ref, kc_ref, vc_ref, *, nch):
    kc_ref[0] = _compress(k_ref, nch, w1_ref, pe_ref, w2_ref, 0)
    vc_ref[0] = _compress(v_ref, nch, w1_ref, pe_ref, w2_ref, 1)


def _compress_prompt(rows, lw, b, t):
    nch = t // CMP_STRIDE
    fixed = lambda i: (0,) * 5
    return pl.pallas_call(
        functools.partial(_cmp_kernel, nch=nch),
        out_shape=(jax.ShapeDtypeStruct((b, nch, LANES), F32),) * 2,
        grid=(b,),
        in_specs=[
            pl.BlockSpec((t, LANES), lambda i: (i, 0)),
            pl.BlockSpec((t, LANES), lambda i: (i, 1)),
            pl.BlockSpec(lw["cmp_w1"].shape, fixed),
            pl.BlockSpec(lw["cmp_pe"].shape, lambda i: (0,) * 4),
            pl.BlockSpec(lw["cmp_w2"].shape, lambda i: (0,) * 3),
        ],
        out_specs=(pl.BlockSpec((1, nch, LANES), lambda i: (i, 0, 0)),) * 2,
        compiler_params=_cparams(("parallel",)),
        name="compress_prompt",
    )(rows, rows, lw["cmp_w1"], lw["cmp_pe"], lw["cmp_w2"])


def _masked_softmax(s, mask):
    s = jnp.where(mask, s, NEG_INF)
    p = jnp.where(mask, jnp.exp(s - jnp.max(s, axis=-1, keepdims=True)), 0.0)
    return p / jnp.maximum(jnp.sum(p, axis=-1, keepdims=True), 1e-30)


def _split3(x):
    h1 = x.astype(BF16)
    r1 = x - h1.astype(F32)
    h2 = r1.astype(BF16)
    h3 = (r1 - h2.astype(F32)).astype(BF16)
    return h1, h2, h3


def _cmp_and_select(qb, qpos, nq, kc, vc, msel, n_slc):
    nrow = N_Q_HEADS * nq
    grow = HEADS_PER_GROUP * nq
    qpos_rows = jnp.concatenate([qpos] * N_Q_HEADS, axis=0)
    nch = kc.shape[0]
    s = _dot_nt(qb, kc.astype(BF16))
    cend = lax.broadcasted_iota(jnp.int32, (nrow, nch), 1) * CMP_STRIDE + (CMP_LEN - 1)
    p = _masked_softmax(s, cend <= qpos_rows)
    o_cmp = _dot(p.astype(BF16), vc.astype(BF16))

    jn = msel.shape[1]
    jid = lax.broadcasted_iota(jnp.int32, (nq, jn), 1)
    cur = qpos // SLC_BLOCK
    valid = jid * SLC_BLOCK <= qpos
    forced = (jid == 0) | (jid == cur) | (jid == cur - 1)
    sels = []
    for g in range(N_KV_HEADS):
        pg = p[g * grow:g * grow + nq]
        for hh in range(1, HEADS_PER_GROUP):
            pg = pg + p[g * grow + hh * nq:g * grow + (hh + 1) * nq]
        p_slc = sum(_dot(part, msel) for part in _split3(pg))
        score = jnp.where(valid, p_slc + jnp.where(forced, FORCE_BONUS, 0.0), NEG_INF)
        score = jnp.where(jid < n_slc, score, -3e38)
        rank = jnp.zeros((nq, jn), F32)
        for i in range(n_slc):
            col = score[:, i:i + 1]
            tie = jnp.where(jid > i, 1.0, 0.0)
            rank = rank + jnp.where(col > score, 1.0, jnp.where(col == score, tie, 0.0))
        sels.append(jnp.where(rank < float(min(N_SELECT, n_slc)), 1.0, 0.0).astype(BF16))
    return o_cmp, sels


def _softmax_pv(s, bias, values_fn):
    s = s + bias
    e = jnp.exp(s - jnp.max(s, axis=-1, keepdims=True))
    return values_fn(e.astype(BF16)) / jnp.sum(e, axis=-1, keepdims=True)


def _slc_branch(qb, qpos, nq, sels, eexp, klen, scores_fn, values_fn):
    grow = HEADS_PER_GROUP * nq
    causal = lax.broadcasted_iota(jnp.int32, (nq, klen), 1) <= qpos
    outs = []
    for g in range(N_KV_HEADS):
        bias_g = jnp.where(causal, jnp.where(_dot(sels[g], eexp) > 0.5, 0.0, NEG_INF), NEG_INF)
        bias = jnp.concatenate([bias_g] * HEADS_PER_GROUP, axis=0)
        s = scores_fn(qb[g * grow:(g + 1) * grow])
        outs.append(_softmax_pv(s, bias, values_fn))
    return jnp.concatenate(outs, axis=0)


def _win_branch(qb, qpos, nq, k_win, v_win, wpos0):
    wpos = wpos0 + lax.broadcasted_iota(jnp.int32, (nq, k_win.shape[0]), 1)
    bias_q = jnp.where(wpos <= qpos, jnp.where(wpos > qpos - WINDOW, 0.0, NEG_INF), NEG_INF)
    bias = jnp.concatenate([bias_q] * N_Q_HEADS, axis=0)
    vb = v_win.astype(BF16)
    return _softmax_pv(_dot_nt(qb, k_win.astype(BF16)), bias, lambda pr: _dot(pr, vb))


def _combine(gates, o_cmp, o_slc, o_win, nq):
    lane = lax.broadcasted_iota(jnp.int32, (nq, LANES), 1)
    outs = []
    for h in range(N_Q_HEADS):
        r0 = h * nq
        o = (gates[:, 3 * h:3 * h + 1] * o_cmp[r0:r0 + nq]
             + gates[:, 3 * h + 1:3 * h + 2] * o_slc[r0:r0 + nq]
             + gates[:, 3 * h + 2:3 * h + 3] * o_win[r0:r0 + nq])
        g = h // HEADS_PER_GROUP
        outs.append(jnp.where((lane >= g * HEAD_DIM) & (lane < (g + 1) * HEAD_DIM), o, 0.0))
    return outs


def _attn_prompt_kernel(q_ref, rows_ref, win_ref, kc_ref, vc_ref, gate_ref, msel_ref, eexp_ref,
                        o_ref, oslc_ref, *, t, n_slc, key_step):
    qblk = pl.program_id(1)
    start = qblk * Q_BLOCK
    scale = HEAD_DIM ** -0.5
    q = jnp.concatenate([q_ref[:, h * LANES:(h + 1) * LANES] for h in range(N_Q_HEADS)], axis=0) * scale
    qb = q.astype(BF16)
    qpos = start + lax.broadcasted_iota(jnp.int32, (Q_BLOCK, 1), 0)
    o_cmp, sels = _cmp_and_select(qb, qpos, Q_BLOCK, kc_ref[0], vc_ref[0], msel_ref[...], n_slc)

    def slc_for(klen):
        ksb = rows_ref[0:klen, 2 * LANES:3 * LANES].astype(BF16)
        vsb = rows_ref[0:klen, 3 * LANES:4 * LANES].astype(BF16)
        oslc_ref[...] = _slc_branch(qb, qpos, Q_BLOCK, sels, eexp_ref[:, 0:klen], klen,
                                    lambda qg: _dot_nt(qg, ksb), lambda pr: _dot(pr, vsb))

    per = key_step // Q_BLOCK
    for v in range(t // key_step):
        pl.when(qblk // per == v)(functools.partial(slc_for, (v + 1) * key_step))

    w0 = pl.multiple_of(jnp.maximum(start - WINDOW, 0), Q_BLOCK)
    wlen = WINDOW + Q_BLOCK
    o_win = _win_branch(qb, qpos, Q_BLOCK, win_ref[pl.ds(w0, wlen), 0:LANES],
                        win_ref[pl.ds(w0, wlen), LANES:2 * LANES], w0)
    outs = _combine(gate_ref[...], o_cmp, oslc_ref[...], o_win, Q_BLOCK)
    for h in range(N_Q_HEADS):
        o_ref[:, h * LANES:(h + 1) * LANES] = outs[h]


def _attn_prompt(q, rows, win, kc, vc, gates, consts, b, t):
    nb = t // Q_BLOCK
    nch = t // CMP_STRIDE
    msel, eexp, n_slc = consts
    tile = lambda i, j: (i * nb + j, 0)
    batch = lambda i, j: (i, 0)
    return pl.pallas_call(
        functools.partial(_attn_prompt_kernel, t=t, n_slc=n_slc, key_step=SLC_KEY_STEP),
        out_shape=jax.ShapeDtypeStruct((b * t, Q_PAD), F32),
        grid=(b, nb),
        in_specs=[
            pl.BlockSpec((Q_BLOCK, Q_PAD), tile),
            pl.BlockSpec((t, 4 * LANES), batch),
            pl.BlockSpec((t, 2 * LANES), batch),
            pl.BlockSpec((1, nch, LANES), lambda i, j: (i, 0, 0)),
            pl.BlockSpec((1, nch, LANES), lambda i, j: (i, 0, 0)),
            pl.BlockSpec((Q_BLOCK, LANES), tile),
            pl.BlockSpec(msel.shape, lambda i, j: (0, 0)),
            pl.BlockSpec(eexp.shape, lambda i, j: (0, 0)),
        ],
        out_specs=pl.BlockSpec((Q_BLOCK, Q_PAD), tile),
        scratch_shapes=[pltpu.VMEM((N_Q_HEADS * Q_BLOCK, LANES), F32)],
        compiler_params=_cparams(("parallel", "parallel")),
        name="attn_prompt",
    )(q, rows, win, kc, vc, gates, msel, eexp)


def _attn_sample_kernel(pt_ref, q_ref, new_ref, nwin_ref, state_ref, gate_ref, cache_ref,
                        w1_ref, pe_ref, w2_ref, msel_ref, eexp_ref, o_ref, raw_ref, past_ref, sem,
                        *, layer, n_pages, n_slc):
    b = pl.program_id(0)
    past_len = n_pages * PAGE_SIZE

    def page_copy(pg, role):
        return pltpu.make_async_copy(
            cache_ref.at[pt_ref[b, pg], layer * 4 + role],
            raw_ref.at[role, :, pl.ds(pg * PAGE_SIZE, PAGE_SIZE)],
            sem.at[pg])

    for pg in range(n_pages):
        for role in range(4):
            page_copy(pg, role).start()

    def land_page(pg, carry):
        for role in range(4):
            page_copy(pg, role).wait()
        off = pl.multiple_of(pg * PAGE_SIZE, PAGE_SIZE)
        for role in range(2):
            past_ref[role, pl.ds(off, PAGE_SIZE), :] = raw_ref[role, :, pl.ds(off, PAGE_SIZE)].T
        return carry

    lax.fori_loop(0, n_pages, land_page, 0)

    nch = past_len // CMP_STRIDE
    kc = _compress(past_ref.at[0], nch, w1_ref, pe_ref, w2_ref, 0)
    vc = _compress(past_ref.at[1], nch, w1_ref, pe_ref, w2_ref, 1)

    scale = HEAD_DIM ** -0.5
    qv = q_ref[0]
    q = jnp.concatenate([qv[:, h * LANES:(h + 1) * LANES] for h in range(N_Q_HEADS)], axis=0) * scale
    qb = q.astype(BF16)
    qpos = past_len + lax.broadcasted_iota(jnp.int32, (SAMPLE_T, 1), 0)
    o_cmp, sels = _cmp_and_select(qb, qpos, SAMPLE_T, kc, vc, msel_ref[...], n_slc)

    pad = jnp.zeros((LANES - SAMPLE_T, LANES), F32)
    new = new_ref[0]
    k_new = jnp.concatenate([new[:, 2 * LANES:3 * LANES], pad], axis=0).astype(BF16)
    v_new = jnp.concatenate([new[:, 3 * LANES:4 * LANES], pad], axis=0).astype(BF16)
    kt = raw_ref[2].astype(BF16)
    vt = raw_ref[3].astype(BF16)
    o_slc = _slc_branch(
        qb, qpos, SAMPLE_T, sels, eexp_ref[...], past_len + LANES,
        lambda qg: jnp.concatenate([_dot(qg, kt), _dot_nt(qg, k_new)], axis=1),
        lambda pr: _dot_nt(pr[:, 0:past_len], vt) + _dot(pr[:, past_len:], v_new))

    nw = nwin_ref[0]
    st = state_ref[0, 0]
    k_win = jnp.concatenate([st[:, 0:LANES], nw[:, 0:LANES], pad], axis=0)
    v_win = jnp.concatenate([st[:, LANES:2 * LANES], nw[:, LANES:2 * LANES], pad], axis=0)
    o_win = _win_branch(qb, qpos, SAMPLE_T, k_win, v_win, past_len - WINDOW)
    outs = _combine(gate_ref[0], o_cmp, o_slc, o_win, SAMPLE_T)
    for h in range(N_Q_HEADS):
        o_ref[0, :, h * LANES:(h + 1) * LANES] = outs[h]


def _attn_sample(page_table, q, new_rows, new_win, state_win, gates, cache, lw, consts, layer):
    bs, n_pages = page_table.shape
    msel, eexp, n_slc = consts
    past_len = n_pages * PAGE_SIZE
    per_b = lambda i, pt: (i, 0, 0)
    grid_spec = pltpu.PrefetchScalarGridSpec(
        num_scalar_prefetch=1,
        grid=(bs,),
        in_specs=[
            pl.BlockSpec((1, SAMPLE_T, Q_PAD), per_b),
            pl.BlockSpec((1, SAMPLE_T, 4 * LANES), per_b),
            pl.BlockSpec((1, SAMPLE_T, 2 * LANES), per_b),
            pl.BlockSpec((1, 1, WINDOW, 2 * LANES), lambda i, pt: (layer, i, 0, 0)),
            pl.BlockSpec((1, SAMPLE_T, LANES), per_b),
            pl.BlockSpec(memory_space=pl.ANY),
            pl.BlockSpec(lw["cmp_w1"].shape, lambda i, pt: (0,) * 5),
            pl.BlockSpec(lw["cmp_pe"].shape, lambda i, pt: (0,) * 4),
            pl.BlockSpec(lw["cmp_w2"].shape, lambda i, pt: (0,) * 3),
            pl.BlockSpec(msel.shape, lambda i, pt: (0, 0)),
            pl.BlockSpec(eexp.shape, lambda i, pt: (0, 0)),
        ],
        out_specs=pl.BlockSpec((1, SAMPLE_T, Q_PAD), per_b),
        scratch_shapes=[
            pltpu.VMEM((4, LANES, past_len), F32),
            pltpu.VMEM((2, past_len, LANES), F32),
            pltpu.SemaphoreType.DMA((n_pages,)),
        ],
    )
    return pl.pallas_call(
        functools.partial(_attn_sample_kernel, layer=layer, n_pages=n_pages, n_slc=n_slc),
        out_shape=jax.ShapeDtypeStruct((bs, SAMPLE_T, Q_PAD), F32),
        grid_spec=grid_spec,
        compiler_params=_cparams(("arbitrary",)),
        name="attn_sample",
    )(page_table, q, new_rows, new_win, state_win, gates, cache,
      lw["cmp_w1"], lw["cmp_pe"], lw["cmp_w2"], msel, eexp)


def _out_kernel(oa_ref, gb_ref, u_ref, p1_ref, p2_ref, cw_ref, x_ref, gm_ref, og_ref, w_ref, n2_ref,
                x1_ref, h2_ref):
    gmat = gm_ref[...]
    y_conv = cw_ref[0:1, :] * p2_ref[...] + cw_ref[1:2, :] * p1_ref[...] + cw_ref[2:3, :] * u_ref[...]
    o_conv = gb_ref[...] * y_conv
    parts = []
    for k in range(N_Q_HEADS):
        parts.append(_group_norm(oa_ref[:, k * LANES:(k + 1) * LANES], gmat,
                                 og_ref[:, k * LANES:(k + 1) * LANES]))
    for k in range(CONV_DIM // LANES):
        parts.append(_group_norm(o_conv[:, k * LANES:(k + 1) * LANES], gmat,
                                 og_ref[:, Q_PAD + k * LANES:Q_PAD + (k + 1) * LANES]))
    mix = jnp.concatenate(parts, axis=1).astype(BF16)
    x1 = x_ref[...] + _dot(mix, w_ref[...])
    x1_ref[...] = x1
    ms = jnp.mean(x1 * x1, axis=-1, keepdims=True)
    h2_ref[...] = (x1 * lax.rsqrt(ms + RMS_EPS) * n2_ref[...]).astype(BF16)


def _out_proj(o_attn, gb, u, prev1, prev2, x, lw, tm):
    n = x.shape[0]
    row = lambda i: (i, 0)
    fixed = lambda i: (0, 0)
    return pl.pallas_call(
        _out_kernel,
        out_shape=(jax.ShapeDtypeStruct((n, D_MODEL), F32), jax.ShapeDtypeStruct((n, D_MODEL), BF16)),
        grid=(n // tm,),
        in_specs=[
            pl.BlockSpec((tm, Q_PAD), row),
            pl.BlockSpec((tm, CONV_DIM), row),
            pl.BlockSpec((tm, CONV_DIM), row),
            pl.BlockSpec((tm, CONV_DIM), row),
            pl.BlockSpec((tm, CONV_DIM), row),
            pl.BlockSpec((SUBLANES, CONV_DIM), fixed),
            pl.BlockSpec((tm, D_MODEL), row),
            pl.BlockSpec((LANES, LANES), fixed),
            pl.BlockSpec((1, MIX_PAD), fixed),
            pl.BlockSpec((MIX_PAD, D_MODEL), fixed),
            pl.BlockSpec((1, D_MODEL), fixed),
        ],
        out_specs=(pl.BlockSpec((tm, D_MODEL), row), pl.BlockSpec((tm, D_MODEL), row)),
        compiler_params=_cparams(("parallel",)),
        name="out_proj",
    )(o_attn, gb, u, prev1, prev2, lw["conv_w"], x, lw["gmat"], lw["out_g"], lw["w_out"], lw["norm2"])


_CAND_ROWS = 80
PEER_SUB = 256
ROUTE_TOKENS = 256


def _route_kernel(h_ref, wq_ref, sk_ref, a1_ref, cnt_ref, b_ref, r2_ref,
                  qt_ref, s_ref, so_ref, rank_ref, v_ref, cand_ref):
    tn = h_ref.shape[0]
    k = PEER_TOPK
    qt_ref[...] = _dot_nt(wq_ref[...], h_ref[...])
    rowid = lax.broadcasted_iota(jnp.int32, (PEER_NKEYS, tn), 0).astype(F32)

    crow = lax.broadcasted_iota(jnp.int32, (_CAND_ROWS, tn), 0)
    cr = jnp.where(crow < 16, 0, jnp.where(crow < 72, 1 + (crow - 16) // 8, crow - 64))
    cj = jnp.where(crow < 16, crow, jnp.where(crow < 72, (crow - 16) % 8, 0))
    cvalid = (cr + 1) * (cj + 1) <= k
    cflat = (cr * k + cj).astype(F32)
    rid16 = lax.broadcasted_iota(jnp.int32, (k, tn), 0).astype(F32)
    neg = -jnp.inf

    def head_body(h, carry):
        for p in range(2):
            base = pl.multiple_of(h * PEER_QDIM + p * (PEER_QDIM // 2), PEER_QDIM // 2)
            s0 = _dot(sk_ref[2 * h + p], qt_ref[pl.ds(base, PEER_QDIM // 2), :].astype(BF16))
            so_ref[p] = s0
            s_ref[p] = s0
            rank_ref[p] = jnp.full((PEER_NKEYS, tn), float(k), F32)
        for r in range(k):
            for p in range(2):
                s = s_ref[p]
                m = jnp.max(s, axis=0, keepdims=True)
                idx = jnp.min(jnp.where(s == m, rowid, float(PEER_NKEYS)), axis=0, keepdims=True)
                hit = rowid == idx
                s_ref[p] = jnp.where(hit, neg, s)
                rank_ref[p] = jnp.where(hit, float(r), rank_ref[p])
                v_ref[p, r:r + 1, :] = m
        v1 = v_ref[0]
        v2 = v_ref[1]
        cand_ref[0:16, :] = v1[0:1] + v2
        for r in range(1, 8):
            cand_ref[8 + 8 * r:16 + 8 * r, :] = v1[r:r + 1] + v2[0:8]
        cand_ref[72:80, :] = v1[8:16] + v2[0:1]
        cand_ref[...] = jnp.where(cvalid, cand_ref[...], neg)
        cmax = v1[0:1] + v2[0:1]
        cnt = jnp.zeros((k, tn), F32)
        zsum = jnp.zeros((1, tn), F32)
        for _ in range(k):
            c = cand_ref[...]
            m = jnp.max(c, axis=0, keepdims=True)
            f = jnp.min(jnp.where(c == m, cflat, 1e9), axis=0, keepdims=True)
            cand_ref[...] = jnp.where(cflat == f, neg, c)
            cnt = cnt + jnp.where(rid16 == jnp.floor(f * (1.0 / k)), 1.0, 0.0)
            zsum = zsum + jnp.exp(m - cmax)
        r1 = rank_ref[0]
        r2 = rank_ref[1]
        a1_ref[h] = jnp.where(r1 < float(k), jnp.exp(so_ref[0] - v1[0:1]), 0.0)
        bval = jnp.where(r2 < float(k), jnp.exp(so_ref[1] - v2[0:1]), 0.0) / zsum
        b_ref[h] = pltpu.bitcast(bval.astype(BF16), jnp.uint32)
        r2_ref[h] = pltpu.bitcast(r2.astype(BF16), jnp.uint32)
        cdense = jnp.zeros((PEER_NKEYS, tn), F32)
        for r in range(k):
            cdense = jnp.where(r1 == float(r), cnt[r:r + 1], cdense)
        cnt_ref[h] = cdense
        return carry

    lax.fori_loop(0, PEER_HEADS, head_body, 0)


def _route(h2, lw, tn):
    n = h2.shape[0]
    shape = jax.ShapeDtypeStruct((PEER_HEADS, PEER_NKEYS, n), F32)
    shape_b = jax.ShapeDtypeStruct((PEER_HEADS, PEER_NKEYS // 2, n), jnp.uint32)
    spec = pl.BlockSpec((PEER_HEADS, PEER_NKEYS, tn), lambda i: (0, 0, i))
    spec_b = pl.BlockSpec((PEER_HEADS, PEER_NKEYS // 2, tn), lambda i: (0, 0, i))
    return pl.pallas_call(
        _route_kernel,
        out_shape=(shape, shape, shape_b, shape_b),
        grid=(n // tn,),
        in_specs=[
            pl.BlockSpec((tn, D_MODEL), lambda i: (i, 0)),
            pl.BlockSpec((PEER_HEADS * PEER_QDIM, D_MODEL), lambda i: (0, 0)),
            pl.BlockSpec((2 * PEER_HEADS, PEER_NKEYS, PEER_QDIM // 2), lambda i: (0, 0, 0)),
        ],
        out_specs=(spec, spec, spec_b, spec_b),
        scratch_shapes=[
            pltpu.VMEM((PEER_HEADS * PEER_QDIM, tn), F32),
            pltpu.VMEM((2, PEER_NKEYS, tn), F32),
            pltpu.VMEM((2, PEER_NKEYS, tn), F32),
            pltpu.VMEM((2, PEER_NKEYS, tn), F32),
            pltpu.VMEM((2, PEER_TOPK, tn), F32),
            pltpu.VMEM((_CAND_ROWS, tn), F32),
        ],
        compiler_params=_cparams(("parallel",)),
        name="peer_route",
    )(h2, lw["peer_wq_t"], lw["peer_sk"])


def _peer_kernel(h_ref, u_ref, vt_ref, a1_ref, cnt_ref, b_ref, r2_ref, x_ref, y_ref,
                 acc_ref, at_ref, *, eb, sub):
    e = pl.program_id(1)
    tn = h_ref.shape[0]
    pack = 2 * SUBLANES

    @pl.when(e == 0)
    def _():
        acc_ref[...] = jnp.zeros_like(acc_ref)

    h = h_ref[...]
    zero = jnp.zeros((pack, LANES), BF16)

    def sub_block(sb):
        r0 = sb * sub
        st = _dot_nt(u_ref[r0:r0 + sub, :], h)
        n_a = sub // PEER_NKEYS
        n_k = PEER_NKEYS // pack
        for c in range(tn // LANES):
            cols = slice(c * LANES, (c + 1) * LANES)
            w = [[None] * n_k for _ in range(n_a)]
            for hh in range(PEER_HEADS):
                cnt, a1 = [], []
                for a in range(n_a):
                    ai = sb * n_a + a
                    cnt.append(jnp.broadcast_to(cnt_ref[hh, ai:ai + 1, cols], (pack, LANES)).astype(BF16))
                    a1.append(jnp.broadcast_to(a1_ref[hh, ai:ai + 1, cols], (pack, LANES)).astype(BF16))
                for k in range(n_k):
                    rk = slice(k * SUBLANES, (k + 1) * SUBLANES)
                    r2 = pltpu.bitcast(r2_ref[hh, rk, cols], BF16)
                    bb = pltpu.bitcast(b_ref[hh, rk, cols], BF16)
                    for a in range(n_a):
                        t = a1[a] * jnp.where(r2 < cnt[a], bb, zero)
                        w[a][k] = t if w[a][k] is None else w[a][k] + t
            for a in range(n_a):
                g = jax.nn.gelu(st[a * PEER_NKEYS:(a + 1) * PEER_NKEYS, cols]).astype(BF16)
                rows = slice(r0 + a * PEER_NKEYS, r0 + (a + 1) * PEER_NKEYS)
                at_ref[rows, cols] = jnp.concatenate(w[a], axis=0) * g

    n_sb = eb // sub
    half = eb // 2
    tot = None
    for sb in range(n_sb):
        sub_block(sb)
        if (sb + 1) * sub % half == 0:
            k0 = (sb + 1) * sub - half
            d = _dot(vt_ref[:, k0:k0 + half], at_ref[k0:k0 + half, :])
            tot = d if tot is None else tot + d
    acc_ref[...] += tot

    @pl.when(e == pl.num_programs(1) - 1)
    def _():
        y_ref[...] = x_ref[...] + acc_ref[...].T


def _peer(h2, routed, x1, lw, tn, eb):
    n = h2.shape[0]
    n_exp = PEER_NKEYS * PEER_NKEYS
    a1, cnt, bb, r2 = routed
    tok = lambda i, e: (i, 0)
    key_rows = pl.BlockSpec((PEER_HEADS, eb // PEER_NKEYS, tn), lambda i, e: (0, e, i))
    key_full = pl.BlockSpec((PEER_HEADS, PEER_NKEYS // 2, tn), lambda i, e: (0, 0, i))
    return pl.pallas_call(
        functools.partial(_peer_kernel, eb=eb, sub=PEER_SUB),
        out_shape=jax.ShapeDtypeStruct((n, D_MODEL), F32),
        grid=(n // tn, n_exp // eb),
        in_specs=[
            pl.BlockSpec((tn, D_MODEL), tok),
            pl.BlockSpec((eb, D_MODEL), lambda i, e: (e, 0)),
            pl.BlockSpec((D_MODEL, eb), lambda i, e: (0, e)),
            key_rows, key_rows, key_full, key_full,
            pl.BlockSpec((tn, D_MODEL), tok),
        ],
        out_specs=pl.BlockSpec((tn, D_MODEL), tok),
        scratch_shapes=[pltpu.VMEM((D_MODEL, tn), F32), pltpu.VMEM((eb, tn), BF16)],
        compiler_params=_cparams(("parallel", "arbitrary")),
        name="peer_experts",
    )(h2, lw["peer_u"], lw["peer_vt"], a1, cnt, bb, r2, x1)


def _rope_tables(pos):
    inv_freq = ROPE_THETA ** (-jnp.arange(ROPE_HALF, dtype=F32) / ROPE_HALF)
    ang = pos.astype(F32)[:, None] * inv_freq[None, :]
    cos, sin = jnp.cos(ang), jnp.sin(ang)
    n = pos.shape[0]
    ones = jnp.ones((n, HEAD_DIM - ROPE_DIM), F32)
    zeros = jnp.zeros((n, HEAD_DIM - ROPE_DIM), F32)
    zh = jnp.zeros((n, ROPE_HALF), F32)
    c = jnp.concatenate([cos, cos, ones], axis=1)
    s_up = jnp.concatenate([-sin, zh, zeros], axis=1)
    s_dn = jnp.concatenate([zh, sin, zeros], axis=1)
    return tuple(jnp.tile(a, (1, 2)) for a in (c, s_up, s_dn))


def _attn_consts(n_cmp, nch, n_slc, slc_len):
    jn = -(-n_slc // LANES) * LANES
    sr = SLC_BLOCK // CMP_STRIDE
    r = CMP_LEN // CMP_STRIDE
    msel = np.zeros((nch, jn), np.float32)
    for j in range(n_slc):
        for m in range(sr):
            for n in range(r):
                i = sr * j + m - n
                if 0 <= i < n_cmp:
                    msel[i, j] += 1.0
    eexp = np.zeros((jn, slc_len), np.float32)
    keys = np.arange(slc_len)
    eexp[keys // SLC_BLOCK, keys] = 1.0
    return jnp.asarray(msel, BF16), jnp.asarray(eexp, BF16), n_slc


def _layer_weights(l, norm1_g, w_in, q_norm_g, k_norm_g, cmp_pe, cmp_w1, cmp_w2, conv_w, out_norm_g,
                   w_out, norm2_g, peer_wq, peer_subkeys, peer_u, peer_v):
    c1 = NSA_WIDTH
    c2 = c1 + 6 * KV_WIDTH
    c3 = c2 + N_Q_HEADS * N_BRANCH
    w = w_in[l]
    half = jnp.zeros((D_MODEL, HEAD_DIM), F32)
    q_cols = []
    for h in range(N_Q_HEADS):
        wh = w[:, h * HEAD_DIM:(h + 1) * HEAD_DIM]
        q_cols += [wh, half] if h < HEADS_PER_GROUP else [half, wh]
    gl_pad = jnp.zeros((D_MODEL, LANES - N_Q_HEADS * N_BRANCH), F32)
    w_all = jnp.concatenate(q_cols + [w[:, c1:c2], w[:, c3:], w[:, c2:c3], gl_pad], axis=1)

    blk = np.kron(np.eye(2, dtype=np.float32), np.full((HEAD_DIM, HEAD_DIM), 1.0 / HEAD_DIM, np.float32))
    kg = jnp.tile(k_norm_g[l], (1, 2))
    kg = jnp.concatenate([kg, jnp.zeros((SUBLANES - N_BRANCH, LANES), F32)], axis=0)

    r = CMP_LEN // CMP_STRIDE
    w1 = cmp_w1[l].reshape(2, r, CMP_STRIDE, HEAD_DIM, HEAD_DIM)
    eye2 = jnp.eye(2, dtype=F32)
    w1 = jnp.einsum('xmsde,gk->xmsgdke', w1, eye2).reshape(2, r, CMP_STRIDE // 2, 2 * LANES, LANES)
    pe = jnp.tile(cmp_pe[l].reshape(2, r, CMP_STRIDE, HEAD_DIM), (1, 1, 1, 2))
    w2 = jnp.einsum('xde,gk->xgdke', cmp_w2[l], eye2).reshape(2, LANES, LANES)

    og = out_norm_g[l]
    zero_h = jnp.zeros((HEAD_DIM,), F32)
    og_parts, wo_rows = [], []
    zero_rows = jnp.zeros((HEAD_DIM, D_MODEL), F32)
    for h in range(N_Q_HEADS):
        gh = og[h * HEAD_DIM:(h + 1) * HEAD_DIM]
        wr = w_out[l][h * HEAD_DIM:(h + 1) * HEAD_DIM]
        og_parts += [gh, zero_h] if h < HEADS_PER_GROUP else [zero_h, gh]
        wo_rows += [wr, zero_rows] if h < HEADS_PER_GROUP else [zero_rows, wr]
    og_pad = jnp.concatenate(og_parts + [og[NSA_WIDTH:]])[None, :]
    wo_pad = jnp.concatenate(wo_rows + [w_out[l][NSA_WIDTH:]], axis=0)

    cw = jnp.concatenate([conv_w[l], jnp.zeros((SUBLANES - CONV_WIDTH, CONV_DIM), F32)], axis=0)
    return {
        "norm1": norm1_g[l][None, :],
        "w_in": w_all.astype(BF16),
        "gmat": jnp.asarray(blk, BF16),
        "q_g": jnp.tile(q_norm_g[l], 2)[None, :],
        "k_g": kg,
        "cmp_w1": w1.astype(BF16),
        "cmp_pe": pe,
        "cmp_w2": w2.astype(BF16),
        "conv_w": cw,
        "out_g": og_pad,
        "w_out": wo_pad.astype(BF16),
        "norm2": norm2_g[l][None, :],
        "peer_wq_t": peer_wq[l].T.astype(BF16),
        "peer_sk": peer_subkeys[l].reshape(2 * PEER_HEADS, PEER_NKEYS, PEER_QDIM // 2).astype(BF16),
        "peer_u": peer_u[l].astype(BF16),
        "peer_vt": peer_v[l].T.astype(BF16),
    }


def _conv_prev(u, prefix, b, t):
    ext = jnp.concatenate([prefix, u.reshape(b, t, CONV_DIM)], axis=1)
    return ext[:, 1:t + 1].reshape(b * t, CONV_DIM), ext[:, 0:t].reshape(b * t, CONV_DIM)


def _row_tile(n):
    for tm in (256, 128):
        if n % tm == 0:
            return tm
    raise ValueError(f"token count {n} must be a multiple of 128")


def _peer_tile(n):
    for tn in (512, 256, 128):
        if n % tn == 0:
            return tn
    raise ValueError(f"token count {n} must be a multiple of 128")


def kernel(x_prompt, x_sample, cache_nsa_kv, state_win_kv, state_conv, page_table,
           norm1_g, w_in, q_norm_g, k_norm_g, cmp_pe, cmp_w1, cmp_w2, conv_w, out_norm_g, w_out,
           norm2_g, peer_wq, peer_subkeys, peer_u, peer_v):
    bp, t, _ = x_prompt.shape
    bs, ts, _ = x_sample.shape
    depth = norm1_g.shape[0]
    n_pages = page_table.shape[1]
    past_len = n_pages * PAGE_SIZE
    assert t % SLC_KEY_STEP == 0 and t >= WINDOW + Q_BLOCK and ts <= SAMPLE_T
    assert state_win_kv.shape[2] == WINDOW and past_len % SLC_BLOCK == 0

    xp = x_prompt.reshape(bp * t, D_MODEL)
    xs = jnp.pad(x_sample, ((0, 0), (0, SAMPLE_T - ts), (0, 0))).reshape(bs * SAMPLE_T, D_MODEL)
    cache = jnp.transpose(cache_nsa_kv, (0, 2, 3, 4, 5, 1)).reshape(
        cache_nsa_kv.shape[0], depth * 4, LANES, PAGE_SIZE)
    state_win = state_win_kv.reshape(depth, bs, WINDOW, 2 * LANES)

    tabs_p = _rope_tables(jnp.arange(t))
    tabs_s = _rope_tables(jnp.tile(past_len + jnp.arange(SAMPLE_T), bs))
    n_cmp_p = (t - CMP_LEN) // CMP_STRIDE + 1
    consts_p = _attn_consts(n_cmp_p, t // CMP_STRIDE, t // SLC_BLOCK, t)
    t_all = past_len + ts
    n_cmp_s = (t_all - CMP_LEN) // CMP_STRIDE + 1
    assert n_cmp_s <= past_len // CMP_STRIDE
    consts_s = _attn_consts(n_cmp_s, past_len // CMP_STRIDE, -(-t_all // SLC_BLOCK), past_len + LANES)
    zero_prefix = jnp.zeros((bp, CONV_WIDTH - 1, CONV_DIM), F32)

    rows_p, rows_s, win_p, win_s, conv_p, conv_s = [], [], [], [], [], []
    for l in range(depth):
        lw = _layer_weights(l, norm1_g, w_in, q_norm_g, k_norm_g, cmp_pe, cmp_w1, cmp_w2, conv_w,
                            out_norm_g, w_out, norm2_g, peer_wq, peer_subkeys, peer_u, peer_v)
        q, rows, win, gates, gb, u = _in_proj(xp, lw, tabs_p, _row_tile(bp * t))
        kc, vc = _compress_prompt(rows, lw, bp, t)
        o_attn = _attn_prompt(q, rows, win, kc, vc, gates, consts_p, bp, t)
        p1, p2 = _conv_prev(u, zero_prefix, bp, t)
        x1, h2 = _out_proj(o_attn, gb, u, p1, p2, xp, lw, _row_tile(bp * t))
        routed = _route(h2, lw, ROUTE_TOKENS if h2.shape[0] % ROUTE_TOKENS == 0 else LANES)
        xp = _peer(h2, routed, x1, lw, _peer_tile(bp * t), 1024)
        rows_p.append(rows.reshape(bp, t, 4, N_KV_HEADS, HEAD_DIM))
        win_p.append(win.reshape(bp, t, 2, N_KV_HEADS, HEAD_DIM)[:, t - WINDOW:])
        conv_p.append(u.reshape(bp, t, CONV_DIM)[:, t - (CONV_WIDTH - 1):])

        q, rows, win, gates, gb, u = _in_proj(xs, lw, tabs_s, _row_tile(bs * SAMPLE_T))
        o_attn = _attn_sample(
            page_table, q.reshape(bs, SAMPLE_T, Q_PAD), rows.reshape(bs, SAMPLE_T, 4 * LANES),
            win.reshape(bs, SAMPLE_T, 2 * LANES), state_win, gates.reshape(bs, SAMPLE_T, LANES),
            cache, lw, consts_s, l).reshape(bs * SAMPLE_T, Q_PAD)
        p1, p2 = _conv_prev(u, state_conv[l], bs, SAMPLE_T)
        x1, h2 = _out_proj(o_attn, gb, u, p1, p2, xs, lw, _row_tile(bs * SAMPLE_T))
        routed = _route(h2, lw, ROUTE_TOKENS if h2.shape[0] % ROUTE_TOKENS == 0 else LANES)
        xs = _peer(h2, routed, x1, lw, _peer_tile(bs * SAMPLE_T), 1024)
        rows_s.append(rows.reshape(bs, SAMPLE_T, 4, N_KV_HEADS, HEAD_DIM)[:, :ts])
        new_win = win.reshape(bs, SAMPLE_T, 2, N_KV_HEADS, HEAD_DIM)[:, :ts]
        old_win = state_win_kv[l]
        win_s.append(jnp.concatenate([old_win, new_win], axis=1)[:, ts:])
        conv_s.append(u.reshape(bs, SAMPLE_T, CONV_DIM)[:, ts - (CONV_WIDTH - 1):ts])

    y_prompt = xp.reshape(bp, t, D_MODEL)
    y_sample = xs.reshape(bs, SAMPLE_T, D_MODEL)[:, :ts]
    return (y_prompt, y_sample,
            jnp.stack(rows_p, axis=2), jnp.stack(rows_s, axis=2),
            jnp.stack(win_p, axis=0), jnp.stack(win_s, axis=0),
            jnp.stack(conv_p, axis=0), jnp.stack(conv_s, axis=0))
```

```python
import functools
import math

import numpy as np
import jax
import jax.numpy as jnp
from jax import lax
from jax.experimental import pallas as pl
from jax.experimental.pallas import tpu as pltpu

F32 = jnp.float32
BF16 = jnp.bfloat16

LANES = 128
SUBLANES = 8
VMEM_LIMIT = 56 * 1024 * 1024

D_MODEL = 1024
HEAD_DIM = 64
N_Q_HEADS = 8
N_KV_HEADS = 2
HEADS_PER_GROUP = N_Q_HEADS // N_KV_HEADS
NSA_WIDTH = N_Q_HEADS * HEAD_DIM
KV_WIDTH = N_KV_HEADS * HEAD_DIM
CONV_DIM = D_MODEL - NSA_WIDTH
CONV_WIDTH = 3
N_BRANCH = 3
ROPE_DIM = HEAD_DIM // 4
ROPE_HALF = ROPE_DIM // 2
ROPE_THETA = 500000.0
CMP_LEN = 32
CMP_STRIDE = 16
SLC_BLOCK = 64
N_SELECT = 16
WINDOW = 512
Q_BLOCK = 128
PAGE_SIZE = 128
PEER_HEADS = 8
PEER_QDIM = 256
PEER_NKEYS = 128
PEER_TOPK = 16
RMS_EPS = 1e-6
NEG_INF = -1e30
FORCE_BONUS = 1e4

Q_PAD = N_Q_HEADS * LANES
IN_COLS = Q_PAD + 6 * KV_WIDTH + 3 * CONV_DIM + LANES
MIX_PAD = Q_PAD + CONV_DIM
SAMPLE_T = 8
SLC_KEY_STEP = 256

_NT = (((1,), (1,)), ((), ()))


def _cparams(sem):
    return pltpu.CompilerParams(dimension_semantics=sem, vmem_limit_bytes=VMEM_LIMIT)


def _dot(a, b):
    return jnp.dot(a, b, preferred_element_type=F32)


def _dot_nt(a, b):
    return lax.dot_general(a, b, _NT, preferred_element_type=F32)


def _group_norm(x, gmat, gain):
    ss = _dot((x * x).astype(BF16), gmat)
    return x * lax.rsqrt(ss + RMS_EPS) * gain


def _in_kernel(x_ref, g1_ref, w_ref, gm_ref, qg_ref, kg_ref, c_ref, s1_ref, s2_ref,
               q_ref, rows_ref, win_ref, gate_ref, gb_ref, u_ref):
    x = x_ref[...]
    ms = jnp.mean(x * x, axis=-1, keepdims=True)
    h = (x * lax.rsqrt(ms + RMS_EPS) * g1_ref[...]).astype(BF16)
    z = _dot(h, w_ref[...])
    gmat = gm_ref[...]
    cos = c_ref[...]
    sin_up = s1_ref[...]
    sin_dn = s2_ref[...]

    def norm_rope(zb, gain):
        y = _group_norm(zb, gmat, gain)
        return (y * cos + pltpu.roll(y, LANES - ROPE_HALF, 1) * sin_up
                + pltpu.roll(y, ROPE_HALF, 1) * sin_dn)

    for hb in range(N_Q_HEADS):
        q_ref[:, hb * LANES:(hb + 1) * LANES] = norm_rope(z[:, hb * LANES:(hb + 1) * LANES], qg_ref[...])
    o = Q_PAD
    kv = [z[:, o + r * LANES:o + (r + 1) * LANES] for r in range(6)]
    rows_ref[:, 0 * LANES:1 * LANES] = norm_rope(kv[0], kg_ref[0:1, :])
    rows_ref[:, 1 * LANES:2 * LANES] = kv[1]
    rows_ref[:, 2 * LANES:3 * LANES] = norm_rope(kv[2], kg_ref[1:2, :])
    rows_ref[:, 3 * LANES:4 * LANES] = kv[3]
    win_ref[:, 0:LANES] = norm_rope(kv[4], kg_ref[2:3, :])
    win_ref[:, LANES:2 * LANES] = kv[5]
    o += 6 * LANES
    gb_ref[...] = z[:, o:o + CONV_DIM]
    u_ref[...] = z[:, o + CONV_DIM:o + 2 * CONV_DIM] * z[:, o + 2 * CONV_DIM:o + 3 * CONV_DIM]
    o += 3 * CONV_DIM
    gate_ref[...] = jax.nn.sigmoid(z[:, o:o + LANES])


def _in_proj(x, lw, tabs, tm):
    n = x.shape[0]
    cos, sin_up, sin_dn = tabs
    npos_tiles = cos.shape[0] // tm
    row = lambda i: (i, 0)
    fixed = lambda i: (0, 0)
    pos = lambda i: (i % npos_tiles, 0)
    out_shape = (
        jax.ShapeDtypeStruct((n, Q_PAD), F32),
        jax.ShapeDtypeStruct((n, 4 * LANES), F32),
        jax.ShapeDtypeStruct((n, 2 * LANES), F32),
        jax.ShapeDtypeStruct((n, LANES), F32),
        jax.ShapeDtypeStruct((n, CONV_DIM), F32),
        jax.ShapeDtypeStruct((n, CONV_DIM), F32),
    )
    return pl.pallas_call(
        _in_kernel,
        out_shape=out_shape,
        grid=(n // tm,),
        in_specs=[
            pl.BlockSpec((tm, D_MODEL), row),
            pl.BlockSpec((1, D_MODEL), fixed),
            pl.BlockSpec((D_MODEL, IN_COLS), fixed),
            pl.BlockSpec((LANES, LANES), fixed),
            pl.BlockSpec((1, LANES), fixed),
            pl.BlockSpec((SUBLANES, LANES), fixed),
            pl.BlockSpec((tm, LANES), pos),
            pl.BlockSpec((tm, LANES), pos),
            pl.BlockSpec((tm, LANES), pos),
        ],
        out_specs=(
            pl.BlockSpec((tm, Q_PAD), row),
            pl.BlockSpec((tm, 4 * LANES), row),
            pl.BlockSpec((tm, 2 * LANES), row),
            pl.BlockSpec((tm, LANES), row),
            pl.BlockSpec((tm, CONV_DIM), row),
            pl.BlockSpec((tm, CONV_DIM), row),
        ),
        compiler_params=_cparams(("parallel",)),
        name="in_proj",
    )(x, lw["norm1"], lw["w_in"], lw["gmat"], lw["q_g"], lw["k_g"], cos, sin_up, sin_dn)


def _compress(src_ref, nch, w1_ref, pe_ref, w2_ref, role):
    acc0 = jnp.zeros((nch, LANES), F32)
    acc1 = jnp.zeros((nch, LANES), F32)
    for s2 in range(CMP_STRIDE // 2):
        xs = []
        for s in (2 * s2, 2 * s2 + 1):
            xs.append(src_ref[pl.ds(s, nch, stride=CMP_STRIDE), :])
        x0 = jnp.concatenate([xs[0] + pe_ref[role, 0, 2 * s2:2 * s2 + 1, :],
                              xs[1] + pe_ref[role, 0, 2 * s2 + 1:2 * s2 + 2, :]], axis=1)
        x1 = jnp.concatenate([xs[0] + pe_ref[role, 1, 2 * s2:2 * s2 + 1, :],
                              xs[1] + pe_ref[role, 1, 2 * s2 + 1:2 * s2 + 2, :]], axis=1)
        acc0 = acc0 + _dot(x0.astype(BF16), w1_ref[role, 0, s2])
        acc1 = acc1 + _dot(x1.astype(BF16), w1_ref[role, 1, s2])
    hid = acc0 + pltpu.roll(acc1, nch - 1, 0)
    return _dot(jax.nn.gelu(hid).astype(BF16), w2_ref[role])


def _cmp_kernel(k_ref, v_ref, w1_ref, pe_ref, w2_ref, kc_ref, vc_ref, *, nch):
    kc_ref[0] = _compress(k_ref, nch, w1_ref, pe_ref, w2_ref, 0)
    vc_ref[0] = _compress(v_ref, nch, w1_ref, pe_ref, w2_ref, 1)


def _compress_prompt(rows, lw, b, t):
    nch = t // CMP_STRIDE
    fixed = lambda i: (0,) * 5
    return pl.pallas_call(
        functools.partial(_cmp_kernel, nch=nch),
        out_shape=(jax.ShapeDtypeStruct((b, nch, LANES), F32),) * 2,
        grid=(b,),
        in_specs=[
            pl.BlockSpec((t, LANES), lambda i: (i, 0)),
            pl.BlockSpec((t, LANES), lambda i: (i, 1)),
            pl.BlockSpec(lw["cmp_w1"].shape, fixed),
            pl.BlockSpec(lw["cmp_pe"].shape, lambda i: (0,) * 4),
            pl.BlockSpec(lw["cmp_w2"].shape, lambda i: (0,) * 3),
        ],
        out_specs=(pl.BlockSpec((1, nch, LANES), lambda i: (i, 0, 0)),) * 2,
        compiler_params=_cparams(("parallel",)),
        name="compress_prompt",
    )(rows, rows, lw["cmp_w1"], lw["cmp_pe"], lw["cmp_w2"])


def _masked_softmax(s, mask):
    s = jnp.where(mask, s, NEG_INF)
    p = jnp.where(mask, jnp.exp(s - jnp.max(s, axis=-1, keepdims=True)), 0.0)
    return p / jnp.maximum(jnp.sum(p, axis=-1, keepdims=True), 1e-30)


def _split3(x):
    h1 = x.astype(BF16)
    r1 = x - h1.astype(F32)
    h2 = r1.astype(BF16)
    h3 = (r1 - h2.astype(F32)).astype(BF16)
    return h1, h2, h3


def _cmp_and_select(qb, qpos, nq, kc, vc, msel, n_slc):
    nrow = N_Q_HEADS * nq
    grow = HEADS_PER_GROUP * nq
    qpos_rows = jnp.concatenate([qpos] * N_Q_HEADS, axis=0)
    nch = kc.shape[0]
    s = _dot_nt(qb, kc.astype(BF16))
    cend = lax.broadcasted_iota(jnp.int32, (nrow, nch), 1) * CMP_STRIDE + (CMP_LEN - 1)
    p = _masked_softmax(s, cend <= qpos_rows)
    o_cmp = _dot(p.astype(BF16), vc.astype(BF16))

    jn = msel.shape[1]
    jid = lax.broadcasted_iota(jnp.int32, (nq, jn), 1)
    cur = qpos // SLC_BLOCK
    valid = jid * SLC_BLOCK <= qpos
    forced = (jid == 0) | (jid == cur) | (jid == cur - 1)
    sels = []
    for g in range(N_KV_HEADS):
        pg = p[g * grow:g * grow + nq]
        for hh in range(1, HEADS_PER_GROUP):
            pg = pg + p[g * grow + hh * nq:g * grow + (hh + 1) * nq]
        p_slc = sum(_dot(part, msel) for part in _split3(pg))
        score = jnp.where(valid, p_slc + jnp.where(forced, FORCE_BONUS, 0.0), NEG_INF)
        score = jnp.where(jid < n_slc, score, -3e38)
        rank = jnp.zeros((nq, jn), F32)
        for i in range(n_slc):
            col = score[:, i:i + 1]
            tie = jnp.where(jid > i, 1.0, 0.0)
            rank = rank + jnp.where(col > score, 1.0, jnp.where(col == score, tie, 0.0))
        sels.append(jnp.where(rank < float(min(N_SELECT, n_slc)), 1.0, 0.0).astype(BF16))
    return o_cmp, sels


def _softmax_pv(s, bias, values_fn):
    s = s + bias
    e = jnp.exp(s - jnp.max(s, axis=-1, keepdims=True))
    return values_fn(e.astype(BF16)) / jnp.sum(e, axis=-1, keepdims=True)


def _slc_branch(qb, qpos, nq, sels, eexp, klen, scores_fn, values_fn):
    grow = HEADS_PER_GROUP * nq
    causal = lax.broadcasted_iota(jnp.int32, (nq, klen), 1) <= qpos
    outs = []
    for g in range(N_KV_HEADS):
        bias_g = jnp.where(causal, jnp.where(_dot(sels[g], eexp) > 0.5, 0.0, NEG_INF), NEG_INF)
        bias = jnp.concatenate([bias_g] * HEADS_PER_GROUP, axis=0)
        s = scores_fn(qb[g * grow:(g + 1) * grow])
        outs.append(_softmax_pv(s, bias, values_fn))
    return jnp.concatenate(outs, axis=0)


def _win_branch(qb, qpos, nq, k_win, v_win, wpos0):
    wpos = wpos0 + lax.broadcasted_iota(jnp.int32, (nq, k_win.shape[0]), 1)
    bias_q = jnp.where(wpos <= qpos, jnp.where(wpos > qpos - WINDOW, 0.0, NEG_INF), NEG_INF)
    bias = jnp.concatenate([bias_q] * N_Q_HEADS, axis=0)
    vb = v_win.astype(BF16)
    return _softmax_pv(_dot_nt(qb, k_win.astype(BF16)), bias, lambda pr: _dot(pr, vb))


def _combine(gates, o_cmp, o_slc, o_win, nq):
    lane = lax.broadcasted_iota(jnp.int32, (nq, LANES), 1)
    outs = []
    for h in range(N_Q_HEADS):
        r0 = h * nq
        o = (gates[:, 3 * h:3 * h + 1] * o_cmp[r0:r0 + nq]
             + gates[:, 3 * h + 1:3 * h + 2] * o_slc[r0:r0 + nq]
             + gates[:, 3 * h + 2:3 * h + 3] * o_win[r0:r0 + nq])
        g = h // HEADS_PER_GROUP
        outs.append(jnp.where((lane >= g * HEAD_DIM) & (lane < (g + 1) * HEAD_DIM), o, 0.0))
    return outs


def _attn_prompt_kernel(q_ref, rows_ref, win_ref, kc_ref, vc_ref, gate_ref, msel_ref, eexp_ref,
                        o_ref, oslc_ref, *, t, n_slc, key_step):
    qblk = pl.program_id(1)
    start = qblk * Q_BLOCK
    scale = HEAD_DIM ** -0.5
    q = jnp.concatenate([q_ref[:, h * LANES:(h + 1) * LANES] for h in range(N_Q_HEADS)], axis=0) * scale
    qb = q.astype(BF16)
    qpos = start + lax.broadcasted_iota(jnp.int32, (Q_BLOCK, 1), 0)
    o_cmp, sels = _cmp_and_select(qb, qpos, Q_BLOCK, kc_ref[0], vc_ref[0], msel_ref[...], n_slc)

    def slc_for(klen):
        ksb = rows_ref[0:klen, 2 * LANES:3 * LANES].astype(BF16)
        vsb = rows_ref[0:klen, 3 * LANES:4 * LANES].astype(BF16)
        oslc_ref[...] = _slc_branch(qb, qpos, Q_BLOCK, sels, eexp_ref[:, 0:klen], klen,
                                    lambda qg: _dot_nt(qg, ksb), lambda pr: _dot(pr, vsb))

    per = key_step // Q_BLOCK
    for v in range(t // key_step):
        pl.when(qblk // per == v)(functools.partial(slc_for, (v + 1) * key_step))

    w0 = pl.multiple_of(jnp.maximum(start - WINDOW, 0), Q_BLOCK)
    wlen = WINDOW + Q_BLOCK
    o_win = _win_branch(qb, qpos, Q_BLOCK, win_ref[pl.ds(w0, wlen), 0:LANES],
                        win_ref[pl.ds(w0, wlen), LANES:2 * LANES], w0)
    outs = _combine(gate_ref[...], o_cmp, oslc_ref[...], o_win, Q_BLOCK)
    for h in range(N_Q_HEADS):
        o_ref[:, h * LANES:(h + 1) * LANES] = outs[h]


def _attn_prompt(q, rows, win, kc, vc, gates, consts, b, t):
    nb = t // Q_BLOCK
    nch = t // CMP_STRIDE
    msel, eexp, n_slc = consts
    tile = lambda i, j: (i * nb + j, 0)
    batch = lambda i, j: (i, 0)
    return pl.pallas_call(
        functools.partial(_attn_prompt_kernel, t=t, n_slc=n_slc, key_step=SLC_KEY_STEP),
        out_shape=jax.ShapeDtypeStruct((b * t, Q_PAD), F32),
        grid=(b, nb),
        in_specs=[
            pl.BlockSpec((Q_BLOCK, Q_PAD), tile),
            pl.BlockSpec((t, 4 * LANES), batch),
            pl.BlockSpec((t, 2 * LANES), batch),
            pl.BlockSpec((1, nch, LANES), lambda i, j: (i, 0, 0)),
            pl.BlockSpec((1, nch, LANES), lambda i, j: (i, 0, 0)),
            pl.BlockSpec((Q_BLOCK, LANES), tile),
            pl.BlockSpec(msel.shape, lambda i, j: (0, 0)),
            pl.BlockSpec(eexp.shape, lambda i, j: (0, 0)),
        ],
        out_specs=pl.BlockSpec((Q_BLOCK, Q_PAD), tile),
        scratch_shapes=[pltpu.VMEM((N_Q_HEADS * Q_BLOCK, LANES), F32)],
        compiler_params=_cparams(("parallel", "parallel")),
        name="attn_prompt",
    )(q, rows, win, kc, vc, gates, msel, eexp)


def _attn_sample_kernel(pt_ref, q_ref, new_ref, nwin_ref, state_ref, gate_ref, cache_ref,
                        w1_ref, pe_ref, w2_ref, msel_ref, eexp_ref, o_ref, raw_ref, past_ref, kvt_ref, sem,
                        *, layer, n_pages, n_slc):
    b = pl.program_id(0)
    nb = pl.num_programs(0)
    past_len = n_pages * PAGE_SIZE
    cmp_roles, slc_roles = (0, 1), (2, 3)

    def page_copy(seq, pg, role):
        return pltpu.make_async_copy(
            cache_ref.at[pt_ref[seq, pg], layer * 4 + role],
            raw_ref.at[role, :, pl.ds(pg * PAGE_SIZE, PAGE_SIZE)],
            sem.at[role // 2, pg])

    def start_pages(seq, roles):
        for pg in range(n_pages):
            for role in roles:
                page_copy(seq, pg, role).start()

    @pl.when(b == 0)
    def _():
        start_pages(b, cmp_roles)
        start_pages(b, slc_roles)

    def land_page(pg, carry):
        for role in cmp_roles:
            page_copy(b, pg, role).wait()
        off = pl.multiple_of(pg * PAGE_SIZE, PAGE_SIZE)
        for role in cmp_roles:
            past_ref[role, pl.ds(off, PAGE_SIZE), :] = raw_ref[role, :, pl.ds(off, PAGE_SIZE)].T
        return carry

    lax.fori_loop(0, n_pages, land_page, 0)

    @pl.when(b + 1 < nb)
    def _():
        start_pages(b + 1, cmp_roles)

    nch = past_len // CMP_STRIDE
    kc = _compress(past_ref.at[0], nch, w1_ref, pe_ref, w2_ref, 0)
    vc = _compress(past_ref.at[1], nch, w1_ref, pe_ref, w2_ref, 1)

    scale = HEAD_DIM ** -0.5
    qv = q_ref[0]
    q = jnp.concatenate([qv[:, h * LANES:(h + 1) * LANES] for h in range(N_Q_HEADS)], axis=0) * scale
    qb = q.astype(BF16)
    qpos = past_len + lax.broadcasted_iota(jnp.int32, (SAMPLE_T, 1), 0)
    o_cmp, sels = _cmp_and_select(qb, qpos, SAMPLE_T, kc, vc, msel_ref[...], n_slc)

    pad = jnp.zeros((LANES - SAMPLE_T, LANES), F32)
    new = new_ref[0]
    k_new = jnp.concatenate([new[:, 2 * LANES:3 * LANES], pad], axis=0).astype(BF16)
    v_new = jnp.concatenate([new[:, 3 * LANES:4 * LANES], pad], axis=0).astype(BF16)
    for pg in range(n_pages):
        for role in slc_roles:
            page_copy(b, pg, role).wait()
    kvt_ref[0] = raw_ref[2].astype(BF16)
    kvt_ref[1] = raw_ref[3].astype(BF16)

    @pl.when(b + 1 < nb)
    def _():
        start_pages(b + 1, slc_roles)

    kt = kvt_ref[0]
    vt = kvt_ref[1]

    o_slc = _slc_branch(
        qb, qpos, SAMPLE_T, sels, eexp_ref[...], past_len + LANES,
        lambda qg: jnp.concatenate([_dot(qg, kt), _dot_nt(qg, k_new)], axis=1),
        lambda pr: _dot_nt(pr[:, 0:past_len], vt) + _dot(pr[:, past_len:], v_new))

    nw = nwin_ref[0]
    st = state_ref[0, 0]
    k_win = jnp.concatenate([st[:, 0:LANES], nw[:, 0:LANES], pad], axis=0)
    v_win = jnp.concatenate([st[:, LANES:2 * LANES], nw[:, LANES:2 * LANES], pad], axis=0)
    o_win = _win_branch(qb, qpos, SAMPLE_T, k_win, v_win, past_len - WINDOW)
    outs = _combine(gate_ref[0], o_cmp, o_slc, o_win, SAMPLE_T)
    for h in range(N_Q_HEADS):
        o_ref[0, :, h * LANES:(h + 1) * LANES] = outs[h]


def _attn_sample(page_table, q, new_rows, new_win, state_win, gates, cache, lw, consts, layer):
    bs, n_pages = page_table.shape
    msel, eexp, n_slc = consts
    past_len = n_pages * PAGE_SIZE
    per_b = lambda i, pt: (i, 0, 0)
    grid_spec = pltpu.PrefetchScalarGridSpec(
        num_scalar_prefetch=1,
        grid=(bs,),
        in_specs=[
            pl.BlockSpec((1, SAMPLE_T, Q_PAD), per_b),
            pl.BlockSpec((1, SAMPLE_T, 4 * LANES), per_b),
            pl.BlockSpec((1, SAMPLE_T, 2 * LANES), per_b),
            pl.BlockSpec((1, 1, WINDOW, 2 * LANES), lambda i, pt: (layer, i, 0, 0)),
            pl.BlockSpec((1, SAMPLE_T, LANES), per_b),
            pl.BlockSpec(memory_space=pl.ANY),
            pl.BlockSpec(lw["cmp_w1"].shape, lambda i, pt: (0,) * 5),
            pl.BlockSpec(lw["cmp_pe"].shape, lambda i, pt: (0,) * 4),
            pl.BlockSpec(lw["cmp_w2"].shape, lambda i, pt: (0,) * 3),
            pl.BlockSpec(msel.shape, lambda i, pt: (0, 0)),
            pl.BlockSpec(eexp.shape, lambda i, pt: (0, 0)),
        ],
        out_specs=pl.BlockSpec((1, SAMPLE_T, Q_PAD), per_b),
        scratch_shapes=[
            pltpu.VMEM((4, LANES, past_len), F32),
            pltpu.VMEM((2, past_len, LANES), F32),
            pltpu.VMEM((2, LANES, past_len), BF16),
            pltpu.SemaphoreType.DMA((2, n_pages)),
        ],
    )
    return pl.pallas_call(
        functools.partial(_attn_sample_kernel, layer=layer, n_pages=n_pages, n_slc=n_slc),
        out_shape=jax.ShapeDtypeStruct((bs, SAMPLE_T, Q_PAD), F32),
        grid_spec=grid_spec,
        compiler_params=_cparams(("arbitrary",)),
        name="attn_sample",
    )(page_table, q, new_rows, new_win, state_win, gates, cache,
      lw["cmp_w1"], lw["cmp_pe"], lw["cmp_w2"], msel, eexp)


def _out_kernel(oa_ref, gb_ref, u_ref, p1_ref, p2_ref, cw_ref, x_ref, gm_ref, og_ref, w_ref, n2_ref,
                x1_ref, h2_ref):
    gmat = gm_ref[...]
    y_conv = cw_ref[0:1, :] * p2_ref[...] + cw_ref[1:2, :] * p1_ref[...] + cw_ref[2:3, :] * u_ref[...]
    o_conv = gb_ref[...] * y_conv
    parts = []
    for k in range(N_Q_HEADS):
        parts.append(_group_norm(oa_ref[:, k * LANES:(k + 1) * LANES], gmat,
                                 og_ref[:, k * LANES:(k + 1) * LANES]))
    for k in range(CONV_DIM // LANES):
        parts.append(_group_norm(o_conv[:, k * LANES:(k + 1) * LANES], gmat,
                                 og_ref[:, Q_PAD + k * LANES:Q_PAD + (k + 1) * LANES]))
    mix = jnp.concatenate(parts, axis=1).astype(BF16)
    x1 = x_ref[...] + _dot(mix, w_ref[...])
    x1_ref[...] = x1
    ms = jnp.mean(x1 * x1, axis=-1, keepdims=True)
    h2_ref[...] = (x1 * lax.rsqrt(ms + RMS_EPS) * n2_ref[...]).T.astype(BF16)


def _out_proj(o_attn, gb, u, prev1, prev2, x, lw, tm):
    n = x.shape[0]
    row = lambda i: (i, 0)
    fixed = lambda i: (0, 0)
    return pl.pallas_call(
        _out_kernel,
        out_shape=(jax.ShapeDtypeStruct((n, D_MODEL), F32), jax.ShapeDtypeStruct((D_MODEL, n), BF16)),
        grid=(n // tm,),
        in_specs=[
            pl.BlockSpec((tm, Q_PAD), row),
            pl.BlockSpec((tm, CONV_DIM), row),
            pl.BlockSpec((tm, CONV_DIM), row),
            pl.BlockSpec((tm, CONV_DIM), row),
            pl.BlockSpec((tm, CONV_DIM), row),
            pl.BlockSpec((SUBLANES, CONV_DIM), fixed),
            pl.BlockSpec((tm, D_MODEL), row),
            pl.BlockSpec((LANES, LANES), fixed),
            pl.BlockSpec((1, MIX_PAD), fixed),
            pl.BlockSpec((MIX_PAD, D_MODEL), fixed),
            pl.BlockSpec((1, D_MODEL), fixed),
        ],
        out_specs=(pl.BlockSpec((tm, D_MODEL), row), pl.BlockSpec((D_MODEL, tm), lambda i: (0, i))),
        compiler_params=_cparams(("parallel",)),
        name="out_proj",
    )(o_attn, gb, u, prev1, prev2, lw["conv_w"], x, lw["gmat"], lw["out_g"], lw["w_out"], lw["norm2"])


_CAND_ROWS = 80
PEER_EB = 2048
PEER_SUB = 256
ROUTE_TOKENS = 256


def _route_kernel(h_ref, wq_ref, sk_ref, a1_ref, cnt_ref, b_ref, r2_ref,
                  qt_ref, s_ref, so_ref, rank_ref, v_ref, cand_ref):
    tn = h_ref.shape[1]
    k = PEER_TOPK
    qt_ref[...] = _dot(wq_ref[...], h_ref[...])

    def twice_bf16(x):
        bits = pltpu.bitcast(x.astype(BF16).astype(F32), jnp.uint32)
        return bits | (bits >> 16)
    rowid = lax.broadcasted_iota(jnp.int32, (PEER_NKEYS, tn), 0).astype(F32)

    crow = lax.broadcasted_iota(jnp.int32, (_CAND_ROWS, tn), 0)
    cr = jnp.where(crow < 16, 0, jnp.where(crow < 72, 1 + (crow - 16) // 8, crow - 64))
    cj = jnp.where(crow < 16, crow, jnp.where(crow < 72, (crow - 16) % 8, 0))
    cvalid = (cr + 1) * (cj + 1) <= k
    cflat = (cr * k + cj).astype(F32)
    rid16 = lax.broadcasted_iota(jnp.int32, (k, tn), 0).astype(F32)
    neg = -jnp.inf

    def head_body(h, carry):
        for p in range(2):
            base = pl.multiple_of(h * PEER_QDIM + p * (PEER_QDIM // 2), PEER_QDIM // 2)
            s0 = _dot(sk_ref[2 * h + p], qt_ref[pl.ds(base, PEER_QDIM // 2), :].astype(BF16))
            so_ref[p] = s0
            s_ref[p] = s0
            rank_ref[p] = jnp.full((PEER_NKEYS, tn), float(k), F32)
        for r in range(k):
            for p in range(2):
                s = s_ref[p]
                m = jnp.max(s, axis=0, keepdims=True)
                idx = jnp.min(jnp.where(s == m, rowid, float(PEER_NKEYS)), axis=0, keepdims=True)
                hit = rowid == idx
                s_ref[p] = jnp.where(hit, neg, s)
                rank_ref[p] = jnp.where(hit, float(r), rank_ref[p])
                v_ref[p, r:r + 1, :] = m
        v1 = v_ref[0]
        v2 = v_ref[1]
        cand_ref[0:16, :] = v1[0:1] + v2
        for r in range(1, 8):
            cand_ref[8 + 8 * r:16 + 8 * r, :] = v1[r:r + 1] + v2[0:8]
        cand_ref[72:80, :] = v1[8:16] + v2[0:1]
        cand_ref[...] = jnp.where(cvalid, cand_ref[...], neg)
        cmax = v1[0:1] + v2[0:1]
        cnt = jnp.zeros((k, tn), F32)
        zsum = jnp.zeros((1, tn), F32)
        for _ in range(k):
            c = cand_ref[...]
            m = jnp.max(c, axis=0, keepdims=True)
            f = jnp.min(jnp.where(c == m, cflat, 1e9), axis=0, keepdims=True)
            cand_ref[...] = jnp.where(cflat == f, neg, c)
            cnt = cnt + jnp.where(rid16 == jnp.floor(f * (1.0 / k)), 1.0, 0.0)
            zsum = zsum + jnp.exp(m - cmax)
        r1 = rank_ref[0]
        r2 = rank_ref[1]
        a1_ref[h] = twice_bf16(jnp.where(r1 < float(k), jnp.exp(so_ref[0] - v1[0:1]), 0.0))
        bval = jnp.where(r2 < float(k), jnp.exp(so_ref[1] - v2[0:1]), 0.0) / zsum
        b_ref[h] = pltpu.bitcast(bval.astype(BF16), jnp.uint32)
        r2_ref[h] = pltpu.bitcast(r2.astype(BF16), jnp.uint32)
        cdense = jnp.zeros((PEER_NKEYS, tn), F32)
        for r in range(k):
            cdense = jnp.where(r1 == float(r), cnt[r:r + 1], cdense)
        cnt_ref[h] = twice_bf16(cdense)
        return carry

    lax.fori_loop(0, PEER_HEADS, head_body, 0)


def _route(h2, lw, tn):
    n = h2.shape[1]
    shape = jax.ShapeDtypeStruct((PEER_HEADS, PEER_NKEYS, n), jnp.uint32)
    shape_b = jax.ShapeDtypeStruct((PEER_HEADS, PEER_NKEYS // 2, n), jnp.uint32)
    spec = pl.BlockSpec((PEER_HEADS, PEER_NKEYS, tn), lambda i: (0, 0, i))
    spec_b = pl.BlockSpec((PEER_HEADS, PEER_NKEYS // 2, tn), lambda i: (0, 0, i))
    return pl.pallas_call(
        _route_kernel,
        out_shape=(shape, shape, shape_b, shape_b),
        grid=(n // tn,),
        in_specs=[
            pl.BlockSpec((D_MODEL, tn), lambda i: (0, i)),
            pl.BlockSpec((PEER_HEADS * PEER_QDIM, D_MODEL), lambda i: (0, 0)),
            pl.BlockSpec((2 * PEER_HEADS, PEER_NKEYS, PEER_QDIM // 2), lambda i: (0, 0, 0)),
        ],
        out_specs=(spec, spec, spec_b, spec_b),
        scratch_shapes=[
            pltpu.VMEM((PEER_HEADS * PEER_QDIM, tn), F32),
            pltpu.VMEM((2, PEER_NKEYS, tn), F32),
            pltpu.VMEM((2, PEER_NKEYS, tn), F32),
            pltpu.VMEM((2, PEER_NKEYS, tn), F32),
            pltpu.VMEM((2, PEER_TOPK, tn), F32),
            pltpu.VMEM((_CAND_ROWS, tn), F32),
        ],
        compiler_params=_cparams(("parallel",)),
        name="peer_route",
    )(h2, lw["peer_wq_t"], lw["peer_sk"])


def _peer_kernel(h_ref, u_ref, vt_ref, a1_ref, cnt_ref, b_ref, r2_ref, x_ref, y_ref,
                 acc_ref, at_ref, *, eb, sub):
    e = pl.program_id(1)
    tn = h_ref.shape[1]
    pack = 2 * SUBLANES

    @pl.when(e == 0)
    def _():
        acc_ref[...] = jnp.zeros_like(acc_ref)

    h = h_ref[...]
    zero = jnp.zeros((pack, LANES), BF16)

    def row_pair(ref, hh, ai, cols):
        return pltpu.bitcast(jnp.broadcast_to(ref[hh, ai:ai + 1, cols], (SUBLANES, LANES)), BF16)

    def sub_block(sb):
        r0 = sb * sub
        st = _dot(u_ref[r0:r0 + sub, :], h)
        n_a = sub // PEER_NKEYS
        n_k = PEER_NKEYS // pack
        for c in range(tn // LANES):
            cols = slice(c * LANES, (c + 1) * LANES)
            w = [[None] * n_k for _ in range(n_a)]
            for hh in range(PEER_HEADS):
                cnt = [row_pair(cnt_ref, hh, sb * n_a + a, cols) for a in range(n_a)]
                a1 = [row_pair(a1_ref, hh, sb * n_a + a, cols) for a in range(n_a)]
                for k in range(n_k):
                    rk = slice(k * SUBLANES, (k + 1) * SUBLANES)
                    r2 = pltpu.bitcast(r2_ref[hh, rk, cols], BF16)
                    bb = pltpu.bitcast(b_ref[hh, rk, cols], BF16)
                    for a in range(n_a):
                        t = a1[a] * jnp.where(r2 < cnt[a], bb, zero)
                        w[a][k] = t if w[a][k] is None else w[a][k] + t
            for a in range(n_a):
                g = jax.nn.gelu(st[a * PEER_NKEYS:(a + 1) * PEER_NKEYS, cols]).astype(BF16)
                rows = slice(r0 + a * PEER_NKEYS, r0 + (a + 1) * PEER_NKEYS)
                at_ref[rows, cols] = jnp.concatenate(w[a], axis=0) * g

    n_sb = eb // sub
    half = eb // 2
    tot = None
    for sb in range(n_sb):
        sub_block(sb)
        if (sb + 1) * sub % half == 0:
            k0 = (sb + 1) * sub - half
            d = _dot(vt_ref[:, k0:k0 + half], at_ref[k0:k0 + half, :])
            tot = d if tot is None else tot + d
    acc_ref[...] += tot

    @pl.when(e == pl.num_programs(1) - 1)
    def _():
        y_ref[...] = x_ref[...] + acc_ref[...].T


def _peer(h2, routed, x1, lw, tn, eb):
    n = h2.shape[1]
    n_exp = PEER_NKEYS * PEER_NKEYS
    a1, cnt, bb, r2 = routed
    tok = lambda i, e: (i, 0)
    key_rows = pl.BlockSpec((PEER_HEADS, eb // PEER_NKEYS, tn), lambda i, e: (0, e, i))
    key_full = pl.BlockSpec((PEER_HEADS, PEER_NKEYS // 2, tn), lambda i, e: (0, 0, i))
    return pl.pallas_call(
        functools.partial(_peer_kernel, eb=eb, sub=PEER_SUB),
        out_shape=jax.ShapeDtypeStruct((n, D_MODEL), F32),
        grid=(n // tn, n_exp // eb),
        in_specs=[
            pl.BlockSpec((D_MODEL, tn), lambda i, e: (0, i)),
            pl.BlockSpec((eb, D_MODEL), lambda i, e: (e, 0)),
            pl.BlockSpec((D_MODEL, eb), lambda i, e: (0, e)),
            key_rows, key_rows, key_full, key_full,
            pl.BlockSpec((tn, D_MODEL), tok),
        ],
        out_specs=pl.BlockSpec((tn, D_MODEL), tok),
        scratch_shapes=[pltpu.VMEM((D_MODEL, tn), F32), pltpu.VMEM((eb, tn), BF16)],
        compiler_params=_cparams(("parallel", "arbitrary")),
        name="peer_experts",
    )(h2, lw["peer_u"], lw["peer_vt"], a1, cnt, bb, r2, x1)


def _rope_tables(pos):
    inv_freq = ROPE_THETA ** (-jnp.arange(ROPE_HALF, dtype=F32) / ROPE_HALF)
    ang = pos.astype(F32)[:, None] * inv_freq[None, :]
    cos, sin = jnp.cos(ang), jnp.sin(ang)
    n = pos.shape[0]
    ones = jnp.ones((n, HEAD_DIM - ROPE_DIM), F32)
    zeros = jnp.zeros((n, HEAD_DIM - ROPE_DIM), F32)
    zh = jnp.zeros((n, ROPE_HALF), F32)
    c = jnp.concatenate([cos, cos, ones], axis=1)
    s_up = jnp.concatenate([-sin, zh, zeros], axis=1)
    s_dn = jnp.concatenate([zh, sin, zeros], axis=1)
    return tuple(jnp.tile(a, (1, 2)) for a in (c, s_up, s_dn))


def _attn_consts(n_cmp, nch, n_slc, slc_len):
    jn = -(-n_slc // LANES) * LANES
    sr = SLC_BLOCK // CMP_STRIDE
    r = CMP_LEN // CMP_STRIDE
    msel = np.zeros((nch, jn), np.float32)
    for j in range(n_slc):
        for m in range(sr):
            for n in range(r):
                i = sr * j + m - n
                if 0 <= i < n_cmp:
                    msel[i, j] += 1.0
    eexp = np.zeros((jn, slc_len), np.float32)
    keys = np.arange(slc_len)
    eexp[keys // SLC_BLOCK, keys] = 1.0
    return jnp.asarray(msel, BF16), jnp.asarray(eexp, BF16), n_slc


def _pad_heads(x, axis):
    shape = x.shape
    x = jnp.moveaxis(x, axis, -1).reshape(shape[:axis] + shape[axis + 1:] + (N_KV_HEADS, HEADS_PER_GROUP, HEAD_DIM))
    nd = x.ndim
    lo = jnp.pad(x[..., 0, :, :], [(0, 0)] * (nd - 2) + [(0, HEAD_DIM)])
    hi = jnp.pad(x[..., 1, :, :], [(0, 0)] * (nd - 2) + [(HEAD_DIM, 0)])
    out = jnp.concatenate([lo, hi], axis=-2)
    out = out.reshape(out.shape[:-2] + (Q_PAD,))
    return jnp.moveaxis(out, -1, axis)


def _prep_weights(norm1_g, w_in, q_norm_g, k_norm_g, cmp_pe, cmp_w1, cmp_w2, conv_w, out_norm_g,
                  w_out, norm2_g, peer_wq, peer_subkeys, peer_u, peer_v):
    depth = w_in.shape[0]
    c1 = NSA_WIDTH
    c2 = c1 + 6 * KV_WIDTH
    c3 = c2 + N_Q_HEADS * N_BRANCH
    gl_pad = jnp.zeros((depth, D_MODEL, LANES - N_Q_HEADS * N_BRANCH), F32)
    w_all = jnp.concatenate([_pad_heads(w_in[:, :, :c1], 2), w_in[:, :, c1:c2], w_in[:, :, c3:],
                             w_in[:, :, c2:c3], gl_pad], axis=2)

    blk = np.kron(np.eye(2, dtype=np.float32), np.full((HEAD_DIM, HEAD_DIM), 1.0 / HEAD_DIM, np.float32))
    kg = jnp.pad(jnp.tile(k_norm_g, (1, 1, 2)), ((0, 0), (0, SUBLANES - N_BRANCH), (0, 0)))

    r = CMP_LEN // CMP_STRIDE
    eye2 = jnp.eye(2, dtype=F32)
    w1 = cmp_w1.reshape(depth, 2, r, CMP_STRIDE, HEAD_DIM, HEAD_DIM)
    w1 = jnp.einsum('lxmsde,gk->lxmsgdke', w1, eye2).reshape(depth, 2, r, CMP_STRIDE // 2, 2 * LANES, LANES)
    pe = jnp.tile(cmp_pe.reshape(depth, 2, r, CMP_STRIDE, HEAD_DIM), (1, 1, 1, 1, 2))
    w2 = jnp.einsum('lxde,gk->lxgdke', cmp_w2, eye2).reshape(depth, 2, LANES, LANES)

    og_pad = jnp.concatenate([_pad_heads(out_norm_g[:, :NSA_WIDTH], 1), out_norm_g[:, NSA_WIDTH:]], axis=1)
    wo_pad = jnp.concatenate([_pad_heads(w_out[:, :NSA_WIDTH], 1), w_out[:, NSA_WIDTH:]], axis=1)
    return {
        "norm1": norm1_g[:, None, :],
        "w_in": w_all.astype(BF16),
        "gmat": jnp.broadcast_to(jnp.asarray(blk, BF16), (depth, LANES, LANES)),
        "q_g": jnp.tile(q_norm_g, (1, 2))[:, None, :],
        "k_g": kg,
        "cmp_w1": w1.astype(BF16),
        "cmp_pe": pe,
        "cmp_w2": w2.astype(BF16),
        "conv_w": jnp.pad(conv_w, ((0, 0), (0, SUBLANES - CONV_WIDTH), (0, 0))),
        "out_g": og_pad[:, None, :],
        "w_out": wo_pad.astype(BF16),
        "norm2": norm2_g[:, None, :],
        "peer_wq_t": jnp.swapaxes(peer_wq, 1, 2).astype(BF16),
        "peer_sk": peer_subkeys.reshape(depth, 2 * PEER_HEADS, PEER_NKEYS, PEER_QDIM // 2).astype(BF16),
        "peer_u": peer_u.astype(BF16),
        "peer_vt": jnp.swapaxes(peer_v, 1, 2).astype(BF16),
    }


def _conv_prev(u, prefix, b, t):
    ext = jnp.concatenate([prefix, u.reshape(b, t, CONV_DIM)], axis=1)
    return ext[:, 1:t + 1].reshape(b * t, CONV_DIM), ext[:, 0:t].reshape(b * t, CONV_DIM)


def _row_tile(n):
    for tm in (256, 128):
        if n % tm == 0:
            return tm
    raise ValueError(f"token count {n} must be a multiple of 128")


def _peer_tile(n):
    for tn in (512, 256, 128):
        if n % tn == 0:
            return tn
    raise ValueError(f"token count {n} must be a multiple of 128")


def kernel(x_prompt, x_sample, cache_nsa_kv, state_win_kv, state_conv, page_table,
           norm1_g, w_in, q_norm_g, k_norm_g, cmp_pe, cmp_w1, cmp_w2, conv_w, out_norm_g, w_out,
           norm2_g, peer_wq, peer_subkeys, peer_u, peer_v):
    bp, t, _ = x_prompt.shape
    bs, ts, _ = x_sample.shape
    depth = norm1_g.shape[0]
    n_pages = page_table.shape[1]
    past_len = n_pages * PAGE_SIZE
    assert t % SLC_KEY_STEP == 0 and t >= WINDOW + Q_BLOCK and ts <= SAMPLE_T
    assert state_win_kv.shape[2] == WINDOW and past_len % SLC_BLOCK == 0

    xp = x_prompt.reshape(bp * t, D_MODEL)
    xs = jnp.pad(x_sample, ((0, 0), (0, SAMPLE_T - ts), (0, 0))).reshape(bs * SAMPLE_T, D_MODEL)
    cache = jnp.transpose(cache_nsa_kv, (0, 2, 3, 4, 5, 1)).reshape(
        cache_nsa_kv.shape[0], depth * 4, LANES, PAGE_SIZE)
    state_win = state_win_kv.reshape(depth, bs, WINDOW, 2 * LANES)

    tabs_p = _rope_tables(jnp.arange(t))
    tabs_s = _rope_tables(jnp.tile(past_len + jnp.arange(SAMPLE_T), bs))
    n_cmp_p = (t - CMP_LEN) // CMP_STRIDE + 1
    consts_p = _attn_consts(n_cmp_p, t // CMP_STRIDE, t // SLC_BLOCK, t)
    t_all = past_len + ts
    n_cmp_s = (t_all - CMP_LEN) // CMP_STRIDE + 1
    assert n_cmp_s <= past_len // CMP_STRIDE
    consts_s = _attn_consts(n_cmp_s, past_len // CMP_STRIDE, -(-t_all // SLC_BLOCK), past_len + LANES)
    zero_prefix = jnp.zeros((bp, CONV_WIDTH - 1, CONV_DIM), F32)

    weights = _prep_weights(norm1_g, w_in, q_norm_g, k_norm_g, cmp_pe, cmp_w1, cmp_w2, conv_w, out_norm_g,
                            w_out, norm2_g, peer_wq, peer_subkeys, peer_u, peer_v)
    rows_p, rows_s, win_p, win_s, conv_p, conv_s = [], [], [], [], [], []
    for l in range(depth):
        lw = {name: stacked[l] for name, stacked in weights.items()}
        q, rows, win, gates, gb, u = _in_proj(xp, lw, tabs_p, _row_tile(bp * t))
        kc, vc = _compress_prompt(rows, lw, bp, t)
        o_attn = _attn_prompt(q, rows, win, kc, vc, gates, consts_p, bp, t)
        p1, p2 = _conv_prev(u, zero_prefix, bp, t)
        x1, h2 = _out_proj(o_attn, gb, u, p1, p2, xp, lw, _row_tile(bp * t))
        routed = _route(h2, lw, ROUTE_TOKENS if h2.shape[1] % ROUTE_TOKENS == 0 else LANES)
        xp = _peer(h2, routed, x1, lw, _peer_tile(bp * t), PEER_EB)
        rows_p.append(rows.reshape(bp, t, 4, N_KV_HEADS, HEAD_DIM))
        win_p.append(win.reshape(bp, t, 2, N_KV_HEADS, HEAD_DIM)[:, t - WINDOW:])
        conv_p.append(u.reshape(bp, t, CONV_DIM)[:, t - (CONV_WIDTH - 1):])

        q, rows, win, gates, gb, u = _in_proj(xs, lw, tabs_s, _row_tile(bs * SAMPLE_T))
        o_attn = _attn_sample(
            page_table, q.reshape(bs, SAMPLE_T, Q_PAD), rows.reshape(bs, SAMPLE_T, 4 * LANES),
            win.reshape(bs, SAMPLE_T, 2 * LANES), state_win, gates.reshape(bs, SAMPLE_T, LANES),
            cache, lw, consts_s, l).reshape(bs * SAMPLE_T, Q_PAD)
        p1, p2 = _conv_prev(u, state_conv[l], bs, SAMPLE_T)
        x1, h2 = _out_proj(o_attn, gb, u, p1, p2, xs, lw, _row_tile(bs * SAMPLE_T))
        routed = _route(h2, lw, ROUTE_TOKENS if h2.shape[1] % ROUTE_TOKENS == 0 else LANES)
        xs = _peer(h2, routed, x1, lw, _peer_tile(bs * SAMPLE_T), PEER_EB)
        rows_s.append(rows.reshape(bs, SAMPLE_T, 4, N_KV_HEADS, HEAD_DIM)[:, :ts])
        new_win = win.reshape(bs, SAMPLE_T, 2, N_KV_HEADS, HEAD_DIM)[:, :ts]
        old_win = state_win_kv[l]
        win_s.append(jnp.concatenate([old_win, new_win], axis=1)[:, ts:])
        conv_s.append(u.reshape(bs, SAMPLE_T, CONV_DIM)[:, ts - (CONV_WIDTH - 1):ts])

    y_prompt = xp.reshape(bp, t, D_MODEL)
    y_sample = xs.reshape(bs, SAMPLE_T, D_MODEL)[:, :ts]
    return (y_prompt, y_sample,
            jnp.stack(rows_p, axis=2), jnp.stack(rows_s, axis=2),
            jnp.stack(win_p, axis=0), jnp.stack(win_s, axis=0),
            jnp.stack(conv_p, axis=0), jnp.stack(conv_s, axis=0))
```

```python
import functools
import math

import numpy as np
import jax
import jax.numpy as jnp
from jax import lax
from jax.experimental import pallas as pl
from jax.experimental.pallas import tpu as pltpu

F32 = jnp.float32
BF16 = jnp.bfloat16

LANES = 128
SUBLANES = 8
VMEM_LIMIT = 56 * 1024 * 1024

D_MODEL = 1024
HEAD_DIM = 64
N_Q_HEADS = 8
N_KV_HEADS = 2
HEADS_PER_GROUP = N_Q_HEADS // N_KV_HEADS
NSA_WIDTH = N_Q_HEADS * HEAD_DIM
KV_WIDTH = N_KV_HEADS * HEAD_DIM
CONV_DIM = D_MODEL - NSA_WIDTH
CONV_WIDTH = 3
N_BRANCH = 3
ROPE_DIM = HEAD_DIM // 4
ROPE_HALF = ROPE_DIM // 2
ROPE_THETA = 500000.0
CMP_LEN = 32
CMP_STRIDE = 16
SLC_BLOCK = 64
N_SELECT = 16
WINDOW = 512
Q_BLOCK = 128
PAGE_SIZE = 128
PEER_HEADS = 8
PEER_QDIM = 256
PEER_NKEYS = 128
PEER_TOPK = 16
RMS_EPS = 1e-6
NEG_INF = -1e30
FORCE_BONUS = 1e4

Q_PAD = N_Q_HEADS * LANES
IN_COLS = Q_PAD + 6 * KV_WIDTH + 3 * CONV_DIM + LANES
MIX_PAD = Q_PAD + CONV_DIM
SAMPLE_T = 8
SLC_KEY_STEP = 256

_NT = (((1,), (1,)), ((), ()))


def _cparams(sem):
    return pltpu.CompilerParams(dimension_semantics=sem, vmem_limit_bytes=VMEM_LIMIT)


def _dot(a, b):
    return jnp.dot(a, b, preferred_element_type=F32)


def _dot_nt(a, b):
    return lax.dot_general(a, b, _NT, preferred_element_type=F32)


def _group_norm(x, gmat, gain):
    ss = _dot((x * x).astype(BF16), gmat)
    return x * lax.rsqrt(ss + RMS_EPS) * gain


def _in_kernel(x_ref, g1_ref, w_ref, gm_ref, qg_ref, kg_ref, c_ref, s1_ref, s2_ref,
               q_ref, rows_ref, win_ref, gate_ref, gb_ref, u_ref):
    x = x_ref[...]
    ms = jnp.mean(x * x, axis=-1, keepdims=True)
    h = (x * lax.rsqrt(ms + RMS_EPS) * g1_ref[...]).astype(BF16)
    z = _dot(h, w_ref[...])
    gmat = gm_ref[...]
    cos = c_ref[...]
    sin_up = s1_ref[...]
    sin_dn = s2_ref[...]

    def norm_rope(zb, gain):
        y = _group_norm(zb, gmat, gain)
        return (y * cos + pltpu.roll(y, LANES - ROPE_HALF, 1) * sin_up
                + pltpu.roll(y, ROPE_HALF, 1) * sin_dn)

    for hb in range(N_Q_HEADS):
        q_ref[:, hb * LANES:(hb + 1) * LANES] = norm_rope(z[:, hb * LANES:(hb + 1) * LANES], qg_ref[...])
    o = Q_PAD
    kv = [z[:, o + r * LANES:o + (r + 1) * LANES] for r in range(6)]
    rows_ref[:, 0 * LANES:1 * LANES] = norm_rope(kv[0], kg_ref[0:1, :])
    rows_ref[:, 1 * LANES:2 * LANES] = kv[1]
    rows_ref[:, 2 * LANES:3 * LANES] = norm_rope(kv[2], kg_ref[1:2, :])
    rows_ref[:, 3 * LANES:4 * LANES] = kv[3]
    win_ref[:, 0:LANES] = norm_rope(kv[4], kg_ref[2:3, :])
    win_ref[:, LANES:2 * LANES] = kv[5]
    o += 6 * LANES
    gb_ref[...] = z[:, o:o + CONV_DIM]
    u_ref[...] = z[:, o + CONV_DIM:o + 2 * CONV_DIM] * z[:, o + 2 * CONV_DIM:o + 3 * CONV_DIM]
    o += 3 * CONV_DIM
    gate_ref[...] = jax.nn.sigmoid(z[:, o:o + LANES])


def _in_proj(x, lw, tabs, tm):
    n = x.shape[0]
    cos, sin_up, sin_dn = tabs
    npos_tiles = cos.shape[0] // tm
    row = lambda i: (i, 0)
    fixed = lambda i: (0, 0)
    pos = lambda i: (i % npos_tiles, 0)
    out_shape = (
        jax.ShapeDtypeStruct((n, Q_PAD), F32),
        jax.ShapeDtypeStruct((n, 4 * LANES), F32),
        jax.ShapeDtypeStruct((n, 2 * LANES), F32),
        jax.ShapeDtypeStruct((n, LANES), F32),
        jax.ShapeDtypeStruct((n, CONV_DIM), F32),
        jax.ShapeDtypeStruct((n, CONV_DIM), F32),
    )
    return pl.pallas_call(
        _in_kernel,
        out_shape=out_shape,
        grid=(n // tm,),
        in_specs=[
            pl.BlockSpec((tm, D_MODEL), row),
            pl.BlockSpec((1, D_MODEL), fixed),
            pl.BlockSpec((D_MODEL, IN_COLS), fixed),
            pl.BlockSpec((LANES, LANES), fixed),
            pl.BlockSpec((1, LANES), fixed),
            pl.BlockSpec((SUBLANES, LANES), fixed),
            pl.BlockSpec((tm, LANES), pos),
            pl.BlockSpec((tm, LANES), pos),
            pl.BlockSpec((tm, LANES), pos),
        ],
        out_specs=(
            pl.BlockSpec((tm, Q_PAD), row),
            pl.BlockSpec((tm, 4 * LANES), row),
            pl.BlockSpec((tm, 2 * LANES), row),
            pl.BlockSpec((tm, LANES), row),
            pl.BlockSpec((tm, CONV_DIM), row),
            pl.BlockSpec((tm, CONV_DIM), row),
        ),
        compiler_params=_cparams(("parallel",)),
        name="in_proj",
    )(x, lw["norm1"], lw["w_in"], lw["gmat"], lw["q_g"], lw["k_g"], cos, sin_up, sin_dn)


def _compress(src_ref, nch, w1_ref, pe_ref, w2_ref, role):
    acc0 = jnp.zeros((nch, LANES), F32)
    acc1 = jnp.zeros((nch, LANES), F32)
    for s2 in range(CMP_STRIDE // 2):
        xs = []
        for s in (2 * s2, 2 * s2 + 1):
            xs.append(src_ref[pl.ds(s, nch, stride=CMP_STRIDE), :])
        x0 = jnp.concatenate([xs[0] + pe_ref[role, 0, 2 * s2:2 * s2 + 1, :],
                              xs[1] + pe_ref[role, 0, 2 * s2 + 1:2 * s2 + 2, :]], axis=1)
        x1 = jnp.concatenate([xs[0] + pe_ref[role, 1, 2 * s2:2 * s2 + 1, :],
                              xs[1] + pe_ref[role, 1, 2 * s2 + 1:2 * s2 + 2, :]], axis=1)
        acc0 = acc0 + _dot(x0.astype(BF16), w1_ref[role, 0, s2])
        acc1 = acc1 + _dot(x1.astype(BF16), w1_ref[role, 1, s2])
    hid = acc0 + pltpu.roll(acc1, nch - 1, 0)
    return _dot(jax.nn.gelu(hid).astype(BF16), w2_ref[role])


def _cmp_kernel(k_ref, v_ref, w1_ref, pe_ref, w2_ref, kc_ref, vc_ref, *, nch):
    kc_ref[0] = _compress(k_ref, nch, w1_ref, pe_ref, w2_ref, 0)
    vc_ref[0] = _compress(v_ref, nch, w1_ref, pe_ref, w2_ref, 1)


def _compress_prompt(rows, lw, b, t):
    nch = t // CMP_STRIDE
    fixed = lambda i: (0,) * 5
    return pl.pallas_call(
        functools.partial(_cmp_kernel, nch=nch),
        out_shape=(jax.ShapeDtypeStruct((b, nch, LANES), F32),) * 2,
        grid=(b,),
        in_specs=[
            pl.BlockSpec((t, LANES), lambda i: (i, 0)),
            pl.BlockSpec((t, LANES), lambda i: (i, 1)),
            pl.BlockSpec(lw["cmp_w1"].shape, fixed),
            pl.BlockSpec(lw["cmp_pe"].shape, lambda i: (0,) * 4),
            pl.BlockSpec(lw["cmp_w2"].shape, lambda i: (0,) * 3),
        ],
        out_specs=(pl.BlockSpec((1, nch, LANES), lambda i: (i, 0, 0)),) * 2,
        compiler_params=_cparams(("parallel",)),
        name="compress_prompt",
    )(rows, rows, lw["cmp_w1"], lw["cmp_pe"], lw["cmp_w2"])


def _masked_softmax(s, mask):
    s = jnp.where(mask, s, NEG_INF)
    p = jnp.where(mask, jnp.exp(s - jnp.max(s, axis=-1, keepdims=True)), 0.0)
    return p / jnp.maximum(jnp.sum(p, axis=-1, keepdims=True), 1e-30)


def _split3(x):
    h1 = x.astype(BF16)
    r1 = x - h1.astype(F32)
    h2 = r1.astype(BF16)
    h3 = (r1 - h2.astype(F32)).astype(BF16)
    return h1, h2, h3


def _cmp_and_select(qb, qpos, nq, kc, vc, msel, n_slc):
    nrow = N_Q_HEADS * nq
    grow = HEADS_PER_GROUP * nq
    qpos_rows = jnp.concatenate([qpos] * N_Q_HEADS, axis=0)
    nch = kc.shape[0]
    s = _dot_nt(qb, kc.astype(BF16))
    cend = lax.broadcasted_iota(jnp.int32, (nrow, nch), 1) * CMP_STRIDE + (CMP_LEN - 1)
    p = _masked_softmax(s, cend <= qpos_rows)
    o_cmp = _dot(p.astype(BF16), vc.astype(BF16))

    jn = msel.shape[1]
    jid = lax.broadcasted_iota(jnp.int32, (nq, jn), 1)
    cur = qpos // SLC_BLOCK
    valid = jid * SLC_BLOCK <= qpos
    forced = (jid == 0) | (jid == cur) | (jid == cur - 1)
    sels = []
    for g in range(N_KV_HEADS):
        pg = p[g * grow:g * grow + nq]
        for hh in range(1, HEADS_PER_GROUP):
            pg = pg + p[g * grow + hh * nq:g * grow + (hh + 1) * nq]
        p_slc = sum(_dot(part, msel) for part in _split3(pg))
        score = jnp.where(valid, p_slc + jnp.where(forced, FORCE_BONUS, 0.0), NEG_INF)
        score = jnp.where(jid < n_slc, score, -3e38)
        rank = jnp.zeros((nq, jn), F32)
        for i in range(n_slc):
            col = score[:, i:i + 1]
            tie = jnp.where(jid > i, 1.0, 0.0)
            rank = rank + jnp.where(col > score, 1.0, jnp.where(col == score, tie, 0.0))
        sels.append(jnp.where(rank < float(min(N_SELECT, n_slc)), 1.0, 0.0))
    return o_cmp, sels


def _softmax_pv(s, bias, values_fn):
    s = s + bias
    e = jnp.exp(s - jnp.max(s, axis=-1, keepdims=True))
    return values_fn(e.astype(BF16)) / jnp.sum(e, axis=-1, keepdims=True)


def _slc_branch(qb, qpos, nq, sels, eexp, klen, scores_fn, values_fn):
    grow = HEADS_PER_GROUP * nq
    causal = lax.broadcasted_iota(jnp.int32, (nq, klen), 1) <= qpos
    chosen = _dot(jnp.concatenate(sels, axis=0).astype(BF16), eexp)
    outs = []
    for g in range(N_KV_HEADS):
        bias_g = jnp.where(causal, jnp.where(chosen[g * nq:(g + 1) * nq] > 0.5, 0.0, NEG_INF), NEG_INF)
        bias = jnp.concatenate([bias_g] * HEADS_PER_GROUP, axis=0)
        s = scores_fn(qb[g * grow:(g + 1) * grow])
        outs.append(_softmax_pv(s, bias, values_fn))
    return jnp.concatenate(outs, axis=0)


def _win_branch(qb, qpos, nq, k_win, v_win, wpos0):
    wpos = wpos0 + lax.broadcasted_iota(jnp.int32, (nq, k_win.shape[0]), 1)
    bias_q = jnp.where(wpos <= qpos, jnp.where(wpos > qpos - WINDOW, 0.0, NEG_INF), NEG_INF)
    bias = jnp.concatenate([bias_q] * N_Q_HEADS, axis=0)
    vb = v_win.astype(BF16)
    return _softmax_pv(_dot_nt(qb, k_win.astype(BF16)), bias, lambda pr: _dot(pr, vb))


def _combine(gates, o_cmp, o_slc, o_win, nq):
    lane = lax.broadcasted_iota(jnp.int32, (nq, LANES), 1)
    outs = []
    for h in range(N_Q_HEADS):
        r0 = h * nq
        o = (gates[:, 3 * h:3 * h + 1] * o_cmp[r0:r0 + nq]
             + gates[:, 3 * h + 1:3 * h + 2] * o_slc[r0:r0 + nq]
             + gates[:, 3 * h + 2:3 * h + 3] * o_win[r0:r0 + nq])
        g = h // HEADS_PER_GROUP
        outs.append(jnp.where((lane >= g * HEAD_DIM) & (lane < (g + 1) * HEAD_DIM), o, 0.0))
    return outs


def _attn_prompt_kernel(q_ref, rows_ref, win_ref, kc_ref, vc_ref, gate_ref, msel_ref, eexp_ref,
                        o_ref, oslc_ref, *, t, n_slc, key_step):
    qblk = pl.program_id(1)
    start = qblk * Q_BLOCK
    scale = HEAD_DIM ** -0.5
    q = jnp.concatenate([q_ref[:, h * LANES:(h + 1) * LANES] for h in range(N_Q_HEADS)], axis=0) * scale
    qb = q.astype(BF16)
    qpos = start + lax.broadcasted_iota(jnp.int32, (Q_BLOCK, 1), 0)
    o_cmp, sels = _cmp_and_select(qb, qpos, Q_BLOCK, kc_ref[0], vc_ref[0], msel_ref[...], n_slc)

    def slc_for(klen):
        ksb = rows_ref[0:klen, 2 * LANES:3 * LANES].astype(BF16)
        vsb = rows_ref[0:klen, 3 * LANES:4 * LANES].astype(BF16)
        oslc_ref[...] = _slc_branch(qb, qpos, Q_BLOCK, sels, eexp_ref[:, 0:klen], klen,
                                    lambda qg: _dot_nt(qg, ksb), lambda pr: _dot(pr, vsb))

    per = key_step // Q_BLOCK
    for v in range(t // key_step):
        pl.when(qblk // per == v)(functools.partial(slc_for, (v + 1) * key_step))

    w0 = pl.multiple_of(jnp.maximum(start - WINDOW, 0), Q_BLOCK)
    wlen = WINDOW + Q_BLOCK
    o_win = _win_branch(qb, qpos, Q_BLOCK, win_ref[pl.ds(w0, wlen), 0:LANES],
                        win_ref[pl.ds(w0, wlen), LANES:2 * LANES], w0)
    outs = _combine(gate_ref[...], o_cmp, oslc_ref[...], o_win, Q_BLOCK)
    for h in range(N_Q_HEADS):
        o_ref[:, h * LANES:(h + 1) * LANES] = outs[h]


def _attn_prompt(q, rows, win, kc, vc, gates, consts, b, t):
    nb = t // Q_BLOCK
    nch = t // CMP_STRIDE
    msel, eexp, n_slc = consts
    tile = lambda i, j: (i * nb + j, 0)
    batch = lambda i, j: (i, 0)
    return pl.pallas_call(
        functools.partial(_attn_prompt_kernel, t=t, n_slc=n_slc, key_step=SLC_KEY_STEP),
        out_shape=jax.ShapeDtypeStruct((b * t, Q_PAD), F32),
        grid=(b, nb),
        in_specs=[
            pl.BlockSpec((Q_BLOCK, Q_PAD), tile),
            pl.BlockSpec((t, 4 * LANES), batch),
            pl.BlockSpec((t, 2 * LANES), batch),
            pl.BlockSpec((1, nch, LANES), lambda i, j: (i, 0, 0)),
            pl.BlockSpec((1, nch, LANES), lambda i, j: (i, 0, 0)),
            pl.BlockSpec((Q_BLOCK, LANES), tile),
            pl.BlockSpec(msel.shape, lambda i, j: (0, 0)),
            pl.BlockSpec(eexp.shape, lambda i, j: (0, 0)),
        ],
        out_specs=pl.BlockSpec((Q_BLOCK, Q_PAD), tile),
        scratch_shapes=[pltpu.VMEM((N_Q_HEADS * Q_BLOCK, LANES), F32)],
        compiler_params=_cparams(("parallel", "parallel")),
        name="attn_prompt",
    )(q, rows, win, kc, vc, gates, msel, eexp)


def _attn_sample_kernel(pt_ref, q_ref, new_ref, nwin_ref, state_ref, gate_ref, cache_ref,
                        w1_ref, pe_ref, w2_ref, msel_ref, eexp_ref, o_ref, raw_ref, past_ref, kvt_ref, sem,
                        *, layer, n_pages, n_slc):
    b = pl.program_id(0)
    nb = pl.num_programs(0)
    past_len = n_pages * PAGE_SIZE
    cmp_roles, slc_roles = 0, 1

    def page_copy(seq, pg, pair):
        return pltpu.make_async_copy(
            cache_ref.at[pt_ref[seq, pg], pl.ds(layer * 4 + 2 * pair, 2)],
            raw_ref.at[pl.ds(2 * pair, 2), :, pl.ds(pg * PAGE_SIZE, PAGE_SIZE)],
            sem.at[pair, pg])

    def start_pages(seq, pair):
        for pg in range(n_pages):
            page_copy(seq, pg, pair).start()

    @pl.when(b == 0)
    def _():
        start_pages(b, cmp_roles)
        start_pages(b, slc_roles)

    def land_page(pg, carry):
        page_copy(b, pg, cmp_roles).wait()
        off = pl.multiple_of(pg * PAGE_SIZE, PAGE_SIZE)
        for role in range(2):
            past_ref[role, pl.ds(off, PAGE_SIZE), :] = raw_ref[role, :, pl.ds(off, PAGE_SIZE)].T
        return carry

    lax.fori_loop(0, n_pages, land_page, 0)

    @pl.when(b + 1 < nb)
    def _():
        start_pages(b + 1, cmp_roles)

    nch = past_len // CMP_STRIDE
    kc = _compress(past_ref.at[0], nch, w1_ref, pe_ref, w2_ref, 0)
    vc = _compress(past_ref.at[1], nch, w1_ref, pe_ref, w2_ref, 1)

    scale = HEAD_DIM ** -0.5
    qv = q_ref[0]
    q = jnp.concatenate([qv[:, h * LANES:(h + 1) * LANES] for h in range(N_Q_HEADS)], axis=0) * scale
    qb = q.astype(BF16)
    qpos = past_len + lax.broadcasted_iota(jnp.int32, (SAMPLE_T, 1), 0)
    o_cmp, sels = _cmp_and_select(qb, qpos, SAMPLE_T, kc, vc, msel_ref[...], n_slc)

    pad = jnp.zeros((LANES - SAMPLE_T, LANES), F32)
    new = new_ref[0]
    k_new = jnp.concatenate([new[:, 2 * LANES:3 * LANES], pad], axis=0).astype(BF16)
    v_new = jnp.concatenate([new[:, 3 * LANES:4 * LANES], pad], axis=0).astype(BF16)
    for pg in range(n_pages):
        page_copy(b, pg, slc_roles).wait()
    kvt_ref[0] = raw_ref[2].astype(BF16)
    kvt_ref[1] = raw_ref[3].astype(BF16)

    @pl.when(b + 1 < nb)
    def _():
        start_pages(b + 1, slc_roles)

    kt = kvt_ref[0]
    vt = kvt_ref[1]

    o_slc = _slc_branch(
        qb, qpos, SAMPLE_T, sels, eexp_ref[...], past_len + LANES,
        lambda qg: jnp.concatenate([_dot(qg, kt), _dot_nt(qg, k_new)], axis=1),
        lambda pr: _dot_nt(pr[:, 0:past_len], vt) + _dot(pr[:, past_len:], v_new))

    nw = nwin_ref[0]
    st = state_ref[0, 0]
    k_win = jnp.concatenate([st[:, 0:LANES], nw[:, 0:LANES], pad], axis=0)
    v_win = jnp.concatenate([st[:, LANES:2 * LANES], nw[:, LANES:2 * LANES], pad], axis=0)
    o_win = _win_branch(qb, qpos, SAMPLE_T, k_win, v_win, past_len - WINDOW)
    outs = _combine(gate_ref[0], o_cmp, o_slc, o_win, SAMPLE_T)
    for h in range(N_Q_HEADS):
        o_ref[0, :, h * LANES:(h + 1) * LANES] = outs[h]


def _attn_sample(page_table, q, new_rows, new_win, state_win, gates, cache, lw, consts, layer):
    bs, n_pages = page_table.shape
    msel, eexp, n_slc = consts
    past_len = n_pages * PAGE_SIZE
    per_b = lambda i, pt: (i, 0, 0)
    grid_spec = pltpu.PrefetchScalarGridSpec(
        num_scalar_prefetch=1,
        grid=(bs,),
        in_specs=[
            pl.BlockSpec((1, SAMPLE_T, Q_PAD), per_b),
            pl.BlockSpec((1, SAMPLE_T, 4 * LANES), per_b),
            pl.BlockSpec((1, SAMPLE_T, 2 * LANES), per_b),
            pl.BlockSpec((1, 1, WINDOW, 2 * LANES), lambda i, pt: (layer, i, 0, 0)),
            pl.BlockSpec((1, SAMPLE_T, LANES), per_b),
            pl.BlockSpec(memory_space=pl.ANY),
            pl.BlockSpec(lw["cmp_w1"].shape, lambda i, pt: (0,) * 5),
            pl.BlockSpec(lw["cmp_pe"].shape, lambda i, pt: (0,) * 4),
            pl.BlockSpec(lw["cmp_w2"].shape, lambda i, pt: (0,) * 3),
            pl.BlockSpec(msel.shape, lambda i, pt: (0, 0)),
            pl.BlockSpec(eexp.shape, lambda i, pt: (0, 0)),
        ],
        out_specs=pl.BlockSpec((1, SAMPLE_T, Q_PAD), per_b),
        scratch_shapes=[
            pltpu.VMEM((4, LANES, past_len), F32),
            pltpu.VMEM((2, past_len, LANES), F32),
            pltpu.VMEM((2, LANES, past_len), BF16),
            pltpu.SemaphoreType.DMA((2, n_pages)),
        ],
    )
    return pl.pallas_call(
        functools.partial(_attn_sample_kernel, layer=layer, n_pages=n_pages, n_slc=n_slc),
        out_shape=jax.ShapeDtypeStruct((bs, SAMPLE_T, Q_PAD), F32),
        grid_spec=grid_spec,
        compiler_params=_cparams(("arbitrary",)),
        name="attn_sample",
    )(page_table, q, new_rows, new_win, state_win, gates, cache,
      lw["cmp_w1"], lw["cmp_pe"], lw["cmp_w2"], msel, eexp)


def _out_kernel(oa_ref, gb_ref, u_ref, p1_ref, p2_ref, cw_ref, x_ref, gm_ref, og_ref, w_ref, n2_ref,
                x1_ref, h2_ref):
    gmat = gm_ref[...]
    y_conv = cw_ref[0:1, :] * p2_ref[...] + cw_ref[1:2, :] * p1_ref[...] + cw_ref[2:3, :] * u_ref[...]
    o_conv = gb_ref[...] * y_conv
    parts = []
    for k in range(N_Q_HEADS):
        parts.append(_group_norm(oa_ref[:, k * LANES:(k + 1) * LANES], gmat,
                                 og_ref[:, k * LANES:(k + 1) * LANES]))
    for k in range(CONV_DIM // LANES):
        parts.append(_group_norm(o_conv[:, k * LANES:(k + 1) * LANES], gmat,
                                 og_ref[:, Q_PAD + k * LANES:Q_PAD + (k + 1) * LANES]))
    mix = jnp.concatenate(parts, axis=1).astype(BF16)
    x1 = x_ref[...] + _dot(mix, w_ref[...])
    x1_ref[...] = x1
    ms = jnp.mean(x1 * x1, axis=-1, keepdims=True)
    h2_ref[...] = (x1 * lax.rsqrt(ms + RMS_EPS) * n2_ref[...]).T.astype(BF16)


def _out_proj(o_attn, gb, u, prev1, prev2, x, lw, tm):
    n = x.shape[0]
    row = lambda i: (i, 0)
    fixed = lambda i: (0, 0)
    return pl.pallas_call(
        _out_kernel,
        out_shape=(jax.ShapeDtypeStruct((n, D_MODEL), F32), jax.ShapeDtypeStruct((D_MODEL, n), BF16)),
        grid=(n // tm,),
        in_specs=[
            pl.BlockSpec((tm, Q_PAD), row),
            pl.BlockSpec((tm, CONV_DIM), row),
            pl.BlockSpec((tm, CONV_DIM), row),
            pl.BlockSpec((tm, CONV_DIM), row),
            pl.BlockSpec((tm, CONV_DIM), row),
            pl.BlockSpec((SUBLANES, CONV_DIM), fixed),
            pl.BlockSpec((tm, D_MODEL), row),
            pl.BlockSpec((LANES, LANES), fixed),
            pl.BlockSpec((1, MIX_PAD), fixed),
            pl.BlockSpec((MIX_PAD, D_MODEL), fixed),
            pl.BlockSpec((1, D_MODEL), fixed),
        ],
        out_specs=(pl.BlockSpec((tm, D_MODEL), row), pl.BlockSpec((D_MODEL, tm), lambda i: (0, i))),
        compiler_params=_cparams(("parallel",)),
        name="out_proj",
    )(o_attn, gb, u, prev1, prev2, lw["conv_w"], x, lw["gmat"], lw["out_g"], lw["w_out"], lw["norm2"])


_CAND_ROWS = 80
PEER_EB = 2048
PEER_SUB = 256
PEER_GATE_TILES = 2
ROUTE_TOKENS = 256


def _route_kernel(h_ref, wq_ref, sk_ref, a1_ref, cnt_ref, b_ref, r2_ref,
                  qt_ref, s_ref, so_ref, rank_ref, v_ref, cand_ref, cs_ref, z_ref):
    tn = h_ref.shape[1]
    k = PEER_TOPK
    qt_ref[...] = _dot(wq_ref[...], h_ref[...])

    def twice_bf16(x):
        bits = pltpu.bitcast(x.astype(BF16).astype(F32), jnp.uint32)
        return bits | (bits >> 16)
    rowid = lax.broadcasted_iota(jnp.int32, (PEER_NKEYS, tn), 0).astype(F32)

    crow = lax.broadcasted_iota(jnp.int32, (_CAND_ROWS, tn), 0)
    cr = jnp.where(crow < 16, 0, jnp.where(crow < 72, 1 + (crow - 16) // 8, crow - 64))
    cj = jnp.where(crow < 16, crow, jnp.where(crow < 72, (crow - 16) % 8, 0))
    cvalid = (cr + 1) * (cj + 1) <= k
    cflat = (cr * k + cj).astype(F32)
    rid16 = lax.broadcasted_iota(jnp.int32, (k, tn), 0).astype(F32)
    neg = -jnp.inf

    def any_miscount(selected):
        count = jnp.sum(jnp.where(selected, 1.0, 0.0), axis=0, keepdims=True)
        return jnp.max(jnp.where(count != float(k), 1.0, 0.0))

    def reset_scores():
        for p in range(2):
            s_ref[p] = so_ref[p]
            rank_ref[p] = jnp.full((PEER_NKEYS, tn), float(k), F32)

    def topk_keys(exact):
        for r in range(k):
            for p in range(2):
                s = s_ref[p]
                m = jnp.max(s, axis=0, keepdims=True)
                if exact:
                    idx = jnp.min(jnp.where(s == m, rowid, float(PEER_NKEYS)), axis=0, keepdims=True)
                    hit = rowid == idx
                else:
                    hit = s == m
                s_ref[p] = jnp.where(hit, neg, s)
                rank_ref[p] = jnp.where(hit, float(r), rank_ref[p])
                v_ref[p, r:r + 1, :] = m

    def build_candidates():
        v1 = v_ref[0]
        v2 = v_ref[1]
        cand_ref[0:16, :] = v1[0:1] + v2
        for r in range(1, 8):
            cand_ref[8 + 8 * r:16 + 8 * r, :] = v1[r:r + 1] + v2[0:8]
        cand_ref[72:80, :] = v1[8:16] + v2[0:1]
        cand_ref[...] = jnp.where(cvalid, cand_ref[...], neg)

    def topk_pairs_quick():
        cmax = v_ref[0, 0:1, :] + v_ref[1, 0:1, :]
        zsum = jnp.zeros((1, tn), F32)
        for _ in range(k):
            c = cand_ref[...]
            m = jnp.max(c, axis=0, keepdims=True)
            cand_ref[...] = jnp.where(c == m, neg, c)
            zsum = zsum + jnp.exp(m - cmax)
        taken = cvalid & (cand_ref[...] == neg)
        tk = jnp.where(taken, 1.0, 0.0)
        cs_ref[0:1, :] = jnp.sum(tk[0:16], axis=0, keepdims=True)
        for r in range(1, 8):
            cs_ref[r:r + 1, :] = jnp.sum(tk[8 + 8 * r:16 + 8 * r], axis=0, keepdims=True)
        cs_ref[8:16, :] = tk[72:80]
        z_ref[0:1, :] = zsum
        return any_miscount(taken)

    def topk_pairs_exact():
        cmax = v_ref[0, 0:1, :] + v_ref[1, 0:1, :]
        cnt = jnp.zeros((k, tn), F32)
        zsum = jnp.zeros((1, tn), F32)
        for _ in range(k):
            c = cand_ref[...]
            m = jnp.max(c, axis=0, keepdims=True)
            f = jnp.min(jnp.where(c == m, cflat, 1e9), axis=0, keepdims=True)
            cand_ref[...] = jnp.where(cflat == f, neg, c)
            cnt = cnt + jnp.where(rid16 == jnp.floor(f * (1.0 / k)), 1.0, 0.0)
            zsum = zsum + jnp.exp(m - cmax)
        cs_ref[...] = cnt
        z_ref[0:1, :] = zsum

    def head_body(h, carry):
        for p in range(2):
            base = pl.multiple_of(h * PEER_QDIM + p * (PEER_QDIM // 2), PEER_QDIM // 2)
            so_ref[p] = _dot(sk_ref[2 * h + p], qt_ref[pl.ds(base, PEER_QDIM // 2), :].astype(BF16))
        reset_scores()
        topk_keys(exact=False)
        tied = jnp.maximum(any_miscount(rank_ref[0] < float(k)), any_miscount(rank_ref[1] < float(k)))

        @pl.when(tied > 0.5)
        def _():
            reset_scores()
            topk_keys(exact=True)

        build_candidates()
        tied_pairs = topk_pairs_quick()

        @pl.when(tied_pairs > 0.5)
        def _():
            build_candidates()
            topk_pairs_exact()

        v1 = v_ref[0]
        v2 = v_ref[1]
        cnt = cs_ref[...]
        zsum = z_ref[0:1, :]
        r1 = rank_ref[0]
        r2 = rank_ref[1]
        a1_ref[h] = twice_bf16(jnp.where(r1 < float(k), jnp.exp(so_ref[0] - v1[0:1]), 0.0))
        bval = jnp.where(r2 < float(k), jnp.exp(so_ref[1] - v2[0:1]), 0.0) / zsum
        b_ref[h] = pltpu.bitcast(bval.astype(BF16), jnp.uint32)
        r2_ref[h] = pltpu.bitcast(r2.astype(BF16), jnp.uint32)
        cdense = jnp.zeros((PEER_NKEYS, tn), F32)
        for r in range(k):
            cdense = jnp.where(r1 == float(r), cnt[r:r + 1], cdense)
        cnt_ref[h] = twice_bf16(cdense)
        return carry

    lax.fori_loop(0, PEER_HEADS, head_body, 0)


def _route(h2, lw, tn):
    n = h2.shape[1]
    shape = jax.ShapeDtypeStruct((PEER_HEADS, PEER_NKEYS, n), jnp.uint32)
    shape_b = jax.ShapeDtypeStruct((PEER_HEADS, PEER_NKEYS // 2, n), jnp.uint32)
    spec = pl.BlockSpec((PEER_HEADS, PEER_NKEYS, tn), lambda i: (0, 0, i))
    spec_b = pl.BlockSpec((PEER_HEADS, PEER_NKEYS // 2, tn), lambda i: (0, 0, i))
    return pl.pallas_call(
        _route_kernel,
        out_shape=(shape, shape, shape_b, shape_b),
        grid=(n // tn,),
        in_specs=[
            pl.BlockSpec((D_MODEL, tn), lambda i: (0, i)),
            pl.BlockSpec((PEER_HEADS * PEER_QDIM, D_MODEL), lambda i: (0, 0)),
            pl.BlockSpec((2 * PEER_HEADS, PEER_NKEYS, PEER_QDIM // 2), lambda i: (0, 0, 0)),
        ],
        out_specs=(spec, spec, spec_b, spec_b),
        scratch_shapes=[
            pltpu.VMEM((PEER_HEADS * PEER_QDIM, tn), F32),
            pltpu.VMEM((2, PEER_NKEYS, tn), F32),
            pltpu.VMEM((2, PEER_NKEYS, tn), F32),
            pltpu.VMEM((2, PEER_NKEYS, tn), F32),
            pltpu.VMEM((2, PEER_TOPK, tn), F32),
            pltpu.VMEM((_CAND_ROWS, tn), F32),
            pltpu.VMEM((PEER_TOPK, tn), F32),
            pltpu.VMEM((SUBLANES, tn), F32),
        ],
        compiler_params=_cparams(("parallel",)),
        name="peer_route",
    )(h2, lw["peer_wq_t"], lw["peer_sk"])


def _peer_kernel(h_ref, u_ref, vt_ref, a1_ref, cnt_ref, b_ref, r2_ref, x_ref, y_ref,
                 acc_ref, at_ref, *, eb, sub):
    e = pl.program_id(1)
    tn = h_ref.shape[1]
    pack = 2 * SUBLANES

    @pl.when(e == 0)
    def _():
        acc_ref[...] = jnp.zeros_like(acc_ref)

    h = h_ref[...]
    zero = jnp.zeros((pack, LANES), BF16)

    def row_pair(ref, hh, ai, cols):
        return pltpu.bitcast(jnp.broadcast_to(ref[hh, ai:ai + 1, cols], (SUBLANES, LANES)), BF16)

    def sub_block(sb):
        r0 = sb * sub
        st = _dot(u_ref[r0:r0 + sub, :], h)
        n_a = PEER_GATE_TILES
        n_k = PEER_NKEYS // pack
        for a0 in range(0, sub // PEER_NKEYS, n_a):
            ai0 = r0 // PEER_NKEYS + a0
            for c in range(tn // LANES):
                cols = slice(c * LANES, (c + 1) * LANES)
                w = [[None] * n_k for _ in range(n_a)]
                for hh in range(PEER_HEADS):
                    cnt = [row_pair(cnt_ref, hh, ai0 + a, cols) for a in range(n_a)]
                    a1 = [row_pair(a1_ref, hh, ai0 + a, cols) for a in range(n_a)]
                    for k in range(n_k):
                        rk = slice(k * SUBLANES, (k + 1) * SUBLANES)
                        r2 = pltpu.bitcast(r2_ref[hh, rk, cols], BF16)
                        bb = pltpu.bitcast(b_ref[hh, rk, cols], BF16)
                        for a in range(n_a):
                            t = a1[a] * jnp.where(r2 < cnt[a], bb, zero)
                            w[a][k] = t if w[a][k] is None else w[a][k] + t
                for a in range(n_a):
                    lo = (a0 + a) * PEER_NKEYS
                    g = jax.nn.gelu(st[lo:lo + PEER_NKEYS, cols]).astype(BF16)
                    at_ref[r0 + lo:r0 + lo + PEER_NKEYS, cols] = jnp.concatenate(w[a], axis=0) * g

    n_sb = eb // sub
    half = eb // 2
    tot = None
    for sb in range(n_sb):
        sub_block(sb)
        if (sb + 1) * sub % half == 0:
            k0 = (sb + 1) * sub - half
            d = _dot(vt_ref[:, k0:k0 + half], at_ref[k0:k0 + half, :])
            tot = d if tot is None else tot + d
    acc_ref[...] += tot

    @pl.when(e == pl.num_programs(1) - 1)
    def _():
        y_ref[...] = x_ref[...] + acc_ref[...].T


def _peer(h2, routed, x1, lw, tn, eb):
    n = h2.shape[1]
    n_exp = PEER_NKEYS * PEER_NKEYS
    a1, cnt, bb, r2 = routed
    tok = lambda i, e: (i, 0)
    key_rows = pl.BlockSpec((PEER_HEADS, eb // PEER_NKEYS, tn), lambda i, e: (0, e, i))
    key_full = pl.BlockSpec((PEER_HEADS, PEER_NKEYS // 2, tn), lambda i, e: (0, 0, i))
    return pl.pallas_call(
        functools.partial(_peer_kernel, eb=eb, sub=PEER_SUB),
        out_shape=jax.ShapeDtypeStruct((n, D_MODEL), F32),
        grid=(n // tn, n_exp // eb),
        in_specs=[
            pl.BlockSpec((D_MODEL, tn), lambda i, e: (0, i)),
            pl.BlockSpec((eb, D_MODEL), lambda i, e: (e, 0)),
            pl.BlockSpec((D_MODEL, eb), lambda i, e: (0, e)),
            key_rows, key_rows, key_full, key_full,
            pl.BlockSpec((tn, D_MODEL), tok),
        ],
        out_specs=pl.BlockSpec((tn, D_MODEL), tok),
        scratch_shapes=[pltpu.VMEM((D_MODEL, tn), F32), pltpu.VMEM((eb, tn), BF16)],
        compiler_params=_cparams(("parallel", "arbitrary")),
        name="peer_experts",
    )(h2, lw["peer_u"], lw["peer_vt"], a1, cnt, bb, r2, x1)


def _rope_tables(pos):
    inv_freq = ROPE_THETA ** (-jnp.arange(ROPE_HALF, dtype=F32) / ROPE_HALF)
    ang = pos.astype(F32)[:, None] * inv_freq[None, :]
    cos, sin = jnp.cos(ang), jnp.sin(ang)
    n = pos.shape[0]
    ones = jnp.ones((n, HEAD_DIM - ROPE_DIM), F32)
    zeros = jnp.zeros((n, HEAD_DIM - ROPE_DIM), F32)
    zh = jnp.zeros((n, ROPE_HALF), F32)
    c = jnp.concatenate([cos, cos, ones], axis=1)
    s_up = jnp.concatenate([-sin, zh, zeros], axis=1)
    s_dn = jnp.concatenate([zh, sin, zeros], axis=1)
    return tuple(jnp.tile(a, (1, 2)) for a in (c, s_up, s_dn))


def _attn_consts(n_cmp, nch, n_slc, slc_len):
    jn = -(-n_slc // LANES) * LANES
    sr = SLC_BLOCK // CMP_STRIDE
    r = CMP_LEN // CMP_STRIDE
    msel = np.zeros((nch, jn), np.float32)
    for j in range(n_slc):
        for m in range(sr):
            for n in range(r):
                i = sr * j + m - n
                if 0 <= i < n_cmp:
                    msel[i, j] += 1.0
    eexp = np.zeros((jn, slc_len), np.float32)
    keys = np.arange(slc_len)
    eexp[keys // SLC_BLOCK, keys] = 1.0
    return jnp.asarray(msel, BF16), jnp.asarray(eexp, BF16), n_slc


def _pad_heads(x, axis):
    shape = x.shape
    x = jnp.moveaxis(x, axis, -1).reshape(shape[:axis] + shape[axis + 1:] + (N_KV_HEADS, HEADS_PER_GROUP, HEAD_DIM))
    nd = x.ndim
    lo = jnp.pad(x[..., 0, :, :], [(0, 0)] * (nd - 2) + [(0, HEAD_DIM)])
    hi = jnp.pad(x[..., 1, :, :], [(0, 0)] * (nd - 2) + [(HEAD_DIM, 0)])
    out = jnp.concatenate([lo, hi], axis=-2)
    out = out.reshape(out.shape[:-2] + (Q_PAD,))
    return jnp.moveaxis(out, -1, axis)


def _prep_weights(norm1_g, w_in, q_norm_g, k_norm_g, cmp_pe, cmp_w1, cmp_w2, conv_w, out_norm_g,
                  w_out, norm2_g, peer_wq, peer_subkeys, peer_u, peer_v):
    depth = w_in.shape[0]
    c1 = NSA_WIDTH
    c2 = c1 + 6 * KV_WIDTH
    c3 = c2 + N_Q_HEADS * N_BRANCH
    gl_pad = jnp.zeros((depth, D_MODEL, LANES - N_Q_HEADS * N_BRANCH), F32)
    w_all = jnp.concatenate([_pad_heads(w_in[:, :, :c1], 2), w_in[:, :, c1:c2], w_in[:, :, c3:],
                             w_in[:, :, c2:c3], gl_pad], axis=2)

    blk = np.kron(np.eye(2, dtype=np.float32), np.full((HEAD_DIM, HEAD_DIM), 1.0 / HEAD_DIM, np.float32))
    kg = jnp.pad(jnp.tile(k_norm_g, (1, 1, 2)), ((0, 0), (0, SUBLANES - N_BRANCH), (0, 0)))

    r = CMP_LEN // CMP_STRIDE
    eye2 = jnp.eye(2, dtype=F32)
    w1 = cmp_w1.reshape(depth, 2, r, CMP_STRIDE, HEAD_DIM, HEAD_DIM)
    w1 = jnp.einsum('lxmsde,gk->lxmsgdke', w1, eye2).reshape(depth, 2, r, CMP_STRIDE // 2, 2 * LANES, LANES)
    pe = jnp.tile(cmp_pe.reshape(depth, 2, r, CMP_STRIDE, HEAD_DIM), (1, 1, 1, 1, 2))
    w2 = jnp.einsum('lxde,gk->lxgdke', cmp_w2, eye2).reshape(depth, 2, LANES, LANES)

    og_pad = jnp.concatenate([_pad_heads(out_norm_g[:, :NSA_WIDTH], 1), out_norm_g[:, NSA_WIDTH:]], axis=1)
    wo_pad = jnp.concatenate([_pad_heads(w_out[:, :NSA_WIDTH], 1), w_out[:, NSA_WIDTH:]], axis=1)
    return {
        "norm1": norm1_g[:, None, :],
        "w_in": w_all.astype(BF16),
        "gmat": jnp.broadcast_to(jnp.asarray(blk, BF16), (depth, LANES, LANES)),
        "q_g": jnp.tile(q_norm_g, (1, 2))[:, None, :],
        "k_g": kg,
        "cmp_w1": w1.astype(BF16),
        "cmp_pe": pe,
        "cmp_w2": w2.astype(BF16),
        "conv_w": jnp.pad(conv_w, ((0, 0), (0, SUBLANES - CONV_WIDTH), (0, 0))),
        "out_g": og_pad[:, None, :],
        "w_out": wo_pad.astype(BF16),
        "norm2": norm2_g[:, None, :],
        "peer_wq_t": jnp.swapaxes(peer_wq, 1, 2).astype(BF16),
        "peer_sk": peer_subkeys.reshape(depth, 2 * PEER_HEADS, PEER_NKEYS, PEER_QDIM // 2).astype(BF16),
        "peer_u": peer_u.astype(BF16),
        "peer_vt": jnp.swapaxes(peer_v, 1, 2).astype(BF16),
    }


def _conv_prev(u, prefix, b, t):
    ext = jnp.concatenate([prefix, u.reshape(b, t, CONV_DIM)], axis=1)
    return ext[:, 1:t + 1].reshape(b * t, CONV_DIM), ext[:, 0:t].reshape(b * t, CONV_DIM)


def _row_tile(n):
    for tm in (256, 128):
        if n % tm == 0:
            return tm
    raise ValueError(f"token count {n} must be a multiple of 128")


def _peer_tile(n):
    for tn in (512, 256, 128):
        if n % tn == 0:
            return tn
    raise ValueError(f"token count {n} must be a multiple of 128")


def kernel(x_prompt, x_sample, cache_nsa_kv, state_win_kv, state_conv, page_table,
           norm1_g, w_in, q_norm_g, k_norm_g, cmp_pe, cmp_w1, cmp_w2, conv_w, out_norm_g, w_out,
           norm2_g, peer_wq, peer_subkeys, peer_u, peer_v):
    bp, t, _ = x_prompt.shape
    bs, ts, _ = x_sample.shape
    depth = norm1_g.shape[0]
    n_pages = page_table.shape[1]
    past_len = n_pages * PAGE_SIZE
    assert t % SLC_KEY_STEP == 0 and t >= WINDOW + Q_BLOCK and ts <= SAMPLE_T
    assert state_win_kv.shape[2] == WINDOW and past_len % SLC_BLOCK == 0

    xp = x_prompt.reshape(bp * t, D_MODEL)
    xs = jnp.pad(x_sample, ((0, 0), (0, SAMPLE_T - ts), (0, 0))).reshape(bs * SAMPLE_T, D_MODEL)
    cache = jnp.transpose(cache_nsa_kv, (0, 2, 3, 4, 5, 1)).reshape(
        cache_nsa_kv.shape[0], depth * 4, LANES, PAGE_SIZE)
    state_win = state_win_kv.reshape(depth, bs, WINDOW, 2 * LANES)

    tabs_p = _rope_tables(jnp.arange(t))
    tabs_s = _rope_tables(jnp.tile(past_len + jnp.arange(SAMPLE_T), bs))
    n_cmp_p = (t - CMP_LEN) // CMP_STRIDE + 1
    consts_p = _attn_consts(n_cmp_p, t // CMP_STRIDE, t // SLC_BLOCK, t)
    t_all = past_len + ts
    n_cmp_s = (t_all - CMP_LEN) // CMP_STRIDE + 1
    assert n_cmp_s <= past_len // CMP_STRIDE
    consts_s = _attn_consts(n_cmp_s, past_len // CMP_STRIDE, -(-t_all // SLC_BLOCK), past_len + LANES)
    zero_prefix = jnp.zeros((bp, CONV_WIDTH - 1, CONV_DIM), F32)

    weights = _prep_weights(norm1_g, w_in, q_norm_g, k_norm_g, cmp_pe, cmp_w1, cmp_w2, conv_w, out_norm_g,
                            w_out, norm2_g, peer_wq, peer_subkeys, peer_u, peer_v)
    rows_p, rows_s, win_p, win_s, conv_p, conv_s = [], [], [], [], [], []
    for l in range(depth):
        lw = {name: stacked[l] for name, stacked in weights.items()}
        q, rows, win, gates, gb, u = _in_proj(xp, lw, tabs_p, _row_tile(bp * t))
        kc, vc = _compress_prompt(rows, lw, bp, t)
        o_attn = _attn_prompt(q, rows, win, kc, vc, gates, consts_p, bp, t)
        p1, p2 = _conv_prev(u, zero_prefix, bp, t)
        x1, h2 = _out_proj(o_attn, gb, u, p1, p2, xp, lw, _row_tile(bp * t))
        routed = _route(h2, lw, ROUTE_TOKENS if h2.shape[1] % ROUTE_TOKENS == 0 else LANES)
        xp = _peer(h2, routed, x1, lw, _peer_tile(bp * t), PEER_EB)
        rows_p.append(rows.reshape(bp, t, 4, N_KV_HEADS, HEAD_DIM))
        win_p.append(win.reshape(bp, t, 2, N_KV_HEADS, HEAD_DIM)[:, t - WINDOW:])
        conv_p.append(u.reshape(bp, t, CONV_DIM)[:, t - (CONV_WIDTH - 1):])

        q, rows, win, gates, gb, u = _in_proj(xs, lw, tabs_s, _row_tile(bs * SAMPLE_T))
        o_attn = _attn_sample(
            page_table, q.reshape(bs, SAMPLE_T, Q_PAD), rows.reshape(bs, SAMPLE_T, 4 * LANES),
            win.reshape(bs, SAMPLE_T, 2 * LANES), state_win, gates.reshape(bs, SAMPLE_T, LANES),
            cache, lw, consts_s, l).reshape(bs * SAMPLE_T, Q_PAD)
        p1, p2 = _conv_prev(u, state_conv[l], bs, SAMPLE_T)
        x1, h2 = _out_proj(o_attn, gb, u, p1, p2, xs, lw, _row_tile(bs * SAMPLE_T))
        routed = _route(h2, lw, ROUTE_TOKENS if h2.shape[1] % ROUTE_TOKENS == 0 else LANES)
        xs = _peer(h2, routed, x1, lw, _peer_tile(bs * SAMPLE_T), PEER_EB)
        rows_s.append(rows.reshape(bs, SAMPLE_T, 4, N_KV_HEADS, HEAD_DIM)[:, :ts])
        new_win = win.reshape(bs, SAMPLE_T, 2, N_KV_HEADS, HEAD_DIM)[:, :ts]
        old_win = state_win_kv[l]
        win_s.append(jnp.concatenate([old_win, new_win], axis=1)[:, ts:])
        conv_s.append(u.reshape(bs, SAMPLE_T, CONV_DIM)[:, ts - (CONV_WIDTH - 1):ts])

    y_prompt = xp.reshape(bp, t, D_MODEL)
    y_sample = xs.reshape(bs, SAMPLE_T, D_MODEL)[:, :ts]
    return (y_prompt, y_sample,
            jnp.stack(rows_p, axis=2), jnp.stack(rows_s, axis=2),
            jnp.stack(win_p, axis=0), jnp.stack(win_s, axis=0),
            jnp.stack(conv_p, axis=0), jnp.stack(conv_s, axis=0))
```

```python
import functools
import math

import numpy as np
import jax
import jax.numpy as jnp
from jax import lax
from jax.experimental import pallas as pl
from jax.experimental.pallas import tpu as pltpu

F32 = jnp.float32
BF16 = jnp.bfloat16

LANES = 128
SUBLANES = 8
VMEM_LIMIT = 56 * 1024 * 1024

D_MODEL = 1024
HEAD_DIM = 64
N_Q_HEADS = 8
N_KV_HEADS = 2
HEADS_PER_GROUP = N_Q_HEADS // N_KV_HEADS
NSA_WIDTH = N_Q_HEADS * HEAD_DIM
KV_WIDTH = N_KV_HEADS * HEAD_DIM
CONV_DIM = D_MODEL - NSA_WIDTH
CONV_WIDTH = 3
N_BRANCH = 3
ROPE_DIM = HEAD_DIM // 4
ROPE_HALF = ROPE_DIM // 2
ROPE_THETA = 500000.0
CMP_LEN = 32
CMP_STRIDE = 16
SLC_BLOCK = 64
N_SELECT = 16
WINDOW = 512
Q_BLOCK = 128
PAGE_SIZE = 128
PEER_HEADS = 8
PEER_QDIM = 256
PEER_NKEYS = 128
PEER_TOPK = 16
RMS_EPS = 1e-6
NEG_INF = -1e30
FORCE_BONUS = 1e4

Q_PAD = N_Q_HEADS * LANES
IN_COLS = Q_PAD + 6 * KV_WIDTH + 3 * CONV_DIM + LANES
MIX_PAD = Q_PAD + CONV_DIM
SAMPLE_T = 8
SLC_KEY_STEP = 256

_NT = (((1,), (1,)), ((), ()))


def _cparams(sem):
    return pltpu.CompilerParams(dimension_semantics=sem, vmem_limit_bytes=VMEM_LIMIT)


def _dot(a, b):
    return jnp.dot(a, b, preferred_element_type=F32)


def _dot_nt(a, b):
    return lax.dot_general(a, b, _NT, preferred_element_type=F32)


def _group_norm(x, gmat, gain):
    ss = _dot((x * x).astype(BF16), gmat)
    return x * lax.rsqrt(ss + RMS_EPS) * gain


def _in_kernel(x_ref, g1_ref, w_ref, gm_ref, qg_ref, kg_ref, c_ref, s1_ref, s2_ref,
               q_ref, rows_ref, win_ref, gate_ref, gb_ref, u_ref):
    x = x_ref[...]
    ms = jnp.mean(x * x, axis=-1, keepdims=True)
    h = (x * lax.rsqrt(ms + RMS_EPS) * g1_ref[...]).astype(BF16)
    z = _dot(h, w_ref[...])
    gmat = gm_ref[...]
    cos = c_ref[...]
    sin_up = s1_ref[...]
    sin_dn = s2_ref[...]

    def norm_rope(zb, gain):
        y = _group_norm(zb, gmat, gain)
        return (y * cos + pltpu.roll(y, LANES - ROPE_HALF, 1) * sin_up
                + pltpu.roll(y, ROPE_HALF, 1) * sin_dn)

    for hb in range(N_Q_HEADS):
        q_ref[:, hb * LANES:(hb + 1) * LANES] = norm_rope(z[:, hb * LANES:(hb + 1) * LANES], qg_ref[...])
    o = Q_PAD
    kv = [z[:, o + r * LANES:o + (r + 1) * LANES] for r in range(6)]
    rows_ref[:, 0 * LANES:1 * LANES] = norm_rope(kv[0], kg_ref[0:1, :])
    rows_ref[:, 1 * LANES:2 * LANES] = kv[1]
    rows_ref[:, 2 * LANES:3 * LANES] = norm_rope(kv[2], kg_ref[1:2, :])
    rows_ref[:, 3 * LANES:4 * LANES] = kv[3]
    win_ref[:, 0:LANES] = norm_rope(kv[4], kg_ref[2:3, :])
    win_ref[:, LANES:2 * LANES] = kv[5]
    o += 6 * LANES
    gb_ref[...] = z[:, o:o + CONV_DIM]
    u_ref[...] = z[:, o + CONV_DIM:o + 2 * CONV_DIM] * z[:, o + 2 * CONV_DIM:o + 3 * CONV_DIM]
    o += 3 * CONV_DIM
    gate_ref[...] = jax.nn.sigmoid(z[:, o:o + LANES])


def _in_proj(x, lw, tabs, tm):
    n = x.shape[0]
    cos, sin_up, sin_dn = tabs
    npos_tiles = cos.shape[0] // tm
    row = lambda i: (i, 0)
    fixed = lambda i: (0, 0)
    pos = lambda i: (i % npos_tiles, 0)
    out_shape = (
        jax.ShapeDtypeStruct((n, Q_PAD), F32),
        jax.ShapeDtypeStruct((n, 4 * LANES), F32),
        jax.ShapeDtypeStruct((n, 2 * LANES), F32),
        jax.ShapeDtypeStruct((n, LANES), F32),
        jax.ShapeDtypeStruct((n, CONV_DIM), F32),
        jax.ShapeDtypeStruct((n, CONV_DIM), F32),
    )
    return pl.pallas_call(
        _in_kernel,
        out_shape=out_shape,
        grid=(n // tm,),
        in_specs=[
            pl.BlockSpec((tm, D_MODEL), row),
            pl.BlockSpec((1, D_MODEL), fixed),
            pl.BlockSpec((D_MODEL, IN_COLS), fixed),
            pl.BlockSpec((LANES, LANES), fixed),
            pl.BlockSpec((1, LANES), fixed),
            pl.BlockSpec((SUBLANES, LANES), fixed),
            pl.BlockSpec((tm, LANES), pos),
            pl.BlockSpec((tm, LANES), pos),
            pl.BlockSpec((tm, LANES), pos),
        ],
        out_specs=(
            pl.BlockSpec((tm, Q_PAD), row),
            pl.BlockSpec((tm, 4 * LANES), row),
            pl.BlockSpec((tm, 2 * LANES), row),
            pl.BlockSpec((tm, LANES), row),
            pl.BlockSpec((tm, CONV_DIM), row),
            pl.BlockSpec((tm, CONV_DIM), row),
        ),
        compiler_params=_cparams(("parallel",)),
        name="in_proj",
    )(x, lw["norm1"], lw["w_in"], lw["gmat"], lw["q_g"], lw["k_g"], cos, sin_up, sin_dn)


def _compress(src_ref, nch, w1_ref, pe_ref, w2_ref, role):
    acc0 = jnp.zeros((nch, LANES), F32)
    acc1 = jnp.zeros((nch, LANES), F32)
    for s2 in range(CMP_STRIDE // 2):
        xs = []
        for s in (2 * s2, 2 * s2 + 1):
            xs.append(src_ref[pl.ds(s, nch, stride=CMP_STRIDE), :])
        x0 = jnp.concatenate([xs[0] + pe_ref[role, 0, 2 * s2:2 * s2 + 1, :],
                              xs[1] + pe_ref[role, 0, 2 * s2 + 1:2 * s2 + 2, :]], axis=1)
        x1 = jnp.concatenate([xs[0] + pe_ref[role, 1, 2 * s2:2 * s2 + 1, :],
                              xs[1] + pe_ref[role, 1, 2 * s2 + 1:2 * s2 + 2, :]], axis=1)
        acc0 = acc0 + _dot(x0.astype(BF16), w1_ref[role, 0, s2])
        acc1 = acc1 + _dot(x1.astype(BF16), w1_ref[role, 1, s2])
    hid = acc0 + pltpu.roll(acc1, nch - 1, 0)
    return _dot(jax.nn.gelu(hid).astype(BF16), w2_ref[role])


def _cmp_kernel(k_ref, v_ref, w1_ref, pe_ref, w2_ref, kc_ref, vc_ref, *, nch):
    kc_ref[0] = _compress(k_ref, nch, w1_ref, pe_ref, w2_ref, 0)
    vc_ref[0] = _compress(v_ref, nch, w1_ref, pe_ref, w2_ref, 1)


def _compress_prompt(rows, lw, b, t):
    nch = t // CMP_STRIDE
    fixed = lambda i: (0,) * 5
    return pl.pallas_call(
        functools.partial(_cmp_kernel, nch=nch),
        out_shape=(jax.ShapeDtypeStruct((b, nch, LANES), F32),) * 2,
        grid=(b,),
        in_specs=[
            pl.BlockSpec((t, LANES), lambda i: (i, 0)),
            pl.BlockSpec((t, LANES), lambda i: (i, 1)),
            pl.BlockSpec(lw["cmp_w1"].shape, fixed),
            pl.BlockSpec(lw["cmp_pe"].shape, lambda i: (0,) * 4),
            pl.BlockSpec(lw["cmp_w2"].shape, lambda i: (0,) * 3),
        ],
        out_specs=(pl.BlockSpec((1, nch, LANES), lambda i: (i, 0, 0)),) * 2,
        compiler_params=_cparams(("parallel",)),
        name="compress_prompt",
    )(rows, rows, lw["cmp_w1"], lw["cmp_pe"], lw["cmp_w2"])


def _masked_softmax(s, mask):
    s = jnp.where(mask, s, NEG_INF)
    p = jnp.where(mask, jnp.exp(s - jnp.max(s, axis=-1, keepdims=True)), 0.0)
    return p / jnp.maximum(jnp.sum(p, axis=-1, keepdims=True), 1e-30)


def _split3(x):
    h1 = x.astype(BF16)
    r1 = x - h1.astype(F32)
    h2 = r1.astype(BF16)
    h3 = (r1 - h2.astype(F32)).astype(BF16)
    return h1, h2, h3


def _cmp_and_select(qb, qpos, nq, kc, vc, msel, n_slc):
    nrow = N_Q_HEADS * nq
    grow = HEADS_PER_GROUP * nq
    qpos_rows = jnp.concatenate([qpos] * N_Q_HEADS, axis=0)
    nch = kc.shape[0]
    s = _dot_nt(qb, kc.astype(BF16))
    cend = lax.broadcasted_iota(jnp.int32, (nrow, nch), 1) * CMP_STRIDE + (CMP_LEN - 1)
    p = _masked_softmax(s, cend <= qpos_rows)
    o_cmp = _dot(p.astype(BF16), vc.astype(BF16))

    jn = msel.shape[1]
    jid = lax.broadcasted_iota(jnp.int32, (nq, jn), 1)
    cur = qpos // SLC_BLOCK
    valid = jid * SLC_BLOCK <= qpos
    forced = (jid == 0) | (jid == cur) | (jid == cur - 1)
    sels = []
    for g in range(N_KV_HEADS):
        pg = p[g * grow:g * grow + nq]
        for hh in range(1, HEADS_PER_GROUP):
            pg = pg + p[g * grow + hh * nq:g * grow + (hh + 1) * nq]
        p_slc = sum(_dot(part, msel) for part in _split3(pg))
        score = jnp.where(valid, p_slc + jnp.where(forced, FORCE_BONUS, 0.0), NEG_INF)
        score = jnp.where(jid < n_slc, score, -3e38)
        rank = jnp.zeros((nq, jn), F32)
        for i in range(n_slc):
            col = score[:, i:i + 1]
            tie = jnp.where(jid > i, 1.0, 0.0)
            rank = rank + jnp.where(col > score, 1.0, jnp.where(col == score, tie, 0.0))
        sels.append(jnp.where(rank < float(min(N_SELECT, n_slc)), 1.0, 0.0))
    return o_cmp, sels


def _softmax_pv(s, bias, values_fn):
    s = s + bias
    e = jnp.exp(s - jnp.max(s, axis=-1, keepdims=True))
    return values_fn(e.astype(BF16)) / jnp.sum(e, axis=-1, keepdims=True)


def _slc_branch(qb, qpos, nq, sels, eexp, klen, scores_fn, values_fn):
    grow = HEADS_PER_GROUP * nq
    causal = lax.broadcasted_iota(jnp.int32, (nq, klen), 1) <= qpos
    chosen = _dot(jnp.concatenate(sels, axis=0).astype(BF16), eexp)
    outs = []
    for g in range(N_KV_HEADS):
        bias_g = jnp.where(causal, jnp.where(chosen[g * nq:(g + 1) * nq] > 0.5, 0.0, NEG_INF), NEG_INF)
        bias = jnp.concatenate([bias_g] * HEADS_PER_GROUP, axis=0)
        s = scores_fn(qb[g * grow:(g + 1) * grow])
        outs.append(_softmax_pv(s, bias, values_fn))
    return jnp.concatenate(outs, axis=0)


def _win_branch(qb, qpos, nq, k_win, v_win, wpos0):
    wpos = wpos0 + lax.broadcasted_iota(jnp.int32, (nq, k_win.shape[0]), 1)
    bias_q = jnp.where(wpos <= qpos, jnp.where(wpos > qpos - WINDOW, 0.0, NEG_INF), NEG_INF)
    bias = jnp.concatenate([bias_q] * N_Q_HEADS, axis=0)
    vb = v_win.astype(BF16)
    return _softmax_pv(_dot_nt(qb, k_win.astype(BF16)), bias, lambda pr: _dot(pr, vb))


def _combine(gates, o_cmp, o_slc, o_win, nq):
    lane = lax.broadcasted_iota(jnp.int32, (nq, LANES), 1)
    outs = []
    for h in range(N_Q_HEADS):
        r0 = h * nq
        o = (gates[:, 3 * h:3 * h + 1] * o_cmp[r0:r0 + nq]
             + gates[:, 3 * h + 1:3 * h + 2] * o_slc[r0:r0 + nq]
             + gates[:, 3 * h + 2:3 * h + 3] * o_win[r0:r0 + nq])
        g = h // HEADS_PER_GROUP
        outs.append(jnp.where((lane >= g * HEAD_DIM) & (lane < (g + 1) * HEAD_DIM), o, 0.0))
    return outs


def _attn_prompt_kernel(q_ref, rows_ref, win_ref, kc_ref, vc_ref, gate_ref, msel_ref, eexp_ref,
                        o_ref, oslc_ref, *, t, n_slc, key_step):
    qblk = pl.program_id(1)
    start = qblk * Q_BLOCK
    scale = HEAD_DIM ** -0.5
    q = jnp.concatenate([q_ref[:, h * LANES:(h + 1) * LANES] for h in range(N_Q_HEADS)], axis=0) * scale
    qb = q.astype(BF16)
    qpos = start + lax.broadcasted_iota(jnp.int32, (Q_BLOCK, 1), 0)
    o_cmp, sels = _cmp_and_select(qb, qpos, Q_BLOCK, kc_ref[0], vc_ref[0], msel_ref[...], n_slc)

    def slc_for(klen):
        ksb = rows_ref[0:klen, 2 * LANES:3 * LANES].astype(BF16)
        vsb = rows_ref[0:klen, 3 * LANES:4 * LANES].astype(BF16)
        oslc_ref[...] = _slc_branch(qb, qpos, Q_BLOCK, sels, eexp_ref[:, 0:klen], klen,
                                    lambda qg: _dot_nt(qg, ksb), lambda pr: _dot(pr, vsb))

    per = key_step // Q_BLOCK
    for v in range(t // key_step):
        pl.when(qblk // per == v)(functools.partial(slc_for, (v + 1) * key_step))

    w0 = pl.multiple_of(jnp.maximum(start - WINDOW, 0), Q_BLOCK)
    wlen = WINDOW + Q_BLOCK
    o_win = _win_branch(qb, qpos, Q_BLOCK, win_ref[pl.ds(w0, wlen), 0:LANES],
                        win_ref[pl.ds(w0, wlen), LANES:2 * LANES], w0)
    outs = _combine(gate_ref[...], o_cmp, oslc_ref[...], o_win, Q_BLOCK)
    for h in range(N_Q_HEADS):
        o_ref[:, h * LANES:(h + 1) * LANES] = outs[h]


def _attn_prompt(q, rows, win, kc, vc, gates, consts, b, t):
    nb = t // Q_BLOCK
    nch = t // CMP_STRIDE
    msel, eexp, n_slc = consts
    tile = lambda i, j: (i * nb + j, 0)
    batch = lambda i, j: (i, 0)
    return pl.pallas_call(
        functools.partial(_attn_prompt_kernel, t=t, n_slc=n_slc, key_step=SLC_KEY_STEP),
        out_shape=jax.ShapeDtypeStruct((b * t, Q_PAD), F32),
        grid=(b, nb),
        in_specs=[
            pl.BlockSpec((Q_BLOCK, Q_PAD), tile),
            pl.BlockSpec((t, 4 * LANES), batch),
            pl.BlockSpec((t, 2 * LANES), batch),
            pl.BlockSpec((1, nch, LANES), lambda i, j: (i, 0, 0)),
            pl.BlockSpec((1, nch, LANES), lambda i, j: (i, 0, 0)),
            pl.BlockSpec((Q_BLOCK, LANES), tile),
            pl.BlockSpec(msel.shape, lambda i, j: (0, 0)),
            pl.BlockSpec(eexp.shape, lambda i, j: (0, 0)),
        ],
        out_specs=pl.BlockSpec((Q_BLOCK, Q_PAD), tile),
        scratch_shapes=[pltpu.VMEM((N_Q_HEADS * Q_BLOCK, LANES), F32)],
        compiler_params=_cparams(("parallel", "parallel")),
        name="attn_prompt",
    )(q, rows, win, kc, vc, gates, msel, eexp)


def _attn_sample_kernel(pt_ref, q_ref, new_ref, nwin_ref, state_ref, gate_ref, cache_ref,
                        w1_ref, pe_ref, w2_ref, msel_ref, eexp_ref, o_ref, raw_ref, past_ref, kvt_ref, sem,
                        *, layer, n_pages, n_slc):
    b = pl.program_id(0)
    nb = pl.num_programs(0)
    past_len = n_pages * PAGE_SIZE
    cmp_roles, slc_roles = 0, 1

    def page_copy(seq, pg, pair):
        return pltpu.make_async_copy(
            cache_ref.at[pt_ref[seq, pg], pl.ds(layer * 4 + 2 * pair, 2)],
            raw_ref.at[pl.ds(2 * pair, 2), :, pl.ds(pg * PAGE_SIZE, PAGE_SIZE)],
            sem.at[pair, pg])

    def start_pages(seq, pair):
        for pg in range(n_pages):
            page_copy(seq, pg, pair).start()

    @pl.when(b == 0)
    def _():
        start_pages(b, cmp_roles)
        start_pages(b, slc_roles)

    def land_page(pg, carry):
        page_copy(b, pg, cmp_roles).wait()
        off = pl.multiple_of(pg * PAGE_SIZE, PAGE_SIZE)
        for role in range(2):
            past_ref[role, pl.ds(off, PAGE_SIZE), :] = raw_ref[role, :, pl.ds(off, PAGE_SIZE)].T
        return carry

    lax.fori_loop(0, n_pages, land_page, 0)

    @pl.when(b + 1 < nb)
    def _():
        start_pages(b + 1, cmp_roles)

    nch = past_len // CMP_STRIDE
    kc = _compress(past_ref.at[0], nch, w1_ref, pe_ref, w2_ref, 0)
    vc = _compress(past_ref.at[1], nch, w1_ref, pe_ref, w2_ref, 1)

    scale = HEAD_DIM ** -0.5
    qv = q_ref[0]
    q = jnp.concatenate([qv[:, h * LANES:(h + 1) * LANES] for h in range(N_Q_HEADS)], axis=0) * scale
    qb = q.astype(BF16)
    qpos = past_len + lax.broadcasted_iota(jnp.int32, (SAMPLE_T, 1), 0)
    o_cmp, sels = _cmp_and_select(qb, qpos, SAMPLE_T, kc, vc, msel_ref[...], n_slc)

    pad = jnp.zeros((LANES - SAMPLE_T, LANES), F32)
    new = new_ref[0]
    k_new = jnp.concatenate([new[:, 2 * LANES:3 * LANES], pad], axis=0).astype(BF16)
    v_new = jnp.concatenate([new[:, 3 * LANES:4 * LANES], pad], axis=0).astype(BF16)
    for pg in range(n_pages):
        page_copy(b, pg, slc_roles).wait()
    kvt_ref[0] = raw_ref[2].astype(BF16)
    kvt_ref[1] = raw_ref[3].astype(BF16)

    @pl.when(b + 1 < nb)
    def _():
        start_pages(b + 1, slc_roles)

    kt = kvt_ref[0]
    vt = kvt_ref[1]

    o_slc = _slc_branch(
        qb, qpos, SAMPLE_T, sels, eexp_ref[...], past_len + LANES,
        lambda qg: jnp.concatenate([_dot(qg, kt), _dot_nt(qg, k_new)], axis=1),
        lambda pr: _dot_nt(pr[:, 0:past_len], vt) + _dot(pr[:, past_len:], v_new))

    nw = nwin_ref[0]
    st = state_ref[0, 0]
    k_win = jnp.concatenate([st[:, 0:LANES], nw[:, 0:LANES], pad], axis=0)
    v_win = jnp.concatenate([st[:, LANES:2 * LANES], nw[:, LANES:2 * LANES], pad], axis=0)
    o_win = _win_branch(qb, qpos, SAMPLE_T, k_win, v_win, past_len - WINDOW)
    outs = _combine(gate_ref[0], o_cmp, o_slc, o_win, SAMPLE_T)
    for h in range(N_Q_HEADS):
        o_ref[0, :, h * LANES:(h + 1) * LANES] = outs[h]


def _attn_sample(page_table, q, new_rows, new_win, state_win, gates, cache, lw, consts, layer):
    bs, n_pages = page_table.shape
    msel, eexp, n_slc = consts
    past_len = n_pages * PAGE_SIZE
    per_b = lambda i, pt: (i, 0, 0)
    grid_spec = pltpu.PrefetchScalarGridSpec(
        num_scalar_prefetch=1,
        grid=(bs,),
        in_specs=[
            pl.BlockSpec((1, SAMPLE_T, Q_PAD), per_b),
            pl.BlockSpec((1, SAMPLE_T, 4 * LANES), per_b),
            pl.BlockSpec((1, SAMPLE_T, 2 * LANES), per_b),
            pl.BlockSpec((1, 1, WINDOW, 2 * LANES), lambda i, pt: (layer, i, 0, 0)),
            pl.BlockSpec((1, SAMPLE_T, LANES), per_b),
            pl.BlockSpec(memory_space=pl.ANY),
            pl.BlockSpec(lw["cmp_w1"].shape, lambda i, pt: (0,) * 5),
            pl.BlockSpec(lw["cmp_pe"].shape, lambda i, pt: (0,) * 4),
            pl.BlockSpec(lw["cmp_w2"].shape, lambda i, pt: (0,) * 3),
            pl.BlockSpec(msel.shape, lambda i, pt: (0, 0)),
            pl.BlockSpec(eexp.shape, lambda i, pt: (0, 0)),
        ],
        out_specs=pl.BlockSpec((1, SAMPLE_T, Q_PAD), per_b),
        scratch_shapes=[
            pltpu.VMEM((4, LANES, past_len), F32),
            pltpu.VMEM((2, past_len, LANES), F32),
            pltpu.VMEM((2, LANES, past_len), BF16),
            pltpu.SemaphoreType.DMA((2, n_pages)),
        ],
    )
    return pl.pallas_call(
        functools.partial(_attn_sample_kernel, layer=layer, n_pages=n_pages, n_slc=n_slc),
        out_shape=jax.ShapeDtypeStruct((bs, SAMPLE_T, Q_PAD), F32),
        grid_spec=grid_spec,
        compiler_params=_cparams(("arbitrary",)),
        name="attn_sample",
    )(page_table, q, new_rows, new_win, state_win, gates, cache,
      lw["cmp_w1"], lw["cmp_pe"], lw["cmp_w2"], msel, eexp)


def _out_kernel(oa_ref, gb_ref, u_ref, p1_ref, p2_ref, cw_ref, x_ref, gm_ref, og_ref, w_ref, n2_ref,
                x1_ref, h2_ref):
    gmat = gm_ref[...]
    y_conv = cw_ref[0:1, :] * p2_ref[...] + cw_ref[1:2, :] * p1_ref[...] + cw_ref[2:3, :] * u_ref[...]
    o_conv = gb_ref[...] * y_conv
    parts = []
    for k in range(N_Q_HEADS):
        parts.append(_group_norm(oa_ref[:, k * LANES:(k + 1) * LANES], gmat,
                                 og_ref[:, k * LANES:(k + 1) * LANES]))
    for k in range(CONV_DIM // LANES):
        parts.append(_group_norm(o_conv[:, k * LANES:(k + 1) * LANES], gmat,
                                 og_ref[:, Q_PAD + k * LANES:Q_PAD + (k + 1) * LANES]))
    mix = jnp.concatenate(parts, axis=1).astype(BF16)
    x1 = x_ref[...] + _dot(mix, w_ref[...])
    x1_ref[...] = x1
    ms = jnp.mean(x1 * x1, axis=-1, keepdims=True)
    h2_ref[...] = (x1 * lax.rsqrt(ms + RMS_EPS) * n2_ref[...]).T.astype(BF16)


def _out_proj(o_attn, gb, u, prev1, prev2, x, lw, tm):
    n = x.shape[0]
    row = lambda i: (i, 0)
    fixed = lambda i: (0, 0)
    return pl.pallas_call(
        _out_kernel,
        out_shape=(jax.ShapeDtypeStruct((n, D_MODEL), F32), jax.ShapeDtypeStruct((D_MODEL, n), BF16)),
        grid=(n // tm,),
        in_specs=[
            pl.BlockSpec((tm, Q_PAD), row),
            pl.BlockSpec((tm, CONV_DIM), row),
            pl.BlockSpec((tm, CONV_DIM), row),
            pl.BlockSpec((tm, CONV_DIM), row),
            pl.BlockSpec((tm, CONV_DIM), row),
            pl.BlockSpec((SUBLANES, CONV_DIM), fixed),
            pl.BlockSpec((tm, D_MODEL), row),
            pl.BlockSpec((LANES, LANES), fixed),
            pl.BlockSpec((1, MIX_PAD), fixed),
            pl.BlockSpec((MIX_PAD, D_MODEL), fixed),
            pl.BlockSpec((1, D_MODEL), fixed),
        ],
        out_specs=(pl.BlockSpec((tm, D_MODEL), row), pl.BlockSpec((D_MODEL, tm), lambda i: (0, i))),
        compiler_params=_cparams(("parallel",)),
        name="out_proj",
    )(o_attn, gb, u, prev1, prev2, lw["conv_w"], x, lw["gmat"], lw["out_g"], lw["w_out"], lw["norm2"])


_CAND_ROWS = 80
PEER_EB = 2048
PEER_SUB = 256
PEER_GATE_TILES = 2
ROUTE_TOKENS = 256


def _route_kernel(h_ref, wq_ref, sk_ref, a1_ref, cnt_ref, b_ref, r2_ref,
                  qt_ref, s_ref, so_ref, rank_ref, v_ref, cand_ref, cs_ref, z_ref):
    tn = h_ref.shape[1]
    k = PEER_TOPK
    qt_ref[...] = _dot(wq_ref[...], h_ref[...])

    def twice_bf16(x):
        bits = pltpu.bitcast(x.astype(BF16).astype(F32), jnp.uint32)
        return bits | (bits >> 16)
    rowid = lax.broadcasted_iota(jnp.int32, (PEER_NKEYS, tn), 0).astype(F32)

    crow = lax.broadcasted_iota(jnp.int32, (_CAND_ROWS, tn), 0)
    cr = jnp.where(crow < 16, 0, jnp.where(crow < 72, 1 + (crow - 16) // 8, crow - 64))
    cj = jnp.where(crow < 16, crow, jnp.where(crow < 72, (crow - 16) % 8, 0))
    cvalid = (cr + 1) * (cj + 1) <= k
    cflat = (cr * k + cj).astype(F32)
    rid16 = lax.broadcasted_iota(jnp.int32, (k, tn), 0).astype(F32)
    neg = -jnp.inf

    def any_miscount(selected):
        count = jnp.sum(jnp.where(selected, 1.0, 0.0), axis=0, keepdims=True)
        return jnp.max(jnp.where(count != float(k), 1.0, 0.0))

    def reset_scores():
        for p in range(2):
            s_ref[p] = so_ref[p]
            rank_ref[p] = jnp.full((PEER_NKEYS, tn), float(k), F32)

    def topk_keys(exact):
        for r in range(k):
            for p in range(2):
                s = s_ref[p]
                m = jnp.max(s, axis=0, keepdims=True)
                if exact:
                    idx = jnp.min(jnp.where(s == m, rowid, float(PEER_NKEYS)), axis=0, keepdims=True)
                    hit = rowid == idx
                else:
                    hit = s == m
                s_ref[p] = jnp.where(hit, neg, s)
                rank_ref[p] = jnp.where(hit, float(r), rank_ref[p])
                v_ref[p, r:r + 1, :] = m

    def build_candidates():
        v1 = v_ref[0]
        v2 = v_ref[1]
        cand_ref[0:16, :] = v1[0:1] + v2
        for r in range(1, 8):
            cand_ref[8 + 8 * r:16 + 8 * r, :] = v1[r:r + 1] + v2[0:8]
        cand_ref[72:80, :] = v1[8:16] + v2[0:1]
        cand_ref[...] = jnp.where(cvalid, cand_ref[...], neg)

    def topk_pairs_quick():
        cmax = v_ref[0, 0:1, :] + v_ref[1, 0:1, :]
        zsum = jnp.zeros((1, tn), F32)
        for _ in range(k):
            c = cand_ref[...]
            m = jnp.max(c, axis=0, keepdims=True)
            cand_ref[...] = jnp.where(c == m, neg, c)
            zsum = zsum + jnp.exp(m - cmax)
        taken = cvalid & (cand_ref[...] == neg)
        tk = jnp.where(taken, 1.0, 0.0)
        cs_ref[0:1, :] = jnp.sum(tk[0:16], axis=0, keepdims=True)
        for r in range(1, 8):
            cs_ref[r:r + 1, :] = jnp.sum(tk[8 + 8 * r:16 + 8 * r], axis=0, keepdims=True)
        cs_ref[8:16, :] = tk[72:80]
        z_ref[0:1, :] = zsum
        return any_miscount(taken)

    def topk_pairs_exact():
        cmax = v_ref[0, 0:1, :] + v_ref[1, 0:1, :]
        cnt = jnp.zeros((k, tn), F32)
        zsum = jnp.zeros((1, tn), F32)
        for _ in range(k):
            c = cand_ref[...]
            m = jnp.max(c, axis=0, keepdims=True)
            f = jnp.min(jnp.where(c == m, cflat, 1e9), axis=0, keepdims=True)
            cand_ref[...] = jnp.where(cflat == f, neg, c)
            cnt = cnt + jnp.where(rid16 == jnp.floor(f * (1.0 / k)), 1.0, 0.0)
            zsum = zsum + jnp.exp(m - cmax)
        cs_ref[...] = cnt
        z_ref[0:1, :] = zsum

    def head_body(h, carry):
        for p in range(2):
            base = pl.multiple_of(h * PEER_QDIM + p * (PEER_QDIM // 2), PEER_QDIM // 2)
            so_ref[p] = _dot(sk_ref[2 * h + p], qt_ref[pl.ds(base, PEER_QDIM // 2), :].astype(BF16))
        reset_scores()
        topk_keys(exact=False)
        tied = jnp.maximum(any_miscount(rank_ref[0] < float(k)), any_miscount(rank_ref[1] < float(k)))

        @pl.when(tied > 0.5)
        def _():
            reset_scores()
            topk_keys(exact=True)

        build_candidates()
        tied_pairs = topk_pairs_quick()

        @pl.when(tied_pairs > 0.5)
        def _():
            build_candidates()
            topk_pairs_exact()

        v1 = v_ref[0]
        v2 = v_ref[1]
        cnt = cs_ref[...]
        zsum = z_ref[0:1, :]
        r1 = rank_ref[0]
        r2 = rank_ref[1]
        a1_ref[h] = twice_bf16(jnp.where(r1 < float(k), jnp.exp(so_ref[0] - v1[0:1]), 0.0))
        bval = jnp.where(r2 < float(k), jnp.exp(so_ref[1] - v2[0:1]), 0.0) / zsum
        b_ref[h] = pltpu.bitcast(bval.astype(BF16), jnp.uint32)
        r2_ref[h] = pltpu.bitcast(r2.astype(BF16), jnp.uint32)
        cdense = jnp.zeros((PEER_NKEYS, tn), F32)
        for r in range(k):
            cdense = jnp.where(r1 == float(r), cnt[r:r + 1], cdense)
        cnt_ref[h] = twice_bf16(cdense)
        return carry

    lax.fori_loop(0, PEER_HEADS, head_body, 0)


def _route(h2, lw, tn):
    n = h2.shape[1]
    shape = jax.ShapeDtypeStruct((PEER_HEADS, PEER_NKEYS, n), jnp.uint32)
    shape_b = jax.ShapeDtypeStruct((PEER_HEADS, PEER_NKEYS // 2, n), jnp.uint32)
    spec = pl.BlockSpec((PEER_HEADS, PEER_NKEYS, tn), lambda i: (0, 0, i))
    spec_b = pl.BlockSpec((PEER_HEADS, PEER_NKEYS // 2, tn), lambda i: (0, 0, i))
    return pl.pallas_call(
        _route_kernel,
        out_shape=(shape, shape, shape_b, shape_b),
        grid=(n // tn,),
        in_specs=[
            pl.BlockSpec((D_MODEL, tn), lambda i: (0, i)),
            pl.BlockSpec((PEER_HEADS * PEER_QDIM, D_MODEL), lambda i: (0, 0)),
            pl.BlockSpec((2 * PEER_HEADS, PEER_NKEYS, PEER_QDIM // 2), lambda i: (0, 0, 0)),
        ],
        out_specs=(spec, spec, spec_b, spec_b),
        scratch_shapes=[
            pltpu.VMEM((PEER_HEADS * PEER_QDIM, tn), F32),
            pltpu.VMEM((2, PEER_NKEYS, tn), F32),
            pltpu.VMEM((2, PEER_NKEYS, tn), F32),
            pltpu.VMEM((2, PEER_NKEYS, tn), F32),
            pltpu.VMEM((2, PEER_TOPK, tn), F32),
            pltpu.VMEM((_CAND_ROWS, tn), F32),
            pltpu.VMEM((PEER_TOPK, tn), F32),
            pltpu.VMEM((SUBLANES, tn), F32),
        ],
        compiler_params=_cparams(("parallel",)),
        name="peer_route",
    )(h2, lw["peer_wq_t"], lw["peer_sk"])


def _gelu_tanh(x):
    c = math.sqrt(2.0 / math.pi)
    hx = 0.5 * x
    return hx + hx * jnp.tanh(x * (c + (c * 0.044715) * (x * x)))


def _peer_kernel(h_ref, u_ref, vt_ref, a1_ref, cnt_ref, b_ref, r2_ref, x_ref, y_ref,
                 acc_ref, at_ref, *, eb, sub):
    e = pl.program_id(1)
    tn = h_ref.shape[1]
    pack = 2 * SUBLANES

    @pl.when(e == 0)
    def _():
        acc_ref[...] = jnp.zeros_like(acc_ref)

    h = h_ref[...]
    zero = jnp.zeros((pack, LANES), BF16)

    def row_pair(ref, hh, ai, cols):
        return pltpu.bitcast(jnp.broadcast_to(ref[hh, ai:ai + 1, cols], (SUBLANES, LANES)), BF16)

    def sub_block(sb):
        r0 = sb * sub
        st = _dot(u_ref[r0:r0 + sub, :], h)
        n_a = PEER_GATE_TILES
        n_k = PEER_NKEYS // pack
        for a0 in range(0, sub // PEER_NKEYS, n_a):
            ai0 = r0 // PEER_NKEYS + a0
            for c in range(tn // LANES):
                cols = slice(c * LANES, (c + 1) * LANES)
                w = [[None] * n_k for _ in range(n_a)]
                for hh in range(PEER_HEADS):
                    cnt = [row_pair(cnt_ref, hh, ai0 + a, cols) for a in range(n_a)]
                    a1 = [row_pair(a1_ref, hh, ai0 + a, cols) for a in range(n_a)]
                    for k in range(n_k):
                        rk = slice(k * SUBLANES, (k + 1) * SUBLANES)
                        r2 = pltpu.bitcast(r2_ref[hh, rk, cols], BF16)
                        bb = pltpu.bitcast(b_ref[hh, rk, cols], BF16)
                        for a in range(n_a):
                            t = a1[a] * jnp.where(r2 < cnt[a], bb, zero)
                            w[a][k] = t if w[a][k] is None else w[a][k] + t
                for a in range(n_a):
                    lo = (a0 + a) * PEER_NKEYS
                    g = _gelu_tanh(st[lo:lo + PEER_NKEYS, cols].astype(BF16))
                    at_ref[r0 + lo:r0 + lo + PEER_NKEYS, cols] = jnp.concatenate(w[a], axis=0) * g

    n_sb = eb // sub
    half = eb // 2
    tot = None
    for sb in range(n_sb):
        sub_block(sb)
        if (sb + 1) * sub % half == 0:
            k0 = (sb + 1) * sub - half
            d = lax.dot_general(vt_ref[k0:k0 + half, :], at_ref[k0:k0 + half, :],
                                (((0,), (0,)), ((), ())), preferred_element_type=F32)
            tot = d if tot is None else tot + d
    acc_ref[...] += tot

    @pl.when(e == pl.num_programs(1) - 1)
    def _():
        y_ref[...] = x_ref[...] + acc_ref[...].T


def _peer(h2, routed, x1, lw, tn, eb):
    n = h2.shape[1]
    n_exp = PEER_NKEYS * PEER_NKEYS
    a1, cnt, bb, r2 = routed
    tok = lambda i, e: (i, 0)
    key_rows = pl.BlockSpec((PEER_HEADS, eb // PEER_NKEYS, tn), lambda i, e: (0, e, i))
    key_full = pl.BlockSpec((PEER_HEADS, PEER_NKEYS // 2, tn), lambda i, e: (0, 0, i))
    return pl.pallas_call(
        functools.partial(_peer_kernel, eb=eb, sub=PEER_SUB),
        out_shape=jax.ShapeDtypeStruct((n, D_MODEL), F32),
        grid=(n // tn, n_exp // eb),
        in_specs=[
            pl.BlockSpec((D_MODEL, tn), lambda i, e: (0, i)),
            pl.BlockSpec((eb, D_MODEL), lambda i, e: (e, 0)),
            pl.BlockSpec((eb, D_MODEL), lambda i, e: (e, 0)),
            key_rows, key_rows, key_full, key_full,
            pl.BlockSpec((tn, D_MODEL), tok),
        ],
        out_specs=pl.BlockSpec((tn, D_MODEL), tok),
        scratch_shapes=[pltpu.VMEM((D_MODEL, tn), F32), pltpu.VMEM((eb, tn), BF16)],
        compiler_params=_cparams(("parallel", "arbitrary")),
        name="peer_experts",
    )(h2, lw["peer_u"], lw["peer_vt"], a1, cnt, bb, r2, x1)


def _rope_tables(pos):
    inv_freq = ROPE_THETA ** (-jnp.arange(ROPE_HALF, dtype=F32) / ROPE_HALF)
    ang = pos.astype(F32)[:, None] * inv_freq[None, :]
    cos, sin = jnp.cos(ang), jnp.sin(ang)
    n = pos.shape[0]
    ones = jnp.ones((n, HEAD_DIM - ROPE_DIM), F32)
    zeros = jnp.zeros((n, HEAD_DIM - ROPE_DIM), F32)
    zh = jnp.zeros((n, ROPE_HALF), F32)
    c = jnp.concatenate([cos, cos, ones], axis=1)
    s_up = jnp.concatenate([-sin, zh, zeros], axis=1)
    s_dn = jnp.concatenate([zh, sin, zeros], axis=1)
    return tuple(jnp.tile(a, (1, 2)) for a in (c, s_up, s_dn))


def _attn_consts(n_cmp, nch, n_slc, slc_len):
    jn = -(-n_slc // LANES) * LANES
    sr = SLC_BLOCK // CMP_STRIDE
    r = CMP_LEN // CMP_STRIDE
    msel = np.zeros((nch, jn), np.float32)
    for j in range(n_slc):
        for m in range(sr):
            for n in range(r):
                i = sr * j + m - n
                if 0 <= i < n_cmp:
                    msel[i, j] += 1.0
    eexp = np.zeros((jn, slc_len), np.float32)
    keys = np.arange(slc_len)
    eexp[keys // SLC_BLOCK, keys] = 1.0
    return jnp.asarray(msel, BF16), jnp.asarray(eexp, BF16), n_slc


def _pad_heads(x, axis):
    shape = x.shape
    x = jnp.moveaxis(x, axis, -1).reshape(shape[:axis] + shape[axis + 1:] + (N_KV_HEADS, HEADS_PER_GROUP, HEAD_DIM))
    nd = x.ndim
    lo = jnp.pad(x[..., 0, :, :], [(0, 0)] * (nd - 2) + [(0, HEAD_DIM)])
    hi = jnp.pad(x[..., 1, :, :], [(0, 0)] * (nd - 2) + [(HEAD_DIM, 0)])
    out = jnp.concatenate([lo, hi], axis=-2)
    out = out.reshape(out.shape[:-2] + (Q_PAD,))
    return jnp.moveaxis(out, -1, axis)


def _prep_weights(norm1_g, w_in, q_norm_g, k_norm_g, cmp_pe, cmp_w1, cmp_w2, conv_w, out_norm_g,
                  w_out, norm2_g, peer_wq, peer_subkeys, peer_u, peer_v):
    depth = w_in.shape[0]
    c1 = NSA_WIDTH
    c2 = c1 + 6 * KV_WIDTH
    c3 = c2 + N_Q_HEADS * N_BRANCH
    gl_pad = jnp.zeros((depth, D_MODEL, LANES - N_Q_HEADS * N_BRANCH), F32)
    w_all = jnp.concatenate([_pad_heads(w_in[:, :, :c1], 2), w_in[:, :, c1:c2], w_in[:, :, c3:],
                             w_in[:, :, c2:c3], gl_pad], axis=2)

    blk = np.kron(np.eye(2, dtype=np.float32), np.full((HEAD_DIM, HEAD_DIM), 1.0 / HEAD_DIM, np.float32))
    kg = jnp.pad(jnp.tile(k_norm_g, (1, 1, 2)), ((0, 0), (0, SUBLANES - N_BRANCH), (0, 0)))

    r = CMP_LEN // CMP_STRIDE
    eye2 = jnp.eye(2, dtype=F32)
    w1 = cmp_w1.reshape(depth, 2, r, CMP_STRIDE, HEAD_DIM, HEAD_DIM)
    w1 = jnp.einsum('lxmsde,gk->lxmsgdke', w1, eye2).reshape(depth, 2, r, CMP_STRIDE // 2, 2 * LANES, LANES)
    pe = jnp.tile(cmp_pe.reshape(depth, 2, r, CMP_STRIDE, HEAD_DIM), (1, 1, 1, 1, 2))
    w2 = jnp.einsum('lxde,gk->lxgdke', cmp_w2, eye2).reshape(depth, 2, LANES, LANES)

    og_pad = jnp.concatenate([_pad_heads(out_norm_g[:, :NSA_WIDTH], 1), out_norm_g[:, NSA_WIDTH:]], axis=1)
    wo_pad = jnp.concatenate([_pad_heads(w_out[:, :NSA_WIDTH], 1), w_out[:, NSA_WIDTH:]], axis=1)
    return {
        "norm1": norm1_g[:, None, :],
        "w_in": w_all.astype(BF16),
        "gmat": jnp.broadcast_to(jnp.asarray(blk, BF16), (depth, LANES, LANES)),
        "q_g": jnp.tile(q_norm_g, (1, 2))[:, None, :],
        "k_g": kg,
        "cmp_w1": w1.astype(BF16),
        "cmp_pe": pe,
        "cmp_w2": w2.astype(BF16),
        "conv_w": jnp.pad(conv_w, ((0, 0), (0, SUBLANES - CONV_WIDTH), (0, 0))),
        "out_g": og_pad[:, None, :],
        "w_out": wo_pad.astype(BF16),
        "norm2": norm2_g[:, None, :],
        "peer_wq_t": jnp.swapaxes(peer_wq, 1, 2).astype(BF16),
        "peer_sk": peer_subkeys.reshape(depth, 2 * PEER_HEADS, PEER_NKEYS, PEER_QDIM // 2).astype(BF16),
        "peer_u": peer_u.astype(BF16),
        "peer_vt": peer_v.astype(BF16),
    }


def _conv_prev(u, prefix, b, t):
    ext = jnp.concatenate([prefix, u.reshape(b, t, CONV_DIM)], axis=1)
    return ext[:, 1:t + 1].reshape(b * t, CONV_DIM), ext[:, 0:t].reshape(b * t, CONV_DIM)


def _row_tile(n):
    for tm in (256, 128):
        if n % tm == 0:
            return tm
    raise ValueError(f"token count {n} must be a multiple of 128")


def _peer_tile(n):
    for tn in (512, 256, 128):
        if n % tn == 0:
            return tn
    raise ValueError(f"token count {n} must be a multiple of 128")


def kernel(x_prompt, x_sample, cache_nsa_kv, state_win_kv, state_conv, page_table,
           norm1_g, w_in, q_norm_g, k_norm_g, cmp_pe, cmp_w1, cmp_w2, conv_w, out_norm_g, w_out,
           norm2_g, peer_wq, peer_subkeys, peer_u, peer_v):
    bp, t, _ = x_prompt.shape
    bs, ts, _ = x_sample.shape
    depth = norm1_g.shape[0]
    n_pages = page_table.shape[1]
    past_len = n_pages * PAGE_SIZE
    assert t % SLC_KEY_STEP == 0 and t >= WINDOW + Q_BLOCK and ts <= SAMPLE_T
    assert state_win_kv.shape[2] == WINDOW and past_len % SLC_BLOCK == 0

    xp = x_prompt.reshape(bp * t, D_MODEL)
    xs = jnp.pad(x_sample, ((0, 0), (0, SAMPLE_T - ts), (0, 0))).reshape(bs * SAMPLE_T, D_MODEL)
    cache = jnp.transpose(cache_nsa_kv, (0, 2, 3, 4, 5, 1)).reshape(
        cache_nsa_kv.shape[0], depth * 4, LANES, PAGE_SIZE)
    state_win = state_win_kv.reshape(depth, bs, WINDOW, 2 * LANES)

    tabs_p = _rope_tables(jnp.arange(t))
    tabs_s = _rope_tables(jnp.tile(past_len + jnp.arange(SAMPLE_T), bs))
    n_cmp_p = (t - CMP_LEN) // CMP_STRIDE + 1
    consts_p = _attn_consts(n_cmp_p, t // CMP_STRIDE, t // SLC_BLOCK, t)
    t_all = past_len + ts
    n_cmp_s = (t_all - CMP_LEN) // CMP_STRIDE + 1
    assert n_cmp_s <= past_len // CMP_STRIDE
    consts_s = _attn_consts(n_cmp_s, past_len // CMP_STRIDE, -(-t_all // SLC_BLOCK), past_len + LANES)
    zero_prefix = jnp.zeros((bp, CONV_WIDTH - 1, CONV_DIM), F32)

    weights = _prep_weights(norm1_g, w_in, q_norm_g, k_norm_g, cmp_pe, cmp_w1, cmp_w2, conv_w, out_norm_g,
                            w_out, norm2_g, peer_wq, peer_subkeys, peer_u, peer_v)
    rows_p, rows_s, win_p, win_s, conv_p, conv_s = [], [], [], [], [], []
    for l in range(depth):
        lw = {name: stacked[l] for name, stacked in weights.items()}
        q, rows, win, gates, gb, u = _in_proj(xp, lw, tabs_p, _row_tile(bp * t))
        kc, vc = _compress_prompt(rows, lw, bp, t)
        o_attn = _attn_prompt(q, rows, win, kc, vc, gates, consts_p, bp, t)
        p1, p2 = _conv_prev(u, zero_prefix, bp, t)
        x1, h2 = _out_proj(o_attn, gb, u, p1, p2, xp, lw, _row_tile(bp * t))
        routed = _route(h2, lw, ROUTE_TOKENS if h2.shape[1] % ROUTE_TOKENS == 0 else LANES)
        xp = _peer(h2, routed, x1, lw, _peer_tile(bp * t), PEER_EB)
        rows_p.append(rows.reshape(bp, t, 4, N_KV_HEADS, HEAD_DIM))
        win_p.append(win.reshape(bp, t, 2, N_KV_HEADS, HEAD_DIM)[:, t - WINDOW:])
        conv_p.append(u.reshape(bp, t, CONV_DIM)[:, t - (CONV_WIDTH - 1):])

        q, rows, win, gates, gb, u = _in_proj(xs, lw, tabs_s, _row_tile(bs * SAMPLE_T))
        o_attn = _attn_sample(
            page_table, q.reshape(bs, SAMPLE_T, Q_PAD), rows.reshape(bs, SAMPLE_T, 4 * LANES),
            win.reshape(bs, SAMPLE_T, 2 * LANES), state_win, gates.reshape(bs, SAMPLE_T, LANES),
            cache, lw, consts_s, l).reshape(bs * SAMPLE_T, Q_PAD)
        p1, p2 = _conv_prev(u, state_conv[l], bs, SAMPLE_T)
        x1, h2 = _out_proj(o_attn, gb, u, p1, p2, xs, lw, _row_tile(bs * SAMPLE_T))
        routed = _route(h2, lw, ROUTE_TOKENS if h2.shape[1] % ROUTE_TOKENS == 0 else LANES)
        xs = _peer(h2, routed, x1, lw, _peer_tile(bs * SAMPLE_T), PEER_EB)
        rows_s.append(rows.reshape(bs, SAMPLE_T, 4, N_KV_HEADS, HEAD_DIM)[:, :ts])
        new_win = win.reshape(bs, SAMPLE_T, 2, N_KV_HEADS, HEAD_DIM)[:, :ts]
        old_win = state_win_kv[l]
        win_s.append(jnp.concatenate([old_win, new_win], axis=1)[:, ts:])
        conv_s.append(u.reshape(bs, SAMPLE_T, CONV_DIM)[:, ts - (CONV_WIDTH - 1):ts])

    y_prompt = xp.reshape(bp, t, D_MODEL)
    y_sample = xs.reshape(bs, SAMPLE_T, D_MODEL)[:, :ts]
    return (y_prompt, y_sample,
            jnp.stack(rows_p, axis=2), jnp.stack(rows_s, axis=2),
            jnp.stack(win_p, axis=0), jnp.stack(win_s, axis=0),
            jnp.stack(conv_p, axis=0), jnp.stack(conv_s, axis=0))
```

```python
import functools
import math

import numpy as np
import jax
import jax.numpy as jnp
from jax import lax
from jax.experimental import pallas as pl
from jax.experimental.pallas import tpu as pltpu

F32 = jnp.float32
BF16 = jnp.bfloat16

LANES = 128
SUBLANES = 8
VMEM_LIMIT = 56 * 1024 * 1024

D_MODEL = 1024
HEAD_DIM = 64
N_Q_HEADS = 8
N_KV_HEADS = 2
HEADS_PER_GROUP = N_Q_HEADS // N_KV_HEADS
NSA_WIDTH = N_Q_HEADS * HEAD_DIM
KV_WIDTH = N_KV_HEADS * HEAD_DIM
CONV_DIM = D_MODEL - NSA_WIDTH
CONV_WIDTH = 3
N_BRANCH = 3
ROPE_DIM = HEAD_DIM // 4
ROPE_HALF = ROPE_DIM // 2
ROPE_THETA = 500000.0
CMP_LEN = 32
CMP_STRIDE = 16
SLC_BLOCK = 64
N_SELECT = 16
WINDOW = 512
Q_BLOCK = 128
PAGE_SIZE = 128
PEER_HEADS = 8
PEER_QDIM = 256
PEER_NKEYS = 128
PEER_TOPK = 16
RMS_EPS = 1e-6
NEG_INF = -1e30
FORCE_BONUS = 1e4

Q_PAD = N_Q_HEADS * LANES
IN_COLS = Q_PAD + 6 * KV_WIDTH + 3 * CONV_DIM + LANES
MIX_PAD = Q_PAD + CONV_DIM
SAMPLE_T = 8
SLC_KEY_STEP = 256

_NT = (((1,), (1,)), ((), ()))


def _cparams(sem):
    return pltpu.CompilerParams(dimension_semantics=sem, vmem_limit_bytes=VMEM_LIMIT)


def _dot(a, b):
    return jnp.dot(a, b, preferred_element_type=F32)


def _dot_nt(a, b):
    return lax.dot_general(a, b, _NT, preferred_element_type=F32)


def _group_norm(x, gmat, gain):
    ss = _dot((x * x).astype(BF16), gmat)
    return x * lax.rsqrt(ss + RMS_EPS) * gain


def _in_kernel(x_ref, g1_ref, w_ref, gm_ref, qg_ref, kg_ref, c_ref, s1_ref, s2_ref,
               q_ref, rows_ref, win_ref, gate_ref, gb_ref, u_ref):
    x = x_ref[...]
    ms = jnp.mean(x * x, axis=-1, keepdims=True)
    h = (x * lax.rsqrt(ms + RMS_EPS) * g1_ref[...]).astype(BF16)
    z = _dot(h, w_ref[...])
    gmat = gm_ref[...]
    cos = c_ref[...]
    sin_up = s1_ref[...]
    sin_dn = s2_ref[...]

    def norm_rope(zb, gain):
        y = _group_norm(zb, gmat, gain)
        return (y * cos + pltpu.roll(y, LANES - ROPE_HALF, 1) * sin_up
                + pltpu.roll(y, ROPE_HALF, 1) * sin_dn)

    for hb in range(N_Q_HEADS):
        q_ref[:, hb * LANES:(hb + 1) * LANES] = norm_rope(z[:, hb * LANES:(hb + 1) * LANES], qg_ref[...])
    o = Q_PAD
    kv = [z[:, o + r * LANES:o + (r + 1) * LANES] for r in range(6)]
    rows_ref[:, 0 * LANES:1 * LANES] = norm_rope(kv[0], kg_ref[0:1, :])
    rows_ref[:, 1 * LANES:2 * LANES] = kv[1]
    rows_ref[:, 2 * LANES:3 * LANES] = norm_rope(kv[2], kg_ref[1:2, :])
    rows_ref[:, 3 * LANES:4 * LANES] = kv[3]
    win_ref[:, 0:LANES] = norm_rope(kv[4], kg_ref[2:3, :])
    win_ref[:, LANES:2 * LANES] = kv[5]
    o += 6 * LANES
    gb_ref[...] = z[:, o:o + CONV_DIM]
    u_ref[...] = z[:, o + CONV_DIM:o + 2 * CONV_DIM] * z[:, o + 2 * CONV_DIM:o + 3 * CONV_DIM]
    o += 3 * CONV_DIM
    gate_ref[...] = jax.nn.sigmoid(z[:, o:o + LANES])


def _in_proj(x, lw, tabs, tm):
    n = x.shape[0]
    cos, sin_up, sin_dn = tabs
    npos_tiles = cos.shape[0] // tm
    row = lambda i: (i, 0)
    fixed = lambda i: (0, 0)
    pos = lambda i: (i % npos_tiles, 0)
    out_shape = (
        jax.ShapeDtypeStruct((n, Q_PAD), F32),
        jax.ShapeDtypeStruct((n, 4 * LANES), F32),
        jax.ShapeDtypeStruct((n, 2 * LANES), F32),
        jax.ShapeDtypeStruct((n, LANES), F32),
        jax.ShapeDtypeStruct((n, CONV_DIM), F32),
        jax.ShapeDtypeStruct((n, CONV_DIM), F32),
    )
    return pl.pallas_call(
        _in_kernel,
        out_shape=out_shape,
        grid=(n // tm,),
        in_specs=[
            pl.BlockSpec((tm, D_MODEL), row),
            pl.BlockSpec((1, D_MODEL), fixed),
            pl.BlockSpec((D_MODEL, IN_COLS), fixed),
            pl.BlockSpec((LANES, LANES), fixed),
            pl.BlockSpec((1, LANES), fixed),
            pl.BlockSpec((SUBLANES, LANES), fixed),
            pl.BlockSpec((tm, LANES), pos),
            pl.BlockSpec((tm, LANES), pos),
            pl.BlockSpec((tm, LANES), pos),
        ],
        out_specs=(
            pl.BlockSpec((tm, Q_PAD), row),
            pl.BlockSpec((tm, 4 * LANES), row),
            pl.BlockSpec((tm, 2 * LANES), row),
            pl.BlockSpec((tm, LANES), row),
            pl.BlockSpec((tm, CONV_DIM), row),
            pl.BlockSpec((tm, CONV_DIM), row),
        ),
        compiler_params=_cparams(("parallel",)),
        name="in_proj",
    )(x, lw["norm1"], lw["w_in"], lw["gmat"], lw["q_g"], lw["k_g"], cos, sin_up, sin_dn)


def _compress(src_ref, nch, w1_ref, pe_ref, w2_ref, role):
    acc0 = jnp.zeros((nch, LANES), F32)
    acc1 = jnp.zeros((nch, LANES), F32)
    for s2 in range(CMP_STRIDE // 2):
        xs = []
        for s in (2 * s2, 2 * s2 + 1):
            xs.append(src_ref[pl.ds(s, nch, stride=CMP_STRIDE), :])
        x0 = jnp.concatenate([xs[0] + pe_ref[role, 0, 2 * s2:2 * s2 + 1, :],
                              xs[1] + pe_ref[role, 0, 2 * s2 + 1:2 * s2 + 2, :]], axis=1)
        x1 = jnp.concatenate([xs[0] + pe_ref[role, 1, 2 * s2:2 * s2 + 1, :],
                              xs[1] + pe_ref[role, 1, 2 * s2 + 1:2 * s2 + 2, :]], axis=1)
        acc0 = acc0 + _dot(x0.astype(BF16), w1_ref[role, 0, s2])
        acc1 = acc1 + _dot(x1.astype(BF16), w1_ref[role, 1, s2])
    hid = acc0 + pltpu.roll(acc1, nch - 1, 0)
    return _dot(jax.nn.gelu(hid).astype(BF16), w2_ref[role])


def _cmp_kernel(k_ref, v_ref, w1_ref, pe_ref, w2_ref, kc_ref, vc_ref, *, nch):
    kc_ref[0] = _compress(k_ref, nch, w1_ref, pe_ref, w2_ref, 0)
    vc_ref[0] = _compress(v_ref, nch, w1_ref, pe_ref, w2_ref, 1)


def _compress_prompt(rows, lw, b, t):
    nch = t // CMP_STRIDE
    fixed = lambda i: (0,) * 5
    return pl.pallas_call(
        functools.partial(_cmp_kernel, nch=nch),
        out_shape=(jax.ShapeDtypeStruct((b, nch, LANES), F32),) * 2,
        grid=(b,),
        in_specs=[
            pl.BlockSpec((t, LANES), lambda i: (i, 0)),
            pl.BlockSpec((t, LANES), lambda i: (i, 1)),
            pl.BlockSpec(lw["cmp_w1"].shape, fixed),
            pl.BlockSpec(lw["cmp_pe"].shape, lambda i: (0,) * 4),
            pl.BlockSpec(lw["cmp_w2"].shape, lambda i: (0,) * 3),
        ],
        out_specs=(pl.BlockSpec((1, nch, LANES), lambda i: (i, 0, 0)),) * 2,
        compiler_params=_cparams(("parallel",)),
        name="compress_prompt",
    )(rows, rows, lw["cmp_w1"], lw["cmp_pe"], lw["cmp_w2"])


def _masked_softmax(s, mask):
    s = jnp.where(mask, s, NEG_INF)
    p = jnp.where(mask, jnp.exp(s - jnp.max(s, axis=-1, keepdims=True)), 0.0)
    return p / jnp.maximum(jnp.sum(p, axis=-1, keepdims=True), 1e-30)


def _split3(x):
    h1 = x.astype(BF16)
    r1 = x - h1.astype(F32)
    h2 = r1.astype(BF16)
    h3 = (r1 - h2.astype(F32)).astype(BF16)
    return h1, h2, h3


def _cmp_and_select(qb, qpos, nq, kc, vc, msel, n_slc):
    nrow = N_Q_HEADS * nq
    grow = HEADS_PER_GROUP * nq
    qpos_rows = jnp.concatenate([qpos] * N_Q_HEADS, axis=0)
    nch = kc.shape[0]
    s = _dot_nt(qb, kc.astype(BF16))
    cend = lax.broadcasted_iota(jnp.int32, (nrow, nch), 1) * CMP_STRIDE + (CMP_LEN - 1)
    p = _masked_softmax(s, cend <= qpos_rows)
    o_cmp = _dot(p.astype(BF16), vc.astype(BF16))

    jn = msel.shape[1]
    jid = lax.broadcasted_iota(jnp.int32, (nq, jn), 1)
    cur = qpos // SLC_BLOCK
    valid = jid * SLC_BLOCK <= qpos
    forced = (jid == 0) | (jid == cur) | (jid == cur - 1)
    sels = []
    for g in range(N_KV_HEADS):
        pg = p[g * grow:g * grow + nq]
        for hh in range(1, HEADS_PER_GROUP):
            pg = pg + p[g * grow + hh * nq:g * grow + (hh + 1) * nq]
        p_slc = sum(_dot(part, msel) for part in _split3(pg))
        score = jnp.where(valid, p_slc + jnp.where(forced, FORCE_BONUS, 0.0), NEG_INF)
        score = jnp.where(jid < n_slc, score, -3e38)
        k_sel = float(min(N_SELECT, n_slc))
        if nq == LANES and jn == LANES:
            nr = -(-n_slc // SUBLANES) * SUBLANES
            st = score.T[0:nr]
            rid = lax.broadcasted_iota(jnp.int32, (nr, nq), 0)
            rank = jnp.zeros((nr, nq), F32)
            for i in range(n_slc):
                row = st[i:i + 1, :]
                tie = jnp.where(rid > i, 1.0, 0.0)
                rank = rank + jnp.where(row > st, 1.0, jnp.where(row == st, tie, 0.0))
            chosen = jnp.where(rank < k_sel, 1.0, 0.0)
            sels.append(jnp.concatenate([chosen, jnp.zeros((jn - nr, nq), F32)], axis=0).T)
        else:
            rank = jnp.zeros((nq, jn), F32)
            for i in range(n_slc):
                col = score[:, i:i + 1]
                tie = jnp.where(jid > i, 1.0, 0.0)
                rank = rank + jnp.where(col > score, 1.0, jnp.where(col == score, tie, 0.0))
            sels.append(jnp.where(rank < k_sel, 1.0, 0.0))
    return o_cmp, sels


def _softmax_pv(s, bias, values_fn):
    s = s + bias
    e = jnp.exp(s - jnp.max(s, axis=-1, keepdims=True))
    return values_fn(e.astype(BF16)) / jnp.sum(e, axis=-1, keepdims=True)


def _slc_branch(qb, qpos, nq, sels, eexp, klen, scores_fn, values_fn):
    grow = HEADS_PER_GROUP * nq
    causal = lax.broadcasted_iota(jnp.int32, (nq, klen), 1) <= qpos
    chosen = _dot(jnp.concatenate(sels, axis=0).astype(BF16), eexp)
    outs = []
    for g in range(N_KV_HEADS):
        bias_g = jnp.where(causal, jnp.where(chosen[g * nq:(g + 1) * nq] > 0.5, 0.0, NEG_INF), NEG_INF)
        bias = jnp.concatenate([bias_g] * HEADS_PER_GROUP, axis=0)
        s = scores_fn(qb[g * grow:(g + 1) * grow])
        outs.append(_softmax_pv(s, bias, values_fn))
    return jnp.concatenate(outs, axis=0)


def _win_branch(qb, qpos, nq, k_win, v_win, wpos0):
    wpos = wpos0 + lax.broadcasted_iota(jnp.int32, (nq, k_win.shape[0]), 1)
    bias_q = jnp.where(wpos <= qpos, jnp.where(wpos > qpos - WINDOW, 0.0, NEG_INF), NEG_INF)
    bias = jnp.concatenate([bias_q] * N_Q_HEADS, axis=0)
    vb = v_win.astype(BF16)
    return _softmax_pv(_dot_nt(qb, k_win.astype(BF16)), bias, lambda pr: _dot(pr, vb))


def _combine(gates, o_cmp, o_slc, o_win, nq):
    lane = lax.broadcasted_iota(jnp.int32, (nq, LANES), 1)
    outs = []
    for h in range(N_Q_HEADS):
        r0 = h * nq
        o = (gates[:, 3 * h:3 * h + 1] * o_cmp[r0:r0 + nq]
             + gates[:, 3 * h + 1:3 * h + 2] * o_slc[r0:r0 + nq]
             + gates[:, 3 * h + 2:3 * h + 3] * o_win[r0:r0 + nq])
        g = h // HEADS_PER_GROUP
        outs.append(jnp.where((lane >= g * HEAD_DIM) & (lane < (g + 1) * HEAD_DIM), o, 0.0))
    return outs


def _attn_prompt_kernel(q_ref, rows_ref, win_ref, kc_ref, vc_ref, gate_ref, msel_ref, eexp_ref,
                        o_ref, oslc_ref, *, t, n_slc, key_step):
    qblk = pl.program_id(1)
    start = qblk * Q_BLOCK
    scale = HEAD_DIM ** -0.5
    q = jnp.concatenate([q_ref[:, h * LANES:(h + 1) * LANES] for h in range(N_Q_HEADS)], axis=0) * scale
    qb = q.astype(BF16)
    qpos = start + lax.broadcasted_iota(jnp.int32, (Q_BLOCK, 1), 0)
    o_cmp, sels = _cmp_and_select(qb, qpos, Q_BLOCK, kc_ref[0], vc_ref[0], msel_ref[...], n_slc)

    def slc_for(klen):
        ksb = rows_ref[0:klen, 2 * LANES:3 * LANES].astype(BF16)
        vsb = rows_ref[0:klen, 3 * LANES:4 * LANES].astype(BF16)
        oslc_ref[...] = _slc_branch(qb, qpos, Q_BLOCK, sels, eexp_ref[:, 0:klen], klen,
                                    lambda qg: _dot_nt(qg, ksb), lambda pr: _dot(pr, vsb))

    per = key_step // Q_BLOCK
    for v in range(t // key_step):
        pl.when(qblk // per == v)(functools.partial(slc_for, (v + 1) * key_step))

    w0 = pl.multiple_of(jnp.maximum(start - WINDOW, 0), Q_BLOCK)
    wlen = WINDOW + Q_BLOCK
    o_win = _win_branch(qb, qpos, Q_BLOCK, win_ref[pl.ds(w0, wlen), 0:LANES],
                        win_ref[pl.ds(w0, wlen), LANES:2 * LANES], w0)
    outs = _combine(gate_ref[...], o_cmp, oslc_ref[...], o_win, Q_BLOCK)
    for h in range(N_Q_HEADS):
        o_ref[:, h * LANES:(h + 1) * LANES] = outs[h]


def _attn_prompt(q, rows, win, kc, vc, gates, consts, b, t):
    nb = t // Q_BLOCK
    nch = t // CMP_STRIDE
    msel, eexp, n_slc = consts
    tile = lambda i, j: (i * nb + j, 0)
    batch = lambda i, j: (i, 0)
    return pl.pallas_call(
        functools.partial(_attn_prompt_kernel, t=t, n_slc=n_slc, key_step=SLC_KEY_STEP),
        out_shape=jax.ShapeDtypeStruct((b * t, Q_PAD), F32),
        grid=(b, nb),
        in_specs=[
            pl.BlockSpec((Q_BLOCK, Q_PAD), tile),
            pl.BlockSpec((t, 4 * LANES), batch),
            pl.BlockSpec((t, 2 * LANES), batch),
            pl.BlockSpec((1, nch, LANES), lambda i, j: (i, 0, 0)),
            pl.BlockSpec((1, nch, LANES), lambda i, j: (i, 0, 0)),
            pl.BlockSpec((Q_BLOCK, LANES), tile),
            pl.BlockSpec(msel.shape, lambda i, j: (0, 0)),
            pl.BlockSpec(eexp.shape, lambda i, j: (0, 0)),
        ],
        out_specs=pl.BlockSpec((Q_BLOCK, Q_PAD), tile),
        scratch_shapes=[pltpu.VMEM((N_Q_HEADS * Q_BLOCK, LANES), F32)],
        compiler_params=_cparams(("parallel", "parallel")),
        name="attn_prompt",
    )(q, rows, win, kc, vc, gates, msel, eexp)


def _attn_sample_kernel(pt_ref, q_ref, new_ref, nwin_ref, state_ref, gate_ref, cache_ref,
                        w1_ref, pe_ref, w2_ref, msel_ref, eexp_ref, o_ref, raw_ref, past_ref, kvt_ref, sem,
                        *, layer, n_pages, n_slc):
    b = pl.program_id(0)
    nb = pl.num_programs(0)
    past_len = n_pages * PAGE_SIZE
    cmp_roles, slc_roles = 0, 1

    def page_copy(seq, pg, pair):
        return pltpu.make_async_copy(
            cache_ref.at[pt_ref[seq, pg], pl.ds(layer * 4 + 2 * pair, 2)],
            raw_ref.at[pl.ds(2 * pair, 2), :, pl.ds(pg * PAGE_SIZE, PAGE_SIZE)],
            sem.at[pair, pg])

    def start_pages(seq, pair):
        for pg in range(n_pages):
            page_copy(seq, pg, pair).start()

    @pl.when(b == 0)
    def _():
        start_pages(b, cmp_roles)
        start_pages(b, slc_roles)

    pages_per_trip = 2 if n_pages % 2 == 0 else 1

    def land_pages(i, carry):
        for j in range(pages_per_trip):
            page_copy(b, i * pages_per_trip + j, cmp_roles).wait()
        for j in range(pages_per_trip):
            off = pl.multiple_of((i * pages_per_trip + j) * PAGE_SIZE, PAGE_SIZE)
            for role in range(2):
                past_ref[role, pl.ds(off, PAGE_SIZE), :] = raw_ref[role, :, pl.ds(off, PAGE_SIZE)].T
        return carry

    lax.fori_loop(0, n_pages // pages_per_trip, land_pages, 0)

    @pl.when(b + 1 < nb)
    def _():
        start_pages(b + 1, cmp_roles)

    nch = past_len // CMP_STRIDE
    kc = _compress(past_ref.at[0], nch, w1_ref, pe_ref, w2_ref, 0)
    vc = _compress(past_ref.at[1], nch, w1_ref, pe_ref, w2_ref, 1)

    scale = HEAD_DIM ** -0.5
    qv = q_ref[0]
    q = jnp.concatenate([qv[:, h * LANES:(h + 1) * LANES] for h in range(N_Q_HEADS)], axis=0) * scale
    qb = q.astype(BF16)
    qpos = past_len + lax.broadcasted_iota(jnp.int32, (SAMPLE_T, 1), 0)
    o_cmp, sels = _cmp_and_select(qb, qpos, SAMPLE_T, kc, vc, msel_ref[...], n_slc)

    pad = jnp.zeros((LANES - SAMPLE_T, LANES), F32)
    new = new_ref[0]
    k_new = jnp.concatenate([new[:, 2 * LANES:3 * LANES], pad], axis=0).astype(BF16)
    v_new = jnp.concatenate([new[:, 3 * LANES:4 * LANES], pad], axis=0).astype(BF16)
    for pg in range(n_pages):
        page_copy(b, pg, slc_roles).wait()
    kvt_ref[0] = raw_ref[2].astype(BF16)
    kvt_ref[1] = raw_ref[3].astype(BF16)

    @pl.when(b + 1 < nb)
    def _():
        start_pages(b + 1, slc_roles)

    kt = kvt_ref[0]
    vt = kvt_ref[1]

    o_slc = _slc_branch(
        qb, qpos, SAMPLE_T, sels, eexp_ref[...], past_len + LANES,
        lambda qg: jnp.concatenate([_dot(qg, kt), _dot_nt(qg, k_new)], axis=1),
        lambda pr: _dot_nt(pr[:, 0:past_len], vt) + _dot(pr[:, past_len:], v_new))

    nw = nwin_ref[0]
    st = state_ref[0, 0]
    k_win = jnp.concatenate([st[:, 0:LANES], nw[:, 0:LANES], pad], axis=0)
    v_win = jnp.concatenate([st[:, LANES:2 * LANES], nw[:, LANES:2 * LANES], pad], axis=0)
    o_win = _win_branch(qb, qpos, SAMPLE_T, k_win, v_win, past_len - WINDOW)
    outs = _combine(gate_ref[0], o_cmp, o_slc, o_win, SAMPLE_T)
    for h in range(N_Q_HEADS):
        o_ref[0, :, h * LANES:(h + 1) * LANES] = outs[h]


def _attn_sample(page_table, q, new_rows, new_win, state_win, gates, cache, lw, consts, layer):
    bs, n_pages = page_table.shape
    msel, eexp, n_slc = consts
    past_len = n_pages * PAGE_SIZE
    per_b = lambda i, pt: (i, 0, 0)
    grid_spec = pltpu.PrefetchScalarGridSpec(
        num_scalar_prefetch=1,
        grid=(bs,),
        in_specs=[
            pl.BlockSpec((1, SAMPLE_T, Q_PAD), per_b),
            pl.BlockSpec((1, SAMPLE_T, 4 * LANES), per_b),
            pl.BlockSpec((1, SAMPLE_T, 2 * LANES), per_b),
            pl.BlockSpec((1, 1, WINDOW, 2 * LANES), lambda i, pt: (layer, i, 0, 0)),
            pl.BlockSpec((1, SAMPLE_T, LANES), per_b),
            pl.BlockSpec(memory_space=pl.ANY),
            pl.BlockSpec(lw["cmp_w1"].shape, lambda i, pt: (0,) * 5),
            pl.BlockSpec(lw["cmp_pe"].shape, lambda i, pt: (0,) * 4),
            pl.BlockSpec(lw["cmp_w2"].shape, lambda i, pt: (0,) * 3),
            pl.BlockSpec(msel.shape, lambda i, pt: (0, 0)),
            pl.BlockSpec(eexp.shape, lambda i, pt: (0, 0)),
        ],
        out_specs=pl.BlockSpec((1, SAMPLE_T, Q_PAD), per_b),
        scratch_shapes=[
            pltpu.VMEM((4, LANES, past_len), F32),
            pltpu.VMEM((2, past_len, LANES), F32),
            pltpu.VMEM((2, LANES, past_len), BF16),
            pltpu.SemaphoreType.DMA((2, n_pages)),
        ],
    )
    return pl.pallas_call(
        functools.partial(_attn_sample_kernel, layer=layer, n_pages=n_pages, n_slc=n_slc),
        out_shape=jax.ShapeDtypeStruct((bs, SAMPLE_T, Q_PAD), F32),
        grid_spec=grid_spec,
        compiler_params=_cparams(("arbitrary",)),
        name="attn_sample",
    )(page_table, q, new_rows, new_win, state_win, gates, cache,
      lw["cmp_w1"], lw["cmp_pe"], lw["cmp_w2"], msel, eexp)


def _out_kernel(oa_ref, gb_ref, u_ref, p1_ref, p2_ref, cw_ref, x_ref, gm_ref, og_ref, w_ref, n2_ref,
                x1_ref, h2_ref):
    gmat = gm_ref[...]
    y_conv = cw_ref[0:1, :] * p2_ref[...] + cw_ref[1:2, :] * p1_ref[...] + cw_ref[2:3, :] * u_ref[...]
    o_conv = gb_ref[...] * y_conv
    parts = []
    for k in range(N_Q_HEADS):
        parts.append(_group_norm(oa_ref[:, k * LANES:(k + 1) * LANES], gmat,
                                 og_ref[:, k * LANES:(k + 1) * LANES]))
    for k in range(CONV_DIM // LANES):
        parts.append(_group_norm(o_conv[:, k * LANES:(k + 1) * LANES], gmat,
                                 og_ref[:, Q_PAD + k * LANES:Q_PAD + (k + 1) * LANES]))
    mix = jnp.concatenate(parts, axis=1).astype(BF16)
    x1 = x_ref[...] + _dot(mix, w_ref[...])
    x1_ref[...] = x1
    ms = jnp.mean(x1 * x1, axis=-1, keepdims=True)
    h2_ref[...] = (x1 * lax.rsqrt(ms + RMS_EPS) * n2_ref[...]).T.astype(BF16)


def _out_proj(o_attn, gb, u, prev1, prev2, x, lw, tm):
    n = x.shape[0]
    row = lambda i: (i, 0)
    fixed = lambda i: (0, 0)
    return pl.pallas_call(
        _out_kernel,
        out_shape=(jax.ShapeDtypeStruct((n, D_MODEL), F32), jax.ShapeDtypeStruct((D_MODEL, n), BF16)),
        grid=(n // tm,),
        in_specs=[
            pl.BlockSpec((tm, Q_PAD), row),
            pl.BlockSpec((tm, CONV_DIM), row),
            pl.BlockSpec((tm, CONV_DIM), row),
            pl.BlockSpec((tm, CONV_DIM), row),
            pl.BlockSpec((tm, CONV_DIM), row),
            pl.BlockSpec((SUBLANES, CONV_DIM), fixed),
            pl.BlockSpec((tm, D_MODEL), row),
            pl.BlockSpec((LANES, LANES), fixed),
            pl.BlockSpec((1, MIX_PAD), fixed),
            pl.BlockSpec((MIX_PAD, D_MODEL), fixed),
            pl.BlockSpec((1, D_MODEL), fixed),
        ],
        out_specs=(pl.BlockSpec((tm, D_MODEL), row), pl.BlockSpec((D_MODEL, tm), lambda i: (0, i))),
        compiler_params=_cparams(("parallel",)),
        name="out_proj",
    )(o_attn, gb, u, prev1, prev2, lw["conv_w"], x, lw["gmat"], lw["out_g"], lw["w_out"], lw["norm2"])


_CAND_ROWS = 80
PEER_EB = 2048
PEER_SUB = 256
PEER_GATE_TILES = 2
ROUTE_TOKENS = 256


def _route_kernel(h_ref, wq_ref, sk_ref, a1_ref, cnt_ref, b_ref, r2_ref,
                  qt_ref, s_ref, so_ref, rank_ref, v_ref, cand_ref, cs_ref, z_ref):
    tn = h_ref.shape[1]
    k = PEER_TOPK
    qt_ref[...] = _dot(wq_ref[...], h_ref[...])

    def twice_bf16(x):
        bits = pltpu.bitcast(x.astype(BF16).astype(F32), jnp.uint32)
        return bits | (bits >> 16)
    rowid = lax.broadcasted_iota(jnp.int32, (PEER_NKEYS, tn), 0).astype(F32)

    crow = lax.broadcasted_iota(jnp.int32, (_CAND_ROWS, tn), 0)
    cr = jnp.where(crow < 16, 0, jnp.where(crow < 72, 1 + (crow - 16) // 8, crow - 64))
    cj = jnp.where(crow < 16, crow, jnp.where(crow < 72, (crow - 16) % 8, 0))
    cvalid = (cr + 1) * (cj + 1) <= k
    cflat = (cr * k + cj).astype(F32)
    rid16 = lax.broadcasted_iota(jnp.int32, (k, tn), 0).astype(F32)
    neg = -jnp.inf

    def any_miscount(selected):
        count = jnp.sum(jnp.where(selected, 1.0, 0.0), axis=0, keepdims=True)
        return jnp.max(jnp.where(count != float(k), 1.0, 0.0))

    def reset_scores():
        for p in range(2):
            s_ref[p] = so_ref[p]
            rank_ref[p] = jnp.full((PEER_NKEYS, tn), float(k), F32)

    def topk_keys(exact):
        for r in range(k):
            for p in range(2):
                s = s_ref[p]
                m = jnp.max(s, axis=0, keepdims=True)
                if exact:
                    idx = jnp.min(jnp.where(s == m, rowid, float(PEER_NKEYS)), axis=0, keepdims=True)
                    hit = rowid == idx
                else:
                    hit = s == m
                s_ref[p] = jnp.where(hit, neg, s)
                rank_ref[p] = jnp.where(hit, float(r), rank_ref[p])
                v_ref[p, r:r + 1, :] = m

    def build_candidates():
        v1 = v_ref[0]
        v2 = v_ref[1]
        cand_ref[0:16, :] = v1[0:1] + v2
        for r in range(1, 8):
            cand_ref[8 + 8 * r:16 + 8 * r, :] = v1[r:r + 1] + v2[0:8]
        cand_ref[72:80, :] = v1[8:16] + v2[0:1]
        cand_ref[...] = jnp.where(cvalid, cand_ref[...], neg)

    def topk_pairs_quick():
        cmax = v_ref[0, 0:1, :] + v_ref[1, 0:1, :]
        zsum = jnp.zeros((1, tn), F32)
        for _ in range(k):
            c = cand_ref[...]
            m = jnp.max(c, axis=0, keepdims=True)
            cand_ref[...] = jnp.where(c == m, neg, c)
            zsum = zsum + jnp.exp(m - cmax)
        taken = cvalid & (cand_ref[...] == neg)
        tk = jnp.where(taken, 1.0, 0.0)
        cs_ref[0:1, :] = jnp.sum(tk[0:16], axis=0, keepdims=True)
        for r in range(1, 8):
            cs_ref[r:r + 1, :] = jnp.sum(tk[8 + 8 * r:16 + 8 * r], axis=0, keepdims=True)
        cs_ref[8:16, :] = tk[72:80]
        z_ref[0:1, :] = zsum
        return any_miscount(taken)

    def topk_pairs_exact():
        cmax = v_ref[0, 0:1, :] + v_ref[1, 0:1, :]
        cnt = jnp.zeros((k, tn), F32)
        zsum = jnp.zeros((1, tn), F32)
        for _ in range(k):
            c = cand_ref[...]
            m = jnp.max(c, axis=0, keepdims=True)
            f = jnp.min(jnp.where(c == m, cflat, 1e9), axis=0, keepdims=True)
            cand_ref[...] = jnp.where(cflat == f, neg, c)
            cnt = cnt + jnp.where(rid16 == jnp.floor(f * (1.0 / k)), 1.0, 0.0)
            zsum = zsum + jnp.exp(m - cmax)
        cs_ref[...] = cnt
        z_ref[0:1, :] = zsum

    def head_body(h, carry):
        for p in range(2):
            base = pl.multiple_of(h * PEER_QDIM + p * (PEER_QDIM // 2), PEER_QDIM // 2)
            so_ref[p] = _dot(sk_ref[2 * h + p], qt_ref[pl.ds(base, PEER_QDIM // 2), :].astype(BF16))
        reset_scores()
        topk_keys(exact=False)
        tied = jnp.maximum(any_miscount(rank_ref[0] < float(k)), any_miscount(rank_ref[1] < float(k)))

        @pl.when(tied > 0.5)
        def _():
            reset_scores()
            topk_keys(exact=True)

        build_candidates()
        tied_pairs = topk_pairs_quick()

        @pl.when(tied_pairs > 0.5)
        def _():
            build_candidates()
            topk_pairs_exact()

        v1 = v_ref[0]
        v2 = v_ref[1]
        cnt = cs_ref[...]
        zsum = z_ref[0:1, :]
        r1 = rank_ref[0]
        r2 = rank_ref[1]
        a1_ref[h] = twice_bf16(jnp.where(r1 < float(k), jnp.exp(so_ref[0] - v1[0:1]), 0.0))
        bval = jnp.where(r2 < float(k), jnp.exp(so_ref[1] - v2[0:1]), 0.0) / zsum
        b_ref[h] = pltpu.bitcast(bval.astype(BF16), jnp.uint32)
        r2_ref[h] = pltpu.bitcast(r2.astype(BF16), jnp.uint32)
        cdense = jnp.zeros((PEER_NKEYS, tn), F32)
        for r in range(k):
            cdense = jnp.where(r1 == float(r), cnt[r:r + 1], cdense)
        cnt_ref[h] = twice_bf16(cdense)
        return carry

    lax.fori_loop(0, PEER_HEADS, head_body, 0)


def _route(h2, lw, tn):
    n = h2.shape[1]
    shape = jax.ShapeDtypeStruct((PEER_HEADS, PEER_NKEYS, n), jnp.uint32)
    shape_b = jax.ShapeDtypeStruct((PEER_HEADS, PEER_NKEYS // 2, n), jnp.uint32)
    spec = pl.BlockSpec((PEER_HEADS, PEER_NKEYS, tn), lambda i: (0, 0, i))
    spec_b = pl.BlockSpec((PEER_HEADS, PEER_NKEYS // 2, tn), lambda i: (0, 0, i))
    return pl.pallas_call(
        _route_kernel,
        out_shape=(shape, shape, shape_b, shape_b),
        grid=(n // tn,),
        in_specs=[
            pl.BlockSpec((D_MODEL, tn), lambda i: (0, i)),
            pl.BlockSpec((PEER_HEADS * PEER_QDIM, D_MODEL), lambda i: (0, 0)),
            pl.BlockSpec((2 * PEER_HEADS, PEER_NKEYS, PEER_QDIM // 2), lambda i: (0, 0, 0)),
        ],
        out_specs=(spec, spec, spec_b, spec_b),
        scratch_shapes=[
            pltpu.VMEM((PEER_HEADS * PEER_QDIM, tn), F32),
            pltpu.VMEM((2, PEER_NKEYS, tn), F32),
            pltpu.VMEM((2, PEER_NKEYS, tn), F32),
            pltpu.VMEM((2, PEER_NKEYS, tn), F32),
            pltpu.VMEM((2, PEER_TOPK, tn), F32),
            pltpu.VMEM((_CAND_ROWS, tn), F32),
            pltpu.VMEM((PEER_TOPK, tn), F32),
            pltpu.VMEM((SUBLANES, tn), F32),
        ],
        compiler_params=_cparams(("parallel",)),
        name="peer_route",
    )(h2, lw["peer_wq_t"], lw["peer_sk"])


def _gelu_tanh(x):
    c = math.sqrt(2.0 / math.pi)
    hx = 0.5 * x
    return hx + hx * jnp.tanh(x * (c + (c * 0.044715) * (x * x)))


def _peer_kernel(h_ref, u_ref, vt_ref, a1_ref, cnt_ref, b_ref, r2_ref, x_ref, y_ref,
                 acc_ref, at_ref, *, eb, sub):
    e = pl.program_id(1)
    tn = h_ref.shape[1]
    pack = 2 * SUBLANES

    @pl.when(e == 0)
    def _():
        acc_ref[...] = jnp.zeros_like(acc_ref)

    h = h_ref[...]
    zero = jnp.zeros((pack, LANES), BF16)

    def row_pair(ref, hh, ai, cols):
        return pltpu.bitcast(jnp.broadcast_to(ref[hh, ai:ai + 1, cols], (SUBLANES, LANES)), BF16)

    def sub_block(sb):
        r0 = sb * sub
        st = _dot(u_ref[r0:r0 + sub, :], h)
        n_a = PEER_GATE_TILES
        n_k = PEER_NKEYS // pack
        for a0 in range(0, sub // PEER_NKEYS, n_a):
            ai0 = r0 // PEER_NKEYS + a0
            for c in range(tn // LANES):
                cols = slice(c * LANES, (c + 1) * LANES)
                w = [[None] * n_k for _ in range(n_a)]
                for hh in range(PEER_HEADS):
                    cnt = [row_pair(cnt_ref, hh, ai0 + a, cols) for a in range(n_a)]
                    a1 = [row_pair(a1_ref, hh, ai0 + a, cols) for a in range(n_a)]
                    for k in range(n_k):
                        rk = slice(k * SUBLANES, (k + 1) * SUBLANES)
                        r2 = pltpu.bitcast(r2_ref[hh, rk, cols], BF16)
                        bb = pltpu.bitcast(b_ref[hh, rk, cols], BF16)
                        for a in range(n_a):
                            t = a1[a] * jnp.where(r2 < cnt[a], bb, zero)
                            w[a][k] = t if w[a][k] is None else w[a][k] + t
                for a in range(n_a):
                    lo = (a0 + a) * PEER_NKEYS
                    g = _gelu_tanh(st[lo:lo + PEER_NKEYS, cols].astype(BF16))
                    at_ref[r0 + lo:r0 + lo + PEER_NKEYS, cols] = jnp.concatenate(w[a], axis=0) * g

    n_sb = eb // sub
    half = eb // 2
    tot = None
    for sb in range(n_sb):
        sub_block(sb)
        if (sb + 1) * sub % half == 0:
            k0 = (sb + 1) * sub - half
            d = lax.dot_general(vt_ref[k0:k0 + half, :], at_ref[k0:k0 + half, :],
                                (((0,), (0,)), ((), ())), preferred_element_type=F32)
            tot = d if tot is None else tot + d
    acc_ref[...] += tot

    @pl.when(e == pl.num_programs(1) - 1)
    def _():
        y_ref[...] = x_ref[...] + acc_ref[...].T


def _peer(h2, routed, x1, peer_u, peer_v, layer, tn, eb):
    n = h2.shape[1]
    n_exp = PEER_NKEYS * PEER_NKEYS
    a1, cnt, bb, r2 = routed
    tok = lambda i, e: (i, 0)
    key_rows = pl.BlockSpec((PEER_HEADS, eb // PEER_NKEYS, tn), lambda i, e: (0, e, i))
    key_full = pl.BlockSpec((PEER_HEADS, PEER_NKEYS // 2, tn), lambda i, e: (0, 0, i))
    return pl.pallas_call(
        functools.partial(_peer_kernel, eb=eb, sub=PEER_SUB),
        out_shape=jax.ShapeDtypeStruct((n, D_MODEL), F32),
        grid=(n // tn, n_exp // eb),
        in_specs=[
            pl.BlockSpec((D_MODEL, tn), lambda i, e: (0, i)),
            pl.BlockSpec((None, eb, D_MODEL), lambda i, e: (layer, e, 0)),
            pl.BlockSpec((None, eb, D_MODEL), lambda i, e: (layer, e, 0)),
            key_rows, key_rows, key_full, key_full,
            pl.BlockSpec((tn, D_MODEL), tok),
        ],
        out_specs=pl.BlockSpec((tn, D_MODEL), tok),
        scratch_shapes=[pltpu.VMEM((D_MODEL, tn), F32), pltpu.VMEM((eb, tn), BF16)],
        compiler_params=_cparams(("parallel", "arbitrary")),
        name="peer_experts",
    )(h2, peer_u, peer_v, a1, cnt, bb, r2, x1)


def _rope_tables(pos):
    inv_freq = ROPE_THETA ** (-jnp.arange(ROPE_HALF, dtype=F32) / ROPE_HALF)
    ang = pos.astype(F32)[:, None] * inv_freq[None, :]
    cos, sin = jnp.cos(ang), jnp.sin(ang)
    n = pos.shape[0]
    ones = jnp.ones((n, HEAD_DIM - ROPE_DIM), F32)
    zeros = jnp.zeros((n, HEAD_DIM - ROPE_DIM), F32)
    zh = jnp.zeros((n, ROPE_HALF), F32)
    c = jnp.concatenate([cos, cos, ones], axis=1)
    s_up = jnp.concatenate([-sin, zh, zeros], axis=1)
    s_dn = jnp.concatenate([zh, sin, zeros], axis=1)
    return tuple(jnp.tile(a, (1, 2)) for a in (c, s_up, s_dn))


def _attn_consts(n_cmp, nch, n_slc, slc_len):
    jn = -(-n_slc // LANES) * LANES
    sr = SLC_BLOCK // CMP_STRIDE
    r = CMP_LEN // CMP_STRIDE
    msel = np.zeros((nch, jn), np.float32)
    for j in range(n_slc):
        for m in range(sr):
            for n in range(r):
                i = sr * j + m - n
                if 0 <= i < n_cmp:
                    msel[i, j] += 1.0
    eexp = np.zeros((jn, slc_len), np.float32)
    keys = np.arange(slc_len)
    eexp[keys // SLC_BLOCK, keys] = 1.0
    return jnp.asarray(msel, BF16), jnp.asarray(eexp, BF16), n_slc


def _pad_heads(x, axis):
    shape = x.shape
    x = jnp.moveaxis(x, axis, -1).reshape(shape[:axis] + shape[axis + 1:] + (N_KV_HEADS, HEADS_PER_GROUP, HEAD_DIM))
    nd = x.ndim
    lo = jnp.pad(x[..., 0, :, :], [(0, 0)] * (nd - 2) + [(0, HEAD_DIM)])
    hi = jnp.pad(x[..., 1, :, :], [(0, 0)] * (nd - 2) + [(HEAD_DIM, 0)])
    out = jnp.concatenate([lo, hi], axis=-2)
    out = out.reshape(out.shape[:-2] + (Q_PAD,))
    return jnp.moveaxis(out, -1, axis)


def _prep_weights(norm1_g, w_in, q_norm_g, k_norm_g, cmp_pe, cmp_w1, cmp_w2, conv_w, out_norm_g,
                  w_out, norm2_g, peer_wq, peer_subkeys, peer_u, peer_v):
    depth = w_in.shape[0]
    c1 = NSA_WIDTH
    c2 = c1 + 6 * KV_WIDTH
    c3 = c2 + N_Q_HEADS * N_BRANCH
    gl_pad = jnp.zeros((depth, D_MODEL, LANES - N_Q_HEADS * N_BRANCH), F32)
    w_all = jnp.concatenate([_pad_heads(w_in[:, :, :c1], 2), w_in[:, :, c1:c2], w_in[:, :, c3:],
                             w_in[:, :, c2:c3], gl_pad], axis=2)

    blk = np.kron(np.eye(2, dtype=np.float32), np.full((HEAD_DIM, HEAD_DIM), 1.0 / HEAD_DIM, np.float32))
    kg = jnp.pad(jnp.tile(k_norm_g, (1, 1, 2)), ((0, 0), (0, SUBLANES - N_BRANCH), (0, 0)))

    r = CMP_LEN // CMP_STRIDE
    eye2 = jnp.eye(2, dtype=F32)
    w1 = cmp_w1.reshape(depth, 2, r, CMP_STRIDE, HEAD_DIM, HEAD_DIM)
    w1 = jnp.einsum('lxmsde,gk->lxmsgdke', w1, eye2).reshape(depth, 2, r, CMP_STRIDE // 2, 2 * LANES, LANES)
    pe = jnp.tile(cmp_pe.reshape(depth, 2, r, CMP_STRIDE, HEAD_DIM), (1, 1, 1, 1, 2))
    w2 = jnp.einsum('lxde,gk->lxgdke', cmp_w2, eye2).reshape(depth, 2, LANES, LANES)

    og_pad = jnp.concatenate([_pad_heads(out_norm_g[:, :NSA_WIDTH], 1), out_norm_g[:, NSA_WIDTH:]], axis=1)
    wo_pad = jnp.concatenate([_pad_heads(w_out[:, :NSA_WIDTH], 1), w_out[:, NSA_WIDTH:]], axis=1)
    return {
        "norm1": norm1_g[:, None, :],
        "w_in": w_all.astype(BF16),
        "gmat": jnp.broadcast_to(jnp.asarray(blk, BF16), (depth, LANES, LANES)),
        "q_g": jnp.tile(q_norm_g, (1, 2))[:, None, :],
        "k_g": kg,
        "cmp_w1": w1.astype(BF16),
        "cmp_pe": pe,
        "cmp_w2": w2.astype(BF16),
        "conv_w": jnp.pad(conv_w, ((0, 0), (0, SUBLANES - CONV_WIDTH), (0, 0))),
        "out_g": og_pad[:, None, :],
        "w_out": wo_pad.astype(BF16),
        "norm2": norm2_g[:, None, :],
        "peer_wq_t": jnp.swapaxes(peer_wq, 1, 2).astype(BF16),
        "peer_sk": peer_subkeys.reshape(depth, 2 * PEER_HEADS, PEER_NKEYS, PEER_QDIM // 2).astype(BF16),
        "peer_u": peer_u.astype(BF16),
        "peer_vt": peer_v.astype(BF16),
    }


def _conv_prev(u, prefix, b, t):
    ext = jnp.concatenate([prefix, u.reshape(b, t, CONV_DIM)], axis=1)
    return ext[:, 1:t + 1].reshape(b * t, CONV_DIM), ext[:, 0:t].reshape(b * t, CONV_DIM)


def _row_tile(n):
    for tm in (256, 128):
        if n % tm == 0:
            return tm
    raise ValueError(f"token count {n} must be a multiple of 128")


def _peer_tile(n):
    for tn in (512, 256, 128):
        if n % tn == 0:
            return tn
    raise ValueError(f"token count {n} must be a multiple of 128")


def kernel(x_prompt, x_sample, cache_nsa_kv, state_win_kv, state_conv, page_table,
           norm1_g, w_in, q_norm_g, k_norm_g, cmp_pe, cmp_w1, cmp_w2, conv_w, out_norm_g, w_out,
           norm2_g, peer_wq, peer_subkeys, peer_u, peer_v):
    bp, t, _ = x_prompt.shape
    bs, ts, _ = x_sample.shape
    depth = norm1_g.shape[0]
    n_pages = page_table.shape[1]
    past_len = n_pages * PAGE_SIZE
    assert t % SLC_KEY_STEP == 0 and t >= WINDOW + Q_BLOCK and ts <= SAMPLE_T
    assert state_win_kv.shape[2] == WINDOW and past_len % SLC_BLOCK == 0

    xp = x_prompt.reshape(bp * t, D_MODEL)
    xs = jnp.pad(x_sample, ((0, 0), (0, SAMPLE_T - ts), (0, 0))).reshape(bs * SAMPLE_T, D_MODEL)
    cache = jnp.transpose(cache_nsa_kv, (0, 2, 3, 4, 5, 1)).reshape(
        cache_nsa_kv.shape[0], depth * 4, LANES, PAGE_SIZE)
    state_win = state_win_kv.reshape(depth, bs, WINDOW, 2 * LANES)

    tabs_p = _rope_tables(jnp.arange(t))
    tabs_s = _rope_tables(jnp.tile(past_len + jnp.arange(SAMPLE_T), bs))
    n_cmp_p = (t - CMP_LEN) // CMP_STRIDE + 1
    consts_p = _attn_consts(n_cmp_p, t // CMP_STRIDE, t // SLC_BLOCK, t)
    t_all = past_len + ts
    n_cmp_s = (t_all - CMP_LEN) // CMP_STRIDE + 1
    assert n_cmp_s <= past_len // CMP_STRIDE
    consts_s = _attn_consts(n_cmp_s, past_len // CMP_STRIDE, -(-t_all // SLC_BLOCK), past_len + LANES)
    zero_prefix = jnp.zeros((bp, CONV_WIDTH - 1, CONV_DIM), F32)

    weights = _prep_weights(norm1_g, w_in, q_norm_g, k_norm_g, cmp_pe, cmp_w1, cmp_w2, conv_w, out_norm_g,
                            w_out, norm2_g, peer_wq, peer_subkeys, peer_u, peer_v)
    peer_u_b = weights.pop("peer_u")
    peer_v_b = weights.pop("peer_vt")
    rows_p, rows_s, win_p, win_s, conv_p, conv_s = [], [], [], [], [], []
    for l in range(depth):
        lw = {name: stacked[l] for name, stacked in weights.items()}
        q, rows, win, gates, gb, u = _in_proj(xp, lw, tabs_p, _row_tile(bp * t))
        kc, vc = _compress_prompt(rows, lw, bp, t)
        o_attn = _attn_prompt(q, rows, win, kc, vc, gates, consts_p, bp, t)
        p1, p2 = _conv_prev(u, zero_prefix, bp, t)
        x1, h2 = _out_proj(o_attn, gb, u, p1, p2, xp, lw, _row_tile(bp * t))
        routed = _route(h2, lw, ROUTE_TOKENS if h2.shape[1] % ROUTE_TOKENS == 0 else LANES)
        xp = _peer(h2, routed, x1, peer_u_b, peer_v_b, l, _peer_tile(bp * t), PEER_EB)
        rows_p.append(rows.reshape(bp, t, 4, N_KV_HEADS, HEAD_DIM))
        win_p.append(win.reshape(bp, t, 2, N_KV_HEADS, HEAD_DIM)[:, t - WINDOW:])
        conv_p.append(u.reshape(bp, t, CONV_DIM)[:, t - (CONV_WIDTH - 1):])

        q, rows, win, gates, gb, u = _in_proj(xs, lw, tabs_s, _row_tile(bs * SAMPLE_T))
        o_attn = _attn_sample(
            page_table, q.reshape(bs, SAMPLE_T, Q_PAD), rows.reshape(bs, SAMPLE_T, 4 * LANES),
            win.reshape(bs, SAMPLE_T, 2 * LANES), state_win, gates.reshape(bs, SAMPLE_T, LANES),
            cache, lw, consts_s, l).reshape(bs * SAMPLE_T, Q_PAD)
        p1, p2 = _conv_prev(u, state_conv[l], bs, SAMPLE_T)
        x1, h2 = _out_proj(o_attn, gb, u, p1, p2, xs, lw, _row_tile(bs * SAMPLE_T))
        routed = _route(h2, lw, ROUTE_TOKENS if h2.shape[1] % ROUTE_TOKENS == 0 else LANES)
        xs = _peer(h2, routed, x1, peer_u_b, peer_v_b, l, _peer_tile(bs * SAMPLE_T), PEER_EB)
        rows_s.append(rows.reshape(bs, SAMPLE_T, 4, N_KV_HEADS, HEAD_DIM)[:, :ts])
        new_win = win.reshape(bs, SAMPLE_T, 2, N_KV_HEADS, HEAD_DIM)[:, :ts]
        old_win = state_win_kv[l]
        win_s.append(jnp.concatenate([old_win, new_win], axis=1)[:, ts:])
        conv_s.append(u.reshape(bs, SAMPLE_T, CONV_DIM)[:, ts - (CONV_WIDTH - 1):ts])

    y_prompt = xp.reshape(bp, t, D_MODEL)
    y_sample = xs.reshape(bs, SAMPLE_T, D_MODEL)[:, :ts]
    return (y_prompt, y_sample,
            jnp.stack(rows_p, axis=2), jnp.stack(rows_s, axis=2),
            jnp.stack(win_p, axis=0), jnp.stack(win_s, axis=0),
            jnp.stack(conv_p, axis=0), jnp.stack(conv_s, axis=0))
```

```python
import functools
import math

import numpy as np
import jax
import jax.numpy as jnp
from jax import lax
from jax.experimental import pallas as pl
from jax.experimental.pallas import tpu as pltpu

F32 = jnp.float32
BF16 = jnp.bfloat16

LANES = 128
SUBLANES = 8
VMEM_LIMIT = 56 * 1024 * 1024

D_MODEL = 1024
HEAD_DIM = 64
N_Q_HEADS = 8
N_KV_HEADS = 2
HEADS_PER_GROUP = N_Q_HEADS // N_KV_HEADS
NSA_WIDTH = N_Q_HEADS * HEAD_DIM
KV_WIDTH = N_KV_HEADS * HEAD_DIM
CONV_DIM = D_MODEL - NSA_WIDTH
CONV_WIDTH = 3
N_BRANCH = 3
ROPE_DIM = HEAD_DIM // 4
ROPE_HALF = ROPE_DIM // 2
ROPE_THETA = 500000.0
CMP_LEN = 32
CMP_STRIDE = 16
SLC_BLOCK = 64
N_SELECT = 16
WINDOW = 512
Q_BLOCK = 128
PAGE_SIZE = 128
PEER_HEADS = 8
PEER_QDIM = 256
PEER_NKEYS = 128
PEER_TOPK = 16
RMS_EPS = 1e-6
NEG_INF = -1e30
FORCE_BONUS = 1e4

Q_PAD = N_Q_HEADS * LANES
IN_COLS = Q_PAD + 6 * KV_WIDTH + 3 * CONV_DIM + LANES
MIX_PAD = Q_PAD + CONV_DIM
SAMPLE_T = 8
SLC_KEY_STEP = 256

_NT = (((1,), (1,)), ((), ()))


def _cparams(sem):
    return pltpu.CompilerParams(dimension_semantics=sem, vmem_limit_bytes=VMEM_LIMIT)


def _dot(a, b):
    return jnp.dot(a, b, preferred_element_type=F32)


def _dot_nt(a, b):
    return lax.dot_general(a, b, _NT, preferred_element_type=F32)


def _group_norm(x, gmat, gain):
    ss = _dot((x * x).astype(BF16), gmat)
    return x * lax.rsqrt(ss + RMS_EPS) * gain


def _in_kernel(x_ref, g1_ref, w_ref, gm_ref, qg_ref, kg_ref, c_ref, s1_ref, s2_ref,
               q_ref, rows_ref, win_ref, gate_ref, gb_ref, u_ref):
    x = x_ref[...]
    ms = jnp.mean(x * x, axis=-1, keepdims=True)
    h = (x * lax.rsqrt(ms + RMS_EPS) * g1_ref[...]).astype(BF16)
    z = _dot(h, w_ref[...])
    gmat = gm_ref[...]
    cos = c_ref[...]
    sin_up = s1_ref[...]
    sin_dn = s2_ref[...]

    def norm_rope(zb, gain):
        y = _group_norm(zb, gmat, gain)
        return (y * cos + pltpu.roll(y, LANES - ROPE_HALF, 1) * sin_up
                + pltpu.roll(y, ROPE_HALF, 1) * sin_dn)

    for hb in range(N_Q_HEADS):
        q_ref[:, hb * LANES:(hb + 1) * LANES] = norm_rope(z[:, hb * LANES:(hb + 1) * LANES], qg_ref[...])
    o = Q_PAD
    kv = [z[:, o + r * LANES:o + (r + 1) * LANES] for r in range(6)]
    rows_ref[:, 0 * LANES:1 * LANES] = norm_rope(kv[0], kg_ref[0:1, :])
    rows_ref[:, 1 * LANES:2 * LANES] = kv[1]
    rows_ref[:, 2 * LANES:3 * LANES] = norm_rope(kv[2], kg_ref[1:2, :])
    rows_ref[:, 3 * LANES:4 * LANES] = kv[3]
    win_ref[:, 0:LANES] = norm_rope(kv[4], kg_ref[2:3, :])
    win_ref[:, LANES:2 * LANES] = kv[5]
    o += 6 * LANES
    gb_ref[...] = z[:, o:o + CONV_DIM]
    u_ref[...] = z[:, o + CONV_DIM:o + 2 * CONV_DIM] * z[:, o + 2 * CONV_DIM:o + 3 * CONV_DIM]
    o += 3 * CONV_DIM
    gate_ref[...] = jax.nn.sigmoid(z[:, o:o + LANES])


def _in_proj(x, lw, tabs, tm):
    n = x.shape[0]
    cos, sin_up, sin_dn = tabs
    npos_tiles = cos.shape[0] // tm
    row = lambda i: (i, 0)
    fixed = lambda i: (0, 0)
    pos = lambda i: (i % npos_tiles, 0)
    out_shape = (
        jax.ShapeDtypeStruct((n, Q_PAD), F32),
        jax.ShapeDtypeStruct((n, 4 * LANES), F32),
        jax.ShapeDtypeStruct((n, 2 * LANES), F32),
        jax.ShapeDtypeStruct((n, LANES), F32),
        jax.ShapeDtypeStruct((n, CONV_DIM), F32),
        jax.ShapeDtypeStruct((n, CONV_DIM), F32),
    )
    return pl.pallas_call(
        _in_kernel,
        out_shape=out_shape,
        grid=(n // tm,),
        in_specs=[
            pl.BlockSpec((tm, D_MODEL), row),
            pl.BlockSpec((1, D_MODEL), fixed),
            pl.BlockSpec((D_MODEL, IN_COLS), fixed),
            pl.BlockSpec((LANES, LANES), fixed),
            pl.BlockSpec((1, LANES), fixed),
            pl.BlockSpec((SUBLANES, LANES), fixed),
            pl.BlockSpec((tm, LANES), pos),
            pl.BlockSpec((tm, LANES), pos),
            pl.BlockSpec((tm, LANES), pos),
        ],
        out_specs=(
            pl.BlockSpec((tm, Q_PAD), row),
            pl.BlockSpec((tm, 4 * LANES), row),
            pl.BlockSpec((tm, 2 * LANES), row),
            pl.BlockSpec((tm, LANES), row),
            pl.BlockSpec((tm, CONV_DIM), row),
            pl.BlockSpec((tm, CONV_DIM), row),
        ),
        compiler_params=_cparams(("parallel",)),
        name="in_proj",
    )(x, lw["norm1"], lw["w_in"], lw["gmat"], lw["q_g"], lw["k_g"], cos, sin_up, sin_dn)


def _compress(src_ref, nch, w1_ref, pe_ref, w2_ref, role):
    acc0 = jnp.zeros((nch, LANES), F32)
    acc1 = jnp.zeros((nch, LANES), F32)
    for s2 in range(CMP_STRIDE // 2):
        xs = []
        for s in (2 * s2, 2 * s2 + 1):
            xs.append(src_ref[pl.ds(s, nch, stride=CMP_STRIDE), :])
        x0 = jnp.concatenate([xs[0] + pe_ref[role, 0, 2 * s2:2 * s2 + 1, :],
                              xs[1] + pe_ref[role, 0, 2 * s2 + 1:2 * s2 + 2, :]], axis=1)
        x1 = jnp.concatenate([xs[0] + pe_ref[role, 1, 2 * s2:2 * s2 + 1, :],
                              xs[1] + pe_ref[role, 1, 2 * s2 + 1:2 * s2 + 2, :]], axis=1)
        acc0 = acc0 + _dot(x0.astype(BF16), w1_ref[role, 0, s2])
        acc1 = acc1 + _dot(x1.astype(BF16), w1_ref[role, 1, s2])
    hid = acc0 + pltpu.roll(acc1, nch - 1, 0)
    return _dot(jax.nn.gelu(hid).astype(BF16), w2_ref[role])


def _cmp_kernel(k_ref, v_ref, w1_ref, pe_ref, w2_ref, kc_ref, vc_ref, *, nch):
    kc_ref[0] = _compress(k_ref, nch, w1_ref, pe_ref, w2_ref, 0)
    vc_ref[0] = _compress(v_ref, nch, w1_ref, pe_ref, w2_ref, 1)


def _compress_prompt(rows, lw, b, t):
    nch = t // CMP_STRIDE
    fixed = lambda i: (0,) * 5
    return pl.pallas_call(
        functools.partial(_cmp_kernel, nch=nch),
        out_shape=(jax.ShapeDtypeStruct((b, nch, LANES), F32),) * 2,
        grid=(b,),
        in_specs=[
            pl.BlockSpec((t, LANES), lambda i: (i, 0)),
            pl.BlockSpec((t, LANES), lambda i: (i, 1)),
            pl.BlockSpec(lw["cmp_w1"].shape, fixed),
            pl.BlockSpec(lw["cmp_pe"].shape, lambda i: (0,) * 4),
            pl.BlockSpec(lw["cmp_w2"].shape, lambda i: (0,) * 3),
        ],
        out_specs=(pl.BlockSpec((1, nch, LANES), lambda i: (i, 0, 0)),) * 2,
        compiler_params=_cparams(("parallel",)),
        name="compress_prompt",
    )(rows, rows, lw["cmp_w1"], lw["cmp_pe"], lw["cmp_w2"])


def _masked_softmax(s, mask):
    s = jnp.where(mask, s, NEG_INF)
    p = jnp.where(mask, jnp.exp(s - jnp.max(s, axis=-1, keepdims=True)), 0.0)
    return p / jnp.maximum(jnp.sum(p, axis=-1, keepdims=True), 1e-30)


def _split3(x):
    h1 = x.astype(BF16)
    r1 = x - h1.astype(F32)
    h2 = r1.astype(BF16)
    h3 = (r1 - h2.astype(F32)).astype(BF16)
    return h1, h2, h3


def _cmp_and_select(qb, qpos, nq, kc, vc, msel, n_slc):
    nrow = N_Q_HEADS * nq
    grow = HEADS_PER_GROUP * nq
    qpos_rows = jnp.concatenate([qpos] * N_Q_HEADS, axis=0)
    nch = kc.shape[0]
    s = _dot_nt(qb, kc.astype(BF16))
    cend = lax.broadcasted_iota(jnp.int32, (nrow, nch), 1) * CMP_STRIDE + (CMP_LEN - 1)
    p = _masked_softmax(s, cend <= qpos_rows)
    o_cmp = _dot(p.astype(BF16), vc.astype(BF16))

    jn = msel.shape[1]
    jid = lax.broadcasted_iota(jnp.int32, (nq, jn), 1)
    cur = qpos // SLC_BLOCK
    valid = jid * SLC_BLOCK <= qpos
    forced = (jid == 0) | (jid == cur) | (jid == cur - 1)
    sels = []
    for g in range(N_KV_HEADS):
        pg = p[g * grow:g * grow + nq]
        for hh in range(1, HEADS_PER_GROUP):
            pg = pg + p[g * grow + hh * nq:g * grow + (hh + 1) * nq]
        p_slc = sum(_dot(part, msel) for part in _split3(pg))
        score = jnp.where(valid, p_slc + jnp.where(forced, FORCE_BONUS, 0.0), NEG_INF)
        score = jnp.where(jid < n_slc, score, -3e38)
        k_sel = float(min(N_SELECT, n_slc))
        if nq == LANES and jn == LANES:
            nr = -(-n_slc // SUBLANES) * SUBLANES
            st = score.T[0:nr]
            rid = lax.broadcasted_iota(jnp.int32, (nr, nq), 0)
            rank = jnp.zeros((nr, nq), F32)
            for i in range(n_slc):
                row = st[i:i + 1, :]
                tie = jnp.where(rid > i, 1.0, 0.0)
                rank = rank + jnp.where(row > st, 1.0, jnp.where(row == st, tie, 0.0))
            chosen = jnp.where(rank < k_sel, 1.0, 0.0)
            sels.append(jnp.concatenate([chosen, jnp.zeros((jn - nr, nq), F32)], axis=0).T)
        else:
            rank = jnp.zeros((nq, jn), F32)
            for i in range(n_slc):
                col = score[:, i:i + 1]
                tie = jnp.where(jid > i, 1.0, 0.0)
                rank = rank + jnp.where(col > score, 1.0, jnp.where(col == score, tie, 0.0))
            sels.append(jnp.where(rank < k_sel, 1.0, 0.0))
    return o_cmp, sels


def _softmax_pv(s, bias, values_fn):
    s = s + bias
    e = jnp.exp(s - jnp.max(s, axis=-1, keepdims=True))
    return values_fn(e.astype(BF16)) / jnp.sum(e, axis=-1, keepdims=True)


def _slc_branch(qb, qpos, nq, sels, eexp, klen, scores_fn, values_fn):
    grow = HEADS_PER_GROUP * nq
    causal = lax.broadcasted_iota(jnp.int32, (nq, klen), 1) <= qpos
    chosen = _dot(jnp.concatenate(sels, axis=0).astype(BF16), eexp)
    outs = []
    for g in range(N_KV_HEADS):
        bias_g = jnp.where(causal, jnp.where(chosen[g * nq:(g + 1) * nq] > 0.5, 0.0, NEG_INF), NEG_INF)
        bias = jnp.concatenate([bias_g] * HEADS_PER_GROUP, axis=0)
        s = scores_fn(qb[g * grow:(g + 1) * grow])
        outs.append(_softmax_pv(s, bias, values_fn))
    return jnp.concatenate(outs, axis=0)


def _win_branch(qb, qpos, nq, k_win, v_win, wpos0):
    wpos = wpos0 + lax.broadcasted_iota(jnp.int32, (nq, k_win.shape[0]), 1)
    bias_q = jnp.where(wpos <= qpos, jnp.where(wpos > qpos - WINDOW, 0.0, NEG_INF), NEG_INF)
    bias = jnp.concatenate([bias_q] * N_Q_HEADS, axis=0)
    vb = v_win.astype(BF16)
    return _softmax_pv(_dot_nt(qb, k_win.astype(BF16)), bias, lambda pr: _dot(pr, vb))


def _combine(gates, o_cmp, o_slc, o_win, nq):
    lane = lax.broadcasted_iota(jnp.int32, (nq, LANES), 1)
    outs = []
    for h in range(N_Q_HEADS):
        r0 = h * nq
        o = (gates[:, 3 * h:3 * h + 1] * o_cmp[r0:r0 + nq]
             + gates[:, 3 * h + 1:3 * h + 2] * o_slc[r0:r0 + nq]
             + gates[:, 3 * h + 2:3 * h + 3] * o_win[r0:r0 + nq])
        g = h // HEADS_PER_GROUP
        outs.append(jnp.where((lane >= g * HEAD_DIM) & (lane < (g + 1) * HEAD_DIM), o, 0.0))
    return outs


def _attn_prompt_kernel(q_ref, rows_ref, win_ref, kc_ref, vc_ref, gate_ref, msel_ref, eexp_ref,
                        o_ref, oslc_ref, *, t, n_slc, key_step):
    qblk = pl.program_id(1)
    start = qblk * Q_BLOCK
    scale = HEAD_DIM ** -0.5
    q = jnp.concatenate([q_ref[:, h * LANES:(h + 1) * LANES] for h in range(N_Q_HEADS)], axis=0) * scale
    qb = q.astype(BF16)
    qpos = start + lax.broadcasted_iota(jnp.int32, (Q_BLOCK, 1), 0)
    o_cmp, sels = _cmp_and_select(qb, qpos, Q_BLOCK, kc_ref[0], vc_ref[0], msel_ref[...], n_slc)

    def slc_for(klen):
        ksb = rows_ref[0:klen, 2 * LANES:3 * LANES].astype(BF16)
        vsb = rows_ref[0:klen, 3 * LANES:4 * LANES].astype(BF16)
        oslc_ref[...] = _slc_branch(qb, qpos, Q_BLOCK, sels, eexp_ref[:, 0:klen], klen,
                                    lambda qg: _dot_nt(qg, ksb), lambda pr: _dot(pr, vsb))

    per = key_step // Q_BLOCK
    for v in range(t // key_step):
        pl.when(qblk // per == v)(functools.partial(slc_for, (v + 1) * key_step))

    w0 = pl.multiple_of(jnp.maximum(start - WINDOW, 0), Q_BLOCK)
    wlen = WINDOW + Q_BLOCK
    o_win = _win_branch(qb, qpos, Q_BLOCK, win_ref[pl.ds(w0, wlen), 0:LANES],
                        win_ref[pl.ds(w0, wlen), LANES:2 * LANES], w0)
    outs = _combine(gate_ref[...], o_cmp, oslc_ref[...], o_win, Q_BLOCK)
    for h in range(N_Q_HEADS):
        o_ref[:, h * LANES:(h + 1) * LANES] = outs[h]


def _attn_prompt(q, rows, win, kc, vc, gates, consts, b, t):
    nb = t // Q_BLOCK
    nch = t // CMP_STRIDE
    msel, eexp, n_slc = consts
    tile = lambda i, j: (i * nb + j, 0)
    batch = lambda i, j: (i, 0)
    return pl.pallas_call(
        functools.partial(_attn_prompt_kernel, t=t, n_slc=n_slc, key_step=SLC_KEY_STEP),
        out_shape=jax.ShapeDtypeStruct((b * t, Q_PAD), F32),
        grid=(b, nb),
        in_specs=[
            pl.BlockSpec((Q_BLOCK, Q_PAD), tile),
            pl.BlockSpec((t, 4 * LANES), batch),
            pl.BlockSpec((t, 2 * LANES), batch),
            pl.BlockSpec((1, nch, LANES), lambda i, j: (i, 0, 0)),
            pl.BlockSpec((1, nch, LANES), lambda i, j: (i, 0, 0)),
            pl.BlockSpec((Q_BLOCK, LANES), tile),
            pl.BlockSpec(msel.shape, lambda i, j: (0, 0)),
            pl.BlockSpec(eexp.shape, lambda i, j: (0, 0)),
        ],
        out_specs=pl.BlockSpec((Q_BLOCK, Q_PAD), tile),
        scratch_shapes=[pltpu.VMEM((N_Q_HEADS * Q_BLOCK, LANES), F32)],
        compiler_params=_cparams(("parallel", "parallel")),
        name="attn_prompt",
    )(q, rows, win, kc, vc, gates, msel, eexp)


def _attn_sample_kernel(pt_ref, q_ref, new_ref, nwin_ref, state_ref, gate_ref, cache_ref,
                        w1_ref, pe_ref, w2_ref, msel_ref, eexp_ref, o_ref, raw_ref, past_ref, kvt_ref, sem,
                        *, layer, n_pages, n_slc):
    b = pl.program_id(0)
    nb = pl.num_programs(0)
    past_len = n_pages * PAGE_SIZE
    cmp_roles, slc_roles = 0, 1

    def page_copy(seq, pg, pair):
        return pltpu.make_async_copy(
            cache_ref.at[pt_ref[seq, pg], pl.ds(layer * 4 + 2 * pair, 2)],
            raw_ref.at[pl.ds(2 * pair, 2), :, pl.ds(pg * PAGE_SIZE, PAGE_SIZE)],
            sem.at[pair, pg])

    def start_pages(seq, pair):
        for pg in range(n_pages):
            page_copy(seq, pg, pair).start()

    @pl.when(b == 0)
    def _():
        start_pages(b, cmp_roles)
        start_pages(b, slc_roles)

    pages_per_trip = 2 if n_pages % 2 == 0 else 1

    def land_pages(i, carry):
        for j in range(pages_per_trip):
            page_copy(b, i * pages_per_trip + j, cmp_roles).wait()
        for j in range(pages_per_trip):
            off = pl.multiple_of((i * pages_per_trip + j) * PAGE_SIZE, PAGE_SIZE)
            for role in range(2):
                past_ref[role, pl.ds(off, PAGE_SIZE), :] = raw_ref[role, :, pl.ds(off, PAGE_SIZE)].T
        return carry

    lax.fori_loop(0, n_pages // pages_per_trip, land_pages, 0)

    @pl.when(b + 1 < nb)
    def _():
        start_pages(b + 1, cmp_roles)

    nch = past_len // CMP_STRIDE
    kc = _compress(past_ref.at[0], nch, w1_ref, pe_ref, w2_ref, 0)
    vc = _compress(past_ref.at[1], nch, w1_ref, pe_ref, w2_ref, 1)

    scale = HEAD_DIM ** -0.5
    qv = q_ref[0]
    q = jnp.concatenate([qv[:, h * LANES:(h + 1) * LANES] for h in range(N_Q_HEADS)], axis=0) * scale
    qb = q.astype(BF16)
    qpos = past_len + lax.broadcasted_iota(jnp.int32, (SAMPLE_T, 1), 0)
    o_cmp, sels = _cmp_and_select(qb, qpos, SAMPLE_T, kc, vc, msel_ref[...], n_slc)

    pad = jnp.zeros((LANES - SAMPLE_T, LANES), F32)
    new = new_ref[0]
    k_new = jnp.concatenate([new[:, 2 * LANES:3 * LANES], pad], axis=0).astype(BF16)
    v_new = jnp.concatenate([new[:, 3 * LANES:4 * LANES], pad], axis=0).astype(BF16)
    for pg in range(n_pages):
        page_copy(b, pg, slc_roles).wait()
    kvt_ref[0] = raw_ref[2].astype(BF16)
    kvt_ref[1] = raw_ref[3].astype(BF16)

    @pl.when(b + 1 < nb)
    def _():
        start_pages(b + 1, slc_roles)

    kt = kvt_ref[0]
    vt = kvt_ref[1]

    o_slc = _slc_branch(
        qb, qpos, SAMPLE_T, sels, eexp_ref[...], past_len + LANES,
        lambda qg: jnp.concatenate([_dot(qg, kt), _dot_nt(qg, k_new)], axis=1),
        lambda pr: _dot_nt(pr[:, 0:past_len], vt) + _dot(pr[:, past_len:], v_new))

    nw = nwin_ref[0]
    st = state_ref[0, 0]
    k_win = jnp.concatenate([st[:, 0:LANES], nw[:, 0:LANES], pad], axis=0)
    v_win = jnp.concatenate([st[:, LANES:2 * LANES], nw[:, LANES:2 * LANES], pad], axis=0)
    o_win = _win_branch(qb, qpos, SAMPLE_T, k_win, v_win, past_len - WINDOW)
    outs = _combine(gate_ref[0], o_cmp, o_slc, o_win, SAMPLE_T)
    for h in range(N_Q_HEADS):
        o_ref[0, :, h * LANES:(h + 1) * LANES] = outs[h]


def _attn_sample(page_table, q, new_rows, new_win, state_win, gates, cache, lw, consts, layer):
    bs, n_pages = page_table.shape
    msel, eexp, n_slc = consts
    past_len = n_pages * PAGE_SIZE
    per_b = lambda i, pt: (i, 0, 0)
    grid_spec = pltpu.PrefetchScalarGridSpec(
        num_scalar_prefetch=1,
        grid=(bs,),
        in_specs=[
            pl.BlockSpec((1, SAMPLE_T, Q_PAD), per_b),
            pl.BlockSpec((1, SAMPLE_T, 4 * LANES), per_b),
            pl.BlockSpec((1, SAMPLE_T, 2 * LANES), per_b),
            pl.BlockSpec((1, 1, WINDOW, 2 * LANES), lambda i, pt: (layer, i, 0, 0)),
            pl.BlockSpec((1, SAMPLE_T, LANES), per_b),
            pl.BlockSpec(memory_space=pl.ANY),
            pl.BlockSpec(lw["cmp_w1"].shape, lambda i, pt: (0,) * 5),
            pl.BlockSpec(lw["cmp_pe"].shape, lambda i, pt: (0,) * 4),
            pl.BlockSpec(lw["cmp_w2"].shape, lambda i, pt: (0,) * 3),
            pl.BlockSpec(msel.shape, lambda i, pt: (0, 0)),
            pl.BlockSpec(eexp.shape, lambda i, pt: (0, 0)),
        ],
        out_specs=pl.BlockSpec((1, SAMPLE_T, Q_PAD), per_b),
        scratch_shapes=[
            pltpu.VMEM((4, LANES, past_len), F32),
            pltpu.VMEM((2, past_len, LANES), F32),
            pltpu.VMEM((2, LANES, past_len), BF16),
            pltpu.SemaphoreType.DMA((2, n_pages)),
        ],
    )
    return pl.pallas_call(
        functools.partial(_attn_sample_kernel, layer=layer, n_pages=n_pages, n_slc=n_slc),
        out_shape=jax.ShapeDtypeStruct((bs, SAMPLE_T, Q_PAD), F32),
        grid_spec=grid_spec,
        compiler_params=_cparams(("arbitrary",)),
        name="attn_sample",
    )(page_table, q, new_rows, new_win, state_win, gates, cache,
      lw["cmp_w1"], lw["cmp_pe"], lw["cmp_w2"], msel, eexp)


def _out_kernel(oa_ref, gb_ref, u_ref, p1_ref, p2_ref, cw_ref, x_ref, gm_ref, og_ref, w_ref, n2_ref,
                x1_ref, h2_ref):
    gmat = gm_ref[...]
    y_conv = cw_ref[0:1, :] * p2_ref[...] + cw_ref[1:2, :] * p1_ref[...] + cw_ref[2:3, :] * u_ref[...]
    o_conv = gb_ref[...] * y_conv
    parts = []
    for k in range(N_Q_HEADS):
        parts.append(_group_norm(oa_ref[:, k * LANES:(k + 1) * LANES], gmat,
                                 og_ref[:, k * LANES:(k + 1) * LANES]))
    for k in range(CONV_DIM // LANES):
        parts.append(_group_norm(o_conv[:, k * LANES:(k + 1) * LANES], gmat,
                                 og_ref[:, Q_PAD + k * LANES:Q_PAD + (k + 1) * LANES]))
    mix = jnp.concatenate(parts, axis=1).astype(BF16)
    x1 = x_ref[...] + _dot(mix, w_ref[...])
    x1_ref[...] = x1
    ms = jnp.mean(x1 * x1, axis=-1, keepdims=True)
    h2_ref[...] = (x1 * lax.rsqrt(ms + RMS_EPS) * n2_ref[...]).T.astype(BF16)


def _out_proj(o_attn, gb, u, prev1, prev2, x, lw, tm):
    n = x.shape[0]
    row = lambda i: (i, 0)
    fixed = lambda i: (0, 0)
    return pl.pallas_call(
        _out_kernel,
        out_shape=(jax.ShapeDtypeStruct((n, D_MODEL), F32), jax.ShapeDtypeStruct((D_MODEL, n), BF16)),
        grid=(n // tm,),
        in_specs=[
            pl.BlockSpec((tm, Q_PAD), row),
            pl.BlockSpec((tm, CONV_DIM), row),
            pl.BlockSpec((tm, CONV_DIM), row),
            pl.BlockSpec((tm, CONV_DIM), row),
            pl.BlockSpec((tm, CONV_DIM), row),
            pl.BlockSpec((SUBLANES, CONV_DIM), fixed),
            pl.BlockSpec((tm, D_MODEL), row),
            pl.BlockSpec((LANES, LANES), fixed),
            pl.BlockSpec((1, MIX_PAD), fixed),
            pl.BlockSpec((MIX_PAD, D_MODEL), fixed),
            pl.BlockSpec((1, D_MODEL), fixed),
        ],
        out_specs=(pl.BlockSpec((tm, D_MODEL), row), pl.BlockSpec((D_MODEL, tm), lambda i: (0, i))),
        compiler_params=_cparams(("parallel",)),
        name="out_proj",
    )(o_attn, gb, u, prev1, prev2, lw["conv_w"], x, lw["gmat"], lw["out_g"], lw["w_out"], lw["norm2"])


_CAND_ROWS = 80
PEER_EB = 2048
PEER_SUB = 256
PEER_GATE_TILES = 2
ROUTE_TOKENS = 512


def _route_kernel(h_ref, wq_ref, sk_ref, a1_ref, cnt_ref, b_ref, r2_ref,
                  qt_ref, s_ref, so_ref, rank_ref, v_ref, cand_ref, cs_ref, z_ref):
    tn = h_ref.shape[1]
    k = PEER_TOPK
    qt_ref[...] = _dot(wq_ref[...], h_ref[...])

    def twice_bf16(x):
        bits = pltpu.bitcast(x.astype(BF16).astype(F32), jnp.uint32)
        return bits | (bits >> 16)
    rowid = lax.broadcasted_iota(jnp.int32, (PEER_NKEYS, tn), 0).astype(F32)

    crow = lax.broadcasted_iota(jnp.int32, (_CAND_ROWS, tn), 0)
    cr = jnp.where(crow < 16, 0, jnp.where(crow < 72, 1 + (crow - 16) // 8, crow - 64))
    cj = jnp.where(crow < 16, crow, jnp.where(crow < 72, (crow - 16) % 8, 0))
    cvalid = (cr + 1) * (cj + 1) <= k
    cflat = (cr * k + cj).astype(F32)
    rid16 = lax.broadcasted_iota(jnp.int32, (k, tn), 0).astype(F32)
    neg = -jnp.inf

    def any_miscount(selected):
        count = jnp.sum(jnp.where(selected, 1.0, 0.0), axis=0, keepdims=True)
        return jnp.max(jnp.where(count != float(k), 1.0, 0.0))

    def reset_scores():
        for p in range(2):
            s_ref[p] = so_ref[p]
            rank_ref[p] = jnp.full((PEER_NKEYS, tn), float(k), F32)

    def topk_keys(exact):
        for r in range(k):
            for p in range(2):
                s = s_ref[p]
                m = jnp.max(s, axis=0, keepdims=True)
                if exact:
                    idx = jnp.min(jnp.where(s == m, rowid, float(PEER_NKEYS)), axis=0, keepdims=True)
                    hit = rowid == idx
                else:
                    hit = s == m
                s_ref[p] = jnp.where(hit, neg, s)
                rank_ref[p] = jnp.where(hit, float(r), rank_ref[p])
                v_ref[p, r:r + 1, :] = m

    def build_candidates():
        v1 = v_ref[0]
        v2 = v_ref[1]
        cand_ref[0:16, :] = v1[0:1] + v2
        for r in range(1, 8):
            cand_ref[8 + 8 * r:16 + 8 * r, :] = v1[r:r + 1] + v2[0:8]
        cand_ref[72:80, :] = v1[8:16] + v2[0:1]
        cand_ref[...] = jnp.where(cvalid, cand_ref[...], neg)

    def topk_pairs_quick():
        cmax = v_ref[0, 0:1, :] + v_ref[1, 0:1, :]
        zsum = jnp.zeros((1, tn), F32)
        for _ in range(k):
            c = cand_ref[...]
            m = jnp.max(c, axis=0, keepdims=True)
            cand_ref[...] = jnp.where(c == m, neg, c)
            zsum = zsum + jnp.exp(m - cmax)
        taken = cvalid & (cand_ref[...] == neg)
        tk = jnp.where(taken, 1.0, 0.0)
        cs_ref[0:1, :] = jnp.sum(tk[0:16], axis=0, keepdims=True)
        for r in range(1, 8):
            cs_ref[r:r + 1, :] = jnp.sum(tk[8 + 8 * r:16 + 8 * r], axis=0, keepdims=True)
        cs_ref[8:16, :] = tk[72:80]
        z_ref[0:1, :] = zsum
        return any_miscount(taken)

    def topk_pairs_exact():
        cmax = v_ref[0, 0:1, :] + v_ref[1, 0:1, :]
        cnt = jnp.zeros((k, tn), F32)
        zsum = jnp.zeros((1, tn), F32)
        for _ in range(k):
            c = cand_ref[...]
            m = jnp.max(c, axis=0, keepdims=True)
            f = jnp.min(jnp.where(c == m, cflat, 1e9), axis=0, keepdims=True)
            cand_ref[...] = jnp.where(cflat == f, neg, c)
            cnt = cnt + jnp.where(rid16 == jnp.floor(f * (1.0 / k)), 1.0, 0.0)
            zsum = zsum + jnp.exp(m - cmax)
        cs_ref[...] = cnt
        z_ref[0:1, :] = zsum

    def head_body(h, carry):
        for p in range(2):
            base = pl.multiple_of(h * PEER_QDIM + p * (PEER_QDIM // 2), PEER_QDIM // 2)
            so_ref[p] = _dot(sk_ref[2 * h + p], qt_ref[pl.ds(base, PEER_QDIM // 2), :].astype(BF16))
        reset_scores()
        topk_keys(exact=False)
        tied = jnp.maximum(any_miscount(rank_ref[0] < float(k)), any_miscount(rank_ref[1] < float(k)))

        @pl.when(tied > 0.5)
        def _():
            reset_scores()
            topk_keys(exact=True)

        build_candidates()
        tied_pairs = topk_pairs_quick()

        @pl.when(tied_pairs > 0.5)
        def _():
            build_candidates()
            topk_pairs_exact()

        v1 = v_ref[0]
        v2 = v_ref[1]
        cnt = cs_ref[...]
        zsum = z_ref[0:1, :]
        r1 = rank_ref[0]
        r2 = rank_ref[1]
        a1_ref[h] = twice_bf16(jnp.where(r1 < float(k), jnp.exp(so_ref[0] - v1[0:1]), 0.0))
        bval = jnp.where(r2 < float(k), jnp.exp(so_ref[1] - v2[0:1]), 0.0) / zsum
        b_ref[h] = pltpu.bitcast(bval.astype(BF16), jnp.uint32)
        r2_ref[h] = pltpu.bitcast(r2.astype(BF16), jnp.uint32)
        cdense = jnp.zeros((PEER_NKEYS, tn), F32)
        for r in range(k):
            cdense = jnp.where(r1 == float(r), cnt[r:r + 1], cdense)
        cnt_ref[h] = twice_bf16(cdense)
        return carry

    lax.fori_loop(0, PEER_HEADS, head_body, 0)


def _route(h2, lw, tn):
    n = h2.shape[1]
    shape = jax.ShapeDtypeStruct((PEER_HEADS, PEER_NKEYS, n), jnp.uint32)
    shape_b = jax.ShapeDtypeStruct((PEER_HEADS, PEER_NKEYS // 2, n), jnp.uint32)
    spec = pl.BlockSpec((PEER_HEADS, PEER_NKEYS, tn), lambda i: (0, 0, i))
    spec_b = pl.BlockSpec((PEER_HEADS, PEER_NKEYS // 2, tn), lambda i: (0, 0, i))
    return pl.pallas_call(
        _route_kernel,
        out_shape=(shape, shape, shape_b, shape_b),
        grid=(n // tn,),
        in_specs=[
            pl.BlockSpec((D_MODEL, tn), lambda i: (0, i)),
            pl.BlockSpec((PEER_HEADS * PEER_QDIM, D_MODEL), lambda i: (0, 0)),
            pl.BlockSpec((2 * PEER_HEADS, PEER_NKEYS, PEER_QDIM // 2), lambda i: (0, 0, 0)),
        ],
        out_specs=(spec, spec, spec_b, spec_b),
        scratch_shapes=[
            pltpu.VMEM((PEER_HEADS * PEER_QDIM, tn), F32),
            pltpu.VMEM((2, PEER_NKEYS, tn), F32),
            pltpu.VMEM((2, PEER_NKEYS, tn), F32),
            pltpu.VMEM((2, PEER_NKEYS, tn), F32),
            pltpu.VMEM((2, PEER_TOPK, tn), F32),
            pltpu.VMEM((_CAND_ROWS, tn), F32),
            pltpu.VMEM((PEER_TOPK, tn), F32),
            pltpu.VMEM((SUBLANES, tn), F32),
        ],
        compiler_params=_cparams(("parallel",)),
        name="peer_route",
    )(h2, lw["peer_wq_t"], lw["peer_sk"])


def _gelu_tanh(x):
    c = math.sqrt(2.0 / math.pi)
    hx = 0.5 * x
    return hx + hx * jnp.tanh(x * (c + (c * 0.044715) * (x * x)))


def _peer_kernel(h_ref, u_ref, vt_ref, a1_ref, cnt_ref, b_ref, r2_ref, x_ref, y_ref,
                 acc_ref, at_ref, *, eb, sub):
    e = pl.program_id(1)
    tn = h_ref.shape[1]
    pack = 2 * SUBLANES

    @pl.when(e == 0)
    def _():
        acc_ref[...] = jnp.zeros_like(acc_ref)

    h = h_ref[...]
    zero = jnp.zeros((pack, LANES), BF16)

    def row_pair(ref, hh, ai, cols):
        return pltpu.bitcast(jnp.broadcast_to(ref[hh, ai:ai + 1, cols], (SUBLANES, LANES)), BF16)

    def sub_block(sb):
        r0 = sb * sub
        st = _dot(u_ref[r0:r0 + sub, :], h)
        n_a = PEER_GATE_TILES
        n_k = PEER_NKEYS // pack
        for a0 in range(0, sub // PEER_NKEYS, n_a):
            ai0 = r0 // PEER_NKEYS + a0
            for c in range(tn // LANES):
                cols = slice(c * LANES, (c + 1) * LANES)
                w = [[None] * n_k for _ in range(n_a)]
                for hh in range(PEER_HEADS):
                    cnt = [row_pair(cnt_ref, hh, ai0 + a, cols) for a in range(n_a)]
                    a1 = [row_pair(a1_ref, hh, ai0 + a, cols) for a in range(n_a)]
                    for k in range(n_k):
                        rk = slice(k * SUBLANES, (k + 1) * SUBLANES)
                        r2 = pltpu.bitcast(r2_ref[hh, rk, cols], BF16)
                        bb = pltpu.bitcast(b_ref[hh, rk, cols], BF16)
                        for a in range(n_a):
                            t = a1[a] * jnp.where(r2 < cnt[a], bb, zero)
                            w[a][k] = t if w[a][k] is None else w[a][k] + t
                for a in range(n_a):
                    lo = (a0 + a) * PEER_NKEYS
                    g = _gelu_tanh(st[lo:lo + PEER_NKEYS, cols].astype(BF16))
                    at_ref[r0 + lo:r0 + lo + PEER_NKEYS, cols] = jnp.concatenate(w[a], axis=0) * g

    n_sb = eb // sub
    half = eb // 2
    tot = None
    for sb in range(n_sb):
        sub_block(sb)
        if (sb + 1) * sub % half == 0:
            k0 = (sb + 1) * sub - half
            d = lax.dot_general(vt_ref[k0:k0 + half, :], at_ref[k0:k0 + half, :],
                                (((0,), (0,)), ((), ())), preferred_element_type=F32)
            tot = d if tot is None else tot + d
    acc_ref[...] += tot

    @pl.when(e == pl.num_programs(1) - 1)
    def _():
        y_ref[...] = x_ref[...] + acc_ref[...].T


def _peer(h2, routed, x1, peer_u, peer_v, layer, tn, eb):
    n = h2.shape[1]
    n_exp = PEER_NKEYS * PEER_NKEYS
    a1, cnt, bb, r2 = routed
    tok = lambda i, e: (i, 0)
    key_rows = pl.BlockSpec((PEER_HEADS, eb // PEER_NKEYS, tn), lambda i, e: (0, e, i))
    key_full = pl.BlockSpec((PEER_HEADS, PEER_NKEYS // 2, tn), lambda i, e: (0, 0, i))
    return pl.pallas_call(
        functools.partial(_peer_kernel, eb=eb, sub=PEER_SUB),
        out_shape=jax.ShapeDtypeStruct((n, D_MODEL), F32),
        grid=(n // tn, n_exp // eb),
        in_specs=[
            pl.BlockSpec((D_MODEL, tn), lambda i, e: (0, i)),
            pl.BlockSpec((None, eb, D_MODEL), lambda i, e: (layer, e, 0)),
            pl.BlockSpec((None, eb, D_MODEL), lambda i, e: (layer, e, 0)),
            key_rows, key_rows, key_full, key_full,
            pl.BlockSpec((tn, D_MODEL), tok),
        ],
        out_specs=pl.BlockSpec((tn, D_MODEL), tok),
        scratch_shapes=[pltpu.VMEM((D_MODEL, tn), F32), pltpu.VMEM((eb, tn), BF16)],
        compiler_params=_cparams(("parallel", "arbitrary")),
        name="peer_experts",
    )(h2, peer_u, peer_v, a1, cnt, bb, r2, x1)


def _rope_tables(pos):
    inv_freq = ROPE_THETA ** (-jnp.arange(ROPE_HALF, dtype=F32) / ROPE_HALF)
    ang = pos.astype(F32)[:, None] * inv_freq[None, :]
    cos, sin = jnp.cos(ang), jnp.sin(ang)
    n = pos.shape[0]
    ones = jnp.ones((n, HEAD_DIM - ROPE_DIM), F32)
    zeros = jnp.zeros((n, HEAD_DIM - ROPE_DIM), F32)
    zh = jnp.zeros((n, ROPE_HALF), F32)
    c = jnp.concatenate([cos, cos, ones], axis=1)
    s_up = jnp.concatenate([-sin, zh, zeros], axis=1)
    s_dn = jnp.concatenate([zh, sin, zeros], axis=1)
    return tuple(jnp.tile(a, (1, 2)) for a in (c, s_up, s_dn))


def _attn_consts(n_cmp, nch, n_slc, slc_len):
    jn = -(-n_slc // LANES) * LANES
    sr = SLC_BLOCK // CMP_STRIDE
    r = CMP_LEN // CMP_STRIDE
    msel = np.zeros((nch, jn), np.float32)
    for j in range(n_slc):
        for m in range(sr):
            for n in range(r):
                i = sr * j + m - n
                if 0 <= i < n_cmp:
                    msel[i, j] += 1.0
    eexp = np.zeros((jn, slc_len), np.float32)
    keys = np.arange(slc_len)
    eexp[keys // SLC_BLOCK, keys] = 1.0
    return jnp.asarray(msel, BF16), jnp.asarray(eexp, BF16), n_slc


def _pad_heads(x, axis):
    shape = x.shape
    x = jnp.moveaxis(x, axis, -1).reshape(shape[:axis] + shape[axis + 1:] + (N_KV_HEADS, HEADS_PER_GROUP, HEAD_DIM))
    nd = x.ndim
    lo = jnp.pad(x[..., 0, :, :], [(0, 0)] * (nd - 2) + [(0, HEAD_DIM)])
    hi = jnp.pad(x[..., 1, :, :], [(0, 0)] * (nd - 2) + [(HEAD_DIM, 0)])
    out = jnp.concatenate([lo, hi], axis=-2)
    out = out.reshape(out.shape[:-2] + (Q_PAD,))
    return jnp.moveaxis(out, -1, axis)


def _prep_weights(norm1_g, w_in, q_norm_g, k_norm_g, cmp_pe, cmp_w1, cmp_w2, conv_w, out_norm_g,
                  w_out, norm2_g, peer_wq, peer_subkeys, peer_u, peer_v):
    depth = w_in.shape[0]
    c1 = NSA_WIDTH
    c2 = c1 + 6 * KV_WIDTH
    c3 = c2 + N_Q_HEADS * N_BRANCH
    gl_pad = jnp.zeros((depth, D_MODEL, LANES - N_Q_HEADS * N_BRANCH), F32)
    w_all = jnp.concatenate([_pad_heads(w_in[:, :, :c1], 2), w_in[:, :, c1:c2], w_in[:, :, c3:],
                             w_in[:, :, c2:c3], gl_pad], axis=2)

    blk = np.kron(np.eye(2, dtype=np.float32), np.full((HEAD_DIM, HEAD_DIM), 1.0 / HEAD_DIM, np.float32))
    kg = jnp.pad(jnp.tile(k_norm_g, (1, 1, 2)), ((0, 0), (0, SUBLANES - N_BRANCH), (0, 0)))

    r = CMP_LEN // CMP_STRIDE
    eye2 = jnp.eye(2, dtype=F32)
    w1 = cmp_w1.reshape(depth, 2, r, CMP_STRIDE, HEAD_DIM, HEAD_DIM)
    w1 = jnp.einsum('lxmsde,gk->lxmsgdke', w1, eye2).reshape(depth, 2, r, CMP_STRIDE // 2, 2 * LANES, LANES)
    pe = jnp.tile(cmp_pe.reshape(depth, 2, r, CMP_STRIDE, HEAD_DIM), (1, 1, 1, 1, 2))
    w2 = jnp.einsum('lxde,gk->lxgdke', cmp_w2, eye2).reshape(depth, 2, LANES, LANES)

    og_pad = jnp.concatenate([_pad_heads(out_norm_g[:, :NSA_WIDTH], 1), out_norm_g[:, NSA_WIDTH:]], axis=1)
    wo_pad = jnp.concatenate([_pad_heads(w_out[:, :NSA_WIDTH], 1), w_out[:, NSA_WIDTH:]], axis=1)
    return {
        "norm1": norm1_g[:, None, :],
        "w_in": w_all.astype(BF16),
        "gmat": jnp.broadcast_to(jnp.asarray(blk, BF16), (depth, LANES, LANES)),
        "q_g": jnp.tile(q_norm_g, (1, 2))[:, None, :],
        "k_g": kg,
        "cmp_w1": w1.astype(BF16),
        "cmp_pe": pe,
        "cmp_w2": w2.astype(BF16),
        "conv_w": jnp.pad(conv_w, ((0, 0), (0, SUBLANES - CONV_WIDTH), (0, 0))),
        "out_g": og_pad[:, None, :],
        "w_out": wo_pad.astype(BF16),
        "norm2": norm2_g[:, None, :],
        "peer_wq_t": jnp.swapaxes(peer_wq, 1, 2).astype(BF16),
        "peer_sk": peer_subkeys.reshape(depth, 2 * PEER_HEADS, PEER_NKEYS, PEER_QDIM // 2).astype(BF16),
        "peer_u": peer_u.astype(BF16),
        "peer_vt": peer_v.astype(BF16),
    }


def _conv_prev(u, prefix, b, t):
    ext = jnp.concatenate([prefix, u.reshape(b, t, CONV_DIM)], axis=1)
    return ext[:, 1:t + 1].reshape(b * t, CONV_DIM), ext[:, 0:t].reshape(b * t, CONV_DIM)


def _row_tile(n):
    for tm in (512, 256, 128):
        if n % tm == 0:
            return tm
    raise ValueError(f"token count {n} must be a multiple of 128")


def _route_tile(n):
    for tn in (ROUTE_TOKENS, ROUTE_TOKENS // 2, LANES):
        if n % tn == 0:
            return tn
    raise ValueError(f"token count {n} must be a multiple of 128")


def _peer_tile(n):
    for tn in (512, 256, 128):
        if n % tn == 0:
            return tn
    raise ValueError(f"token count {n} must be a multiple of 128")


def kernel(x_prompt, x_sample, cache_nsa_kv, state_win_kv, state_conv, page_table,
           norm1_g, w_in, q_norm_g, k_norm_g, cmp_pe, cmp_w1, cmp_w2, conv_w, out_norm_g, w_out,
           norm2_g, peer_wq, peer_subkeys, peer_u, peer_v):
    bp, t, _ = x_prompt.shape
    bs, ts, _ = x_sample.shape
    depth = norm1_g.shape[0]
    n_pages = page_table.shape[1]
    past_len = n_pages * PAGE_SIZE
    assert t % SLC_KEY_STEP == 0 and t >= WINDOW + Q_BLOCK and ts <= SAMPLE_T
    assert state_win_kv.shape[2] == WINDOW and past_len % SLC_BLOCK == 0

    xp = x_prompt.reshape(bp * t, D_MODEL)
    xs = jnp.pad(x_sample, ((0, 0), (0, SAMPLE_T - ts), (0, 0))).reshape(bs * SAMPLE_T, D_MODEL)
    cache = jnp.transpose(cache_nsa_kv, (0, 2, 3, 4, 5, 1)).reshape(
        cache_nsa_kv.shape[0], depth * 4, LANES, PAGE_SIZE)
    state_win = state_win_kv.reshape(depth, bs, WINDOW, 2 * LANES)

    tabs_p = _rope_tables(jnp.arange(t))
    tabs_s = _rope_tables(jnp.tile(past_len + jnp.arange(SAMPLE_T), bs))
    n_cmp_p = (t - CMP_LEN) // CMP_STRIDE + 1
    consts_p = _attn_consts(n_cmp_p, t // CMP_STRIDE, t // SLC_BLOCK, t)
    t_all = past_len + ts
    n_cmp_s = (t_all - CMP_LEN) // CMP_STRIDE + 1
    assert n_cmp_s <= past_len // CMP_STRIDE
    consts_s = _attn_consts(n_cmp_s, past_len // CMP_STRIDE, -(-t_all // SLC_BLOCK), past_len + LANES)
    zero_prefix = jnp.zeros((bp, CONV_WIDTH - 1, CONV_DIM), F32)

    weights = _prep_weights(norm1_g, w_in, q_norm_g, k_norm_g, cmp_pe, cmp_w1, cmp_w2, conv_w, out_norm_g,
                            w_out, norm2_g, peer_wq, peer_subkeys, peer_u, peer_v)
    peer_u_b = weights.pop("peer_u")
    peer_v_b = weights.pop("peer_vt")
    rows_p, rows_s, win_p, win_s, conv_p, conv_s = [], [], [], [], [], []
    for l in range(depth):
        lw = {name: stacked[l] for name, stacked in weights.items()}
        q, rows, win, gates, gb, u = _in_proj(xp, lw, tabs_p, _row_tile(bp * t))
        kc, vc = _compress_prompt(rows, lw, bp, t)
        o_attn = _attn_prompt(q, rows, win, kc, vc, gates, consts_p, bp, t)
        p1, p2 = _conv_prev(u, zero_prefix, bp, t)
        x1, h2 = _out_proj(o_attn, gb, u, p1, p2, xp, lw, _row_tile(bp * t))
        routed = _route(h2, lw, _route_tile(h2.shape[1]))
        xp = _peer(h2, routed, x1, peer_u_b, peer_v_b, l, _peer_tile(bp * t), PEER_EB)
        rows_p.append(rows.reshape(bp, t, 4, N_KV_HEADS, HEAD_DIM))
        win_p.append(win.reshape(bp, t, 2, N_KV_HEADS, HEAD_DIM)[:, t - WINDOW:])
        conv_p.append(u.reshape(bp, t, CONV_DIM)[:, t - (CONV_WIDTH - 1):])

        q, rows, win, gates, gb, u = _in_proj(xs, lw, tabs_s, _row_tile(bs * SAMPLE_T))
        o_attn = _attn_sample(
            page_table, q.reshape(bs, SAMPLE_T, Q_PAD), rows.reshape(bs, SAMPLE_T, 4 * LANES),
            win.reshape(bs, SAMPLE_T, 2 * LANES), state_win, gates.reshape(bs, SAMPLE_T, LANES),
            cache, lw, consts_s, l).reshape(bs * SAMPLE_T, Q_PAD)
        p1, p2 = _conv_prev(u, state_conv[l], bs, SAMPLE_T)
        x1, h2 = _out_proj(o_attn, gb, u, p1, p2, xs, lw, _row_tile(bs * SAMPLE_T))
        routed = _route(h2, lw, _route_tile(h2.shape[1]))
        xs = _peer(h2, routed, x1, peer_u_b, peer_v_b, l, _peer_tile(bs * SAMPLE_T), PEER_EB)
        rows_s.append(rows.reshape(bs, SAMPLE_T, 4, N_KV_HEADS, HEAD_DIM)[:, :ts])
        new_win = win.reshape(bs, SAMPLE_T, 2, N_KV_HEADS, HEAD_DIM)[:, :ts]
        old_win = state_win_kv[l]
        win_s.append(jnp.concatenate([old_win, new_win], axis=1)[:, ts:])
        conv_s.append(u.reshape(bs, SAMPLE_T, CONV_DIM)[:, ts - (CONV_WIDTH - 1):ts])

    y_prompt = xp.reshape(bp, t, D_MODEL)
    y_sample = xs.reshape(bs, SAMPLE_T, D_MODEL)[:, :ts]
    return (y_prompt, y_sample,
            jnp.stack(rows_p, axis=2), jnp.stack(rows_s, axis=2),
            jnp.stack(win_p, axis=0), jnp.stack(win_s, axis=0),
            jnp.stack(conv_p, axis=0), jnp.stack(conv_s, axis=0))
```

```python
import functools
import math

import numpy as np
import jax
import jax.numpy as jnp
from jax import lax
from jax.experimental import pallas as pl
from jax.experimental.pallas import tpu as pltpu

F32 = jnp.float32
BF16 = jnp.bfloat16

LANES = 128
SUBLANES = 8
VMEM_LIMIT = 56 * 1024 * 1024

D_MODEL = 1024
HEAD_DIM = 64
N_Q_HEADS = 8
N_KV_HEADS = 2
HEADS_PER_GROUP = N_Q_HEADS // N_KV_HEADS
NSA_WIDTH = N_Q_HEADS * HEAD_DIM
KV_WIDTH = N_KV_HEADS * HEAD_DIM
CONV_DIM = D_MODEL - NSA_WIDTH
CONV_WIDTH = 3
N_BRANCH = 3
ROPE_DIM = HEAD_DIM // 4
ROPE_HALF = ROPE_DIM // 2
ROPE_THETA = 500000.0
CMP_LEN = 32
CMP_STRIDE = 16
SLC_BLOCK = 64
N_SELECT = 16
WINDOW = 512
Q_BLOCK = 128
PAGE_SIZE = 128
PEER_HEADS = 8
PEER_QDIM = 256
PEER_NKEYS = 128
PEER_TOPK = 16
RMS_EPS = 1e-6
NEG_INF = -1e30
FORCE_BONUS = 1e4

Q_PAD = N_Q_HEADS * LANES
IN_COLS = Q_PAD + 6 * KV_WIDTH + 3 * CONV_DIM + LANES
MIX_PAD = Q_PAD + CONV_DIM
SAMPLE_T = 8
SLC_KEY_STEP = 256

_NT = (((1,), (1,)), ((), ()))


def _cparams(sem):
    return pltpu.CompilerParams(dimension_semantics=sem, vmem_limit_bytes=VMEM_LIMIT)


def _dot(a, b):
    return jnp.dot(a, b, preferred_element_type=F32)


def _dot_nt(a, b):
    return lax.dot_general(a, b, _NT, preferred_element_type=F32)


def _group_norm(x, gmat, gain):
    ss = _dot((x * x).astype(BF16), gmat)
    return x * lax.rsqrt(ss + RMS_EPS) * gain


def _in_kernel(x_ref, g1_ref, w_ref, gm_ref, qg_ref, kg_ref, c_ref, s1_ref, s2_ref,
               q_ref, rows_ref, win_ref, gate_ref, gb_ref, u_ref):
    x = x_ref[...]
    ms = jnp.mean(x * x, axis=-1, keepdims=True)
    h = (x * lax.rsqrt(ms + RMS_EPS) * g1_ref[...]).astype(BF16)
    z = _dot(h, w_ref[...])
    gmat = gm_ref[...]
    cos = c_ref[...]
    sin_up = s1_ref[...]
    sin_dn = s2_ref[...]

    def norm_rope(zb, gain):
        y = _group_norm(zb, gmat, gain)
        return (y * cos + pltpu.roll(y, LANES - ROPE_HALF, 1) * sin_up
                + pltpu.roll(y, ROPE_HALF, 1) * sin_dn)

    for hb in range(N_Q_HEADS):
        q_ref[:, hb * LANES:(hb + 1) * LANES] = norm_rope(z[:, hb * LANES:(hb + 1) * LANES], qg_ref[...])
    o = Q_PAD
    kv = [z[:, o + r * LANES:o + (r + 1) * LANES] for r in range(6)]
    rows_ref[:, 0 * LANES:1 * LANES] = norm_rope(kv[0], kg_ref[0:1, :])
    rows_ref[:, 1 * LANES:2 * LANES] = kv[1]
    rows_ref[:, 2 * LANES:3 * LANES] = norm_rope(kv[2], kg_ref[1:2, :])
    rows_ref[:, 3 * LANES:4 * LANES] = kv[3]
    win_ref[:, 0:LANES] = norm_rope(kv[4], kg_ref[2:3, :])
    win_ref[:, LANES:2 * LANES] = kv[5]
    o += 6 * LANES
    gb_ref[...] = z[:, o:o + CONV_DIM]
    u_ref[...] = z[:, o + CONV_DIM:o + 2 * CONV_DIM] * z[:, o + 2 * CONV_DIM:o + 3 * CONV_DIM]
    o += 3 * CONV_DIM
    gate_ref[...] = jax.nn.sigmoid(z[:, o:o + LANES])


def _in_proj(x, lw, tabs, tm):
    n = x.shape[0]
    cos, sin_up, sin_dn = tabs
    npos_tiles = cos.shape[0] // tm
    row = lambda i: (i, 0)
    fixed = lambda i: (0, 0)
    pos = lambda i: (i % npos_tiles, 0)
    out_shape = (
        jax.ShapeDtypeStruct((n, Q_PAD), F32),
        jax.ShapeDtypeStruct((n, 4 * LANES), F32),
        jax.ShapeDtypeStruct((n, 2 * LANES), F32),
        jax.ShapeDtypeStruct((n, LANES), F32),
        jax.ShapeDtypeStruct((n, CONV_DIM), F32),
        jax.ShapeDtypeStruct((n, CONV_DIM), F32),
    )
    return pl.pallas_call(
        _in_kernel,
        out_shape=out_shape,
        grid=(n // tm,),
        in_specs=[
            pl.BlockSpec((tm, D_MODEL), row),
            pl.BlockSpec((1, D_MODEL), fixed),
            pl.BlockSpec((D_MODEL, IN_COLS), fixed),
            pl.BlockSpec((LANES, LANES), fixed),
            pl.BlockSpec((1, LANES), fixed),
            pl.BlockSpec((SUBLANES, LANES), fixed),
            pl.BlockSpec((tm, LANES), pos),
            pl.BlockSpec((tm, LANES), pos),
            pl.BlockSpec((tm, LANES), pos),
        ],
        out_specs=(
            pl.BlockSpec((tm, Q_PAD), row),
            pl.BlockSpec((tm, 4 * LANES), row),
            pl.BlockSpec((tm, 2 * LANES), row),
            pl.BlockSpec((tm, LANES), row),
            pl.BlockSpec((tm, CONV_DIM), row),
            pl.BlockSpec((tm, CONV_DIM), row),
        ),
        compiler_params=_cparams(("parallel",)),
        name="in_proj",
    )(x, lw["norm1"], lw["w_in"], lw["gmat"], lw["q_g"], lw["k_g"], cos, sin_up, sin_dn)


def _compress(src_ref, nch, w1_ref, pe_ref, w2_ref, role):
    acc0 = jnp.zeros((nch, LANES), F32)
    acc1 = jnp.zeros((nch, LANES), F32)
    for s2 in range(CMP_STRIDE // 2):
        xs = []
        for s in (2 * s2, 2 * s2 + 1):
            xs.append(src_ref[pl.ds(s, nch, stride=CMP_STRIDE), :])
        x0 = jnp.concatenate([xs[0] + pe_ref[role, 0, 2 * s2:2 * s2 + 1, :],
                              xs[1] + pe_ref[role, 0, 2 * s2 + 1:2 * s2 + 2, :]], axis=1)
        x1 = jnp.concatenate([xs[0] + pe_ref[role, 1, 2 * s2:2 * s2 + 1, :],
                              xs[1] + pe_ref[role, 1, 2 * s2 + 1:2 * s2 + 2, :]], axis=1)
        acc0 = acc0 + _dot(x0.astype(BF16), w1_ref[role, 0, s2])
        acc1 = acc1 + _dot(x1.astype(BF16), w1_ref[role, 1, s2])
    hid = acc0 + pltpu.roll(acc1, nch - 1, 0)
    return _dot(jax.nn.gelu(hid).astype(BF16), w2_ref[role])


def _cmp_kernel(k_ref, v_ref, w1_ref, pe_ref, w2_ref, kc_ref, vc_ref, *, nch):
    kc_ref[0] = _compress(k_ref, nch, w1_ref, pe_ref, w2_ref, 0)
    vc_ref[0] = _compress(v_ref, nch, w1_ref, pe_ref, w2_ref, 1)


def _compress_prompt(rows, lw, b, t):
    nch = t // CMP_STRIDE
    fixed = lambda i: (0,) * 5
    return pl.pallas_call(
        functools.partial(_cmp_kernel, nch=nch),
        out_shape=(jax.ShapeDtypeStruct((b, nch, LANES), F32),) * 2,
        grid=(b,),
        in_specs=[
            pl.BlockSpec((t, LANES), lambda i: (i, 0)),
            pl.BlockSpec((t, LANES), lambda i: (i, 1)),
            pl.BlockSpec(lw["cmp_w1"].shape, fixed),
            pl.BlockSpec(lw["cmp_pe"].shape, lambda i: (0,) * 4),
            pl.BlockSpec(lw["cmp_w2"].shape, lambda i: (0,) * 3),
        ],
        out_specs=(pl.BlockSpec((1, nch, LANES), lambda i: (i, 0, 0)),) * 2,
        compiler_params=_cparams(("parallel",)),
        name="compress_prompt",
    )(rows, rows, lw["cmp_w1"], lw["cmp_pe"], lw["cmp_w2"])


def _masked_softmax(s, mask):
    s = jnp.where(mask, s, NEG_INF)
    p = jnp.where(mask, jnp.exp(s - jnp.max(s, axis=-1, keepdims=True)), 0.0)
    return p / jnp.maximum(jnp.sum(p, axis=-1, keepdims=True), 1e-30)


def _split3(x):
    h1 = x.astype(BF16)
    r1 = x - h1.astype(F32)
    h2 = r1.astype(BF16)
    h3 = (r1 - h2.astype(F32)).astype(BF16)
    return h1, h2, h3


def _cmp_and_select(qb, qpos, nq, kc, vc, msel, n_slc):
    nrow = N_Q_HEADS * nq
    grow = HEADS_PER_GROUP * nq
    qpos_rows = jnp.concatenate([qpos] * N_Q_HEADS, axis=0)
    nch = kc.shape[0]
    s = _dot_nt(qb, kc.astype(BF16))
    cend = lax.broadcasted_iota(jnp.int32, (nrow, nch), 1) * CMP_STRIDE + (CMP_LEN - 1)
    p = _masked_softmax(s, cend <= qpos_rows)
    o_cmp = _dot(p.astype(BF16), vc.astype(BF16))

    jn = msel.shape[1]
    jid = lax.broadcasted_iota(jnp.int32, (nq, jn), 1)
    cur = qpos // SLC_BLOCK
    valid = jid * SLC_BLOCK <= qpos
    forced = (jid == 0) | (jid == cur) | (jid == cur - 1)
    sels = []
    for g in range(N_KV_HEADS):
        pg = p[g * grow:g * grow + nq]
        for hh in range(1, HEADS_PER_GROUP):
            pg = pg + p[g * grow + hh * nq:g * grow + (hh + 1) * nq]
        p_slc = sum(_dot(part, msel) for part in _split3(pg))
        score = jnp.where(valid, p_slc + jnp.where(forced, FORCE_BONUS, 0.0), NEG_INF)
        score = jnp.where(jid < n_slc, score, -3e38)
        k_sel = float(min(N_SELECT, n_slc))
        if nq == LANES and jn == LANES:
            nr = -(-n_slc // SUBLANES) * SUBLANES
            st = score.T[0:nr]
            rid = lax.broadcasted_iota(jnp.int32, (nr, nq), 0)
            rank = jnp.zeros((nr, nq), F32)
            for i in range(n_slc):
                row = st[i:i + 1, :]
                tie = jnp.where(rid > i, 1.0, 0.0)
                rank = rank + jnp.where(row > st, 1.0, jnp.where(row == st, tie, 0.0))
            chosen = jnp.where(rank < k_sel, 1.0, 0.0)
            sels.append(jnp.concatenate([chosen, jnp.zeros((jn - nr, nq), F32)], axis=0).T)
        else:
            rank = jnp.zeros((nq, jn), F32)
            for i in range(n_slc):
                col = score[:, i:i + 1]
                tie = jnp.where(jid > i, 1.0, 0.0)
                rank = rank + jnp.where(col > score, 1.0, jnp.where(col == score, tie, 0.0))
            sels.append(jnp.where(rank < k_sel, 1.0, 0.0))
    return o_cmp, sels


def _softmax_pv(s, bias, values_fn):
    s = s + bias
    e = jnp.exp(s - jnp.max(s, axis=-1, keepdims=True))
    return values_fn(e.astype(BF16)) / jnp.sum(e, axis=-1, keepdims=True)


def _slc_branch(qb, qpos, nq, sels, eexp, klen, scores_fn, values_fn):
    grow = HEADS_PER_GROUP * nq
    causal = lax.broadcasted_iota(jnp.int32, (nq, klen), 1) <= qpos
    chosen = _dot(jnp.concatenate(sels, axis=0).astype(BF16), eexp)
    outs = []
    for g in range(N_KV_HEADS):
        bias_g = jnp.where(causal, jnp.where(chosen[g * nq:(g + 1) * nq] > 0.5, 0.0, NEG_INF), NEG_INF)
        bias = jnp.concatenate([bias_g] * HEADS_PER_GROUP, axis=0)
        s = scores_fn(qb[g * grow:(g + 1) * grow])
        outs.append(_softmax_pv(s, bias, values_fn))
    return jnp.concatenate(outs, axis=0)


def _win_branch(qb, qpos, nq, k_win, v_win, wpos0):
    wpos = wpos0 + lax.broadcasted_iota(jnp.int32, (nq, k_win.shape[0]), 1)
    bias_q = jnp.where(wpos <= qpos, jnp.where(wpos > qpos - WINDOW, 0.0, NEG_INF), NEG_INF)
    bias = jnp.concatenate([bias_q] * N_Q_HEADS, axis=0)
    vb = v_win.astype(BF16)
    return _softmax_pv(_dot_nt(qb, k_win.astype(BF16)), bias, lambda pr: _dot(pr, vb))


def _combine(gates, o_cmp, o_slc, o_win, nq):
    lane = lax.broadcasted_iota(jnp.int32, (nq, LANES), 1)
    outs = []
    for h in range(N_Q_HEADS):
        r0 = h * nq
        o = (gates[:, 3 * h:3 * h + 1] * o_cmp[r0:r0 + nq]
             + gates[:, 3 * h + 1:3 * h + 2] * o_slc[r0:r0 + nq]
             + gates[:, 3 * h + 2:3 * h + 3] * o_win[r0:r0 + nq])
        g = h // HEADS_PER_GROUP
        outs.append(jnp.where((lane >= g * HEAD_DIM) & (lane < (g + 1) * HEAD_DIM), o, 0.0))
    return outs


def _attn_prompt_kernel(q_ref, rows_ref, win_ref, kc_ref, vc_ref, gate_ref, msel_ref, eexp_ref,
                        o_ref, oslc_ref, *, t, n_slc, key_step):
    qblk = pl.program_id(1)
    start = qblk * Q_BLOCK
    scale = HEAD_DIM ** -0.5
    q = jnp.concatenate([q_ref[:, h * LANES:(h + 1) * LANES] for h in range(N_Q_HEADS)], axis=0) * scale
    qb = q.astype(BF16)
    qpos = start + lax.broadcasted_iota(jnp.int32, (Q_BLOCK, 1), 0)
    o_cmp, sels = _cmp_and_select(qb, qpos, Q_BLOCK, kc_ref[0], vc_ref[0], msel_ref[...], n_slc)

    def slc_for(klen):
        ksb = rows_ref[0:klen, 2 * LANES:3 * LANES].astype(BF16)
        vsb = rows_ref[0:klen, 3 * LANES:4 * LANES].astype(BF16)
        oslc_ref[...] = _slc_branch(qb, qpos, Q_BLOCK, sels, eexp_ref[:, 0:klen], klen,
                                    lambda qg: _dot_nt(qg, ksb), lambda pr: _dot(pr, vsb))

    per = key_step // Q_BLOCK
    for v in range(t // key_step):
        pl.when(qblk // per == v)(functools.partial(slc_for, (v + 1) * key_step))

    w0 = pl.multiple_of(jnp.maximum(start - WINDOW, 0), Q_BLOCK)
    wlen = WINDOW + Q_BLOCK
    o_win = _win_branch(qb, qpos, Q_BLOCK, win_ref[pl.ds(w0, wlen), 0:LANES],
                        win_ref[pl.ds(w0, wlen), LANES:2 * LANES], w0)
    outs = _combine(gate_ref[...], o_cmp, oslc_ref[...], o_win, Q_BLOCK)
    for h in range(N_Q_HEADS):
        o_ref[:, h * LANES:(h + 1) * LANES] = outs[h]


def _attn_prompt(q, rows, win, kc, vc, gates, consts, b, t):
    nb = t // Q_BLOCK
    nch = t // CMP_STRIDE
    msel, eexp, n_slc = consts
    tile = lambda i, j: (i * nb + j, 0)
    batch = lambda i, j: (i, 0)
    return pl.pallas_call(
        functools.partial(_attn_prompt_kernel, t=t, n_slc=n_slc, key_step=SLC_KEY_STEP),
        out_shape=jax.ShapeDtypeStruct((b * t, Q_PAD), F32),
        grid=(b, nb),
        in_specs=[
            pl.BlockSpec((Q_BLOCK, Q_PAD), tile),
            pl.BlockSpec((t, 4 * LANES), batch),
            pl.BlockSpec((t, 2 * LANES), batch),
            pl.BlockSpec((1, nch, LANES), lambda i, j: (i, 0, 0)),
            pl.BlockSpec((1, nch, LANES), lambda i, j: (i, 0, 0)),
            pl.BlockSpec((Q_BLOCK, LANES), tile),
            pl.BlockSpec(msel.shape, lambda i, j: (0, 0)),
            pl.BlockSpec(eexp.shape, lambda i, j: (0, 0)),
        ],
        out_specs=pl.BlockSpec((Q_BLOCK, Q_PAD), tile),
        scratch_shapes=[pltpu.VMEM((N_Q_HEADS * Q_BLOCK, LANES), F32)],
        compiler_params=_cparams(("parallel", "parallel")),
        name="attn_prompt",
    )(q, rows, win, kc, vc, gates, msel, eexp)


def _attn_sample_kernel(pt_ref, q_ref, new_ref, nwin_ref, state_ref, gate_ref, cache_ref,
                        w1_ref, pe_ref, w2_ref, msel_ref, eexp_ref, o_ref, raw_ref, past_ref, kvt_ref, sem,
                        *, layer, n_pages, n_slc):
    b = pl.program_id(0)
    nb = pl.num_programs(0)
    past_len = n_pages * PAGE_SIZE
    cmp_roles, slc_roles = 0, 1

    def page_copy(seq, pg, pair):
        return pltpu.make_async_copy(
            cache_ref.at[pt_ref[seq, pg], pl.ds(layer * 4 + 2 * pair, 2)],
            raw_ref.at[pl.ds(2 * pair, 2), :, pl.ds(pg * PAGE_SIZE, PAGE_SIZE)],
            sem.at[pair, pg])

    def start_pages(seq, pair):
        for pg in range(n_pages):
            page_copy(seq, pg, pair).start()

    @pl.when(b == 0)
    def _():
        start_pages(b, cmp_roles)
        start_pages(b, slc_roles)

    pages_per_trip = 2 if n_pages % 2 == 0 else 1

    def land_pages(i, carry):
        for j in range(pages_per_trip):
            page_copy(b, i * pages_per_trip + j, cmp_roles).wait()
        for j in range(pages_per_trip):
            off = pl.multiple_of((i * pages_per_trip + j) * PAGE_SIZE, PAGE_SIZE)
            for role in range(2):
                past_ref[role, pl.ds(off, PAGE_SIZE), :] = raw_ref[role, :, pl.ds(off, PAGE_SIZE)].T
        return carry

    lax.fori_loop(0, n_pages // pages_per_trip, land_pages, 0)

    @pl.when(b + 1 < nb)
    def _():
        start_pages(b + 1, cmp_roles)

    nch = past_len // CMP_STRIDE
    kc = _compress(past_ref.at[0], nch, w1_ref, pe_ref, w2_ref, 0)
    vc = _compress(past_ref.at[1], nch, w1_ref, pe_ref, w2_ref, 1)

    scale = HEAD_DIM ** -0.5
    qv = q_ref[0]
    q = jnp.concatenate([qv[:, h * LANES:(h + 1) * LANES] for h in range(N_Q_HEADS)], axis=0) * scale
    qb = q.astype(BF16)
    qpos = past_len + lax.broadcasted_iota(jnp.int32, (SAMPLE_T, 1), 0)
    o_cmp, sels = _cmp_and_select(qb, qpos, SAMPLE_T, kc, vc, msel_ref[...], n_slc)

    pad = jnp.zeros((LANES - SAMPLE_T, LANES), F32)
    new = new_ref[0]
    k_new = jnp.concatenate([new[:, 2 * LANES:3 * LANES], pad], axis=0).astype(BF16)
    v_new = jnp.concatenate([new[:, 3 * LANES:4 * LANES], pad], axis=0).astype(BF16)
    for pg in range(n_pages):
        page_copy(b, pg, slc_roles).wait()
    kvt_ref[0] = raw_ref[2].astype(BF16)
    kvt_ref[1] = raw_ref[3].astype(BF16)

    @pl.when(b + 1 < nb)
    def _():
        start_pages(b + 1, slc_roles)

    kt = kvt_ref[0]
    vt = kvt_ref[1]

    o_slc = _slc_branch(
        qb, qpos, SAMPLE_T, sels, eexp_ref[...], past_len + LANES,
        lambda qg: jnp.concatenate([_dot(qg, kt), _dot_nt(qg, k_new)], axis=1),
        lambda pr: _dot_nt(pr[:, 0:past_len], vt) + _dot(pr[:, past_len:], v_new))

    nw = nwin_ref[0]
    st = state_ref[0, 0]
    k_win = jnp.concatenate([st[:, 0:LANES], nw[:, 0:LANES], pad], axis=0)
    v_win = jnp.concatenate([st[:, LANES:2 * LANES], nw[:, LANES:2 * LANES], pad], axis=0)
    o_win = _win_branch(qb, qpos, SAMPLE_T, k_win, v_win, past_len - WINDOW)
    outs = _combine(gate_ref[0], o_cmp, o_slc, o_win, SAMPLE_T)
    for h in range(N_Q_HEADS):
        o_ref[0, :, h * LANES:(h + 1) * LANES] = outs[h]


def _attn_sample(page_table, q, new_rows, new_win, state_win, gates, cache, lw, consts, layer):
    bs, n_pages = page_table.shape
    msel, eexp, n_slc = consts
    past_len = n_pages * PAGE_SIZE
    per_b = lambda i, pt: (i, 0, 0)
    grid_spec = pltpu.PrefetchScalarGridSpec(
        num_scalar_prefetch=1,
        grid=(bs,),
        in_specs=[
            pl.BlockSpec((1, SAMPLE_T, Q_PAD), per_b),
            pl.BlockSpec((1, SAMPLE_T, 4 * LANES), per_b),
            pl.BlockSpec((1, SAMPLE_T, 2 * LANES), per_b),
            pl.BlockSpec((1, 1, WINDOW, 2 * LANES), lambda i, pt: (layer, i, 0, 0)),
            pl.BlockSpec((1, SAMPLE_T, LANES), per_b),
            pl.BlockSpec(memory_space=pl.ANY),
            pl.BlockSpec(lw["cmp_w1"].shape, lambda i, pt: (0,) * 5),
            pl.BlockSpec(lw["cmp_pe"].shape, lambda i, pt: (0,) * 4),
            pl.BlockSpec(lw["cmp_w2"].shape, lambda i, pt: (0,) * 3),
            pl.BlockSpec(msel.shape, lambda i, pt: (0, 0)),
            pl.BlockSpec(eexp.shape, lambda i, pt: (0, 0)),
        ],
        out_specs=pl.BlockSpec((1, SAMPLE_T, Q_PAD), per_b),
        scratch_shapes=[
            pltpu.VMEM((4, LANES, past_len), F32),
            pltpu.VMEM((2, past_len, LANES), F32),
            pltpu.VMEM((2, LANES, past_len), BF16),
            pltpu.SemaphoreType.DMA((2, n_pages)),
        ],
    )
    return pl.pallas_call(
        functools.partial(_attn_sample_kernel, layer=layer, n_pages=n_pages, n_slc=n_slc),
        out_shape=jax.ShapeDtypeStruct((bs, SAMPLE_T, Q_PAD), F32),
        grid_spec=grid_spec,
        compiler_params=_cparams(("arbitrary",)),
        name="attn_sample",
    )(page_table, q, new_rows, new_win, state_win, gates, cache,
      lw["cmp_w1"], lw["cmp_pe"], lw["cmp_w2"], msel, eexp)


def _out_kernel(oa_ref, gb_ref, u_ref, p1_ref, p2_ref, cw_ref, x_ref, gm_ref, og_ref, w_ref, n2_ref,
                x1_ref, h2_ref):
    gmat = gm_ref[...]
    y_conv = cw_ref[0:1, :] * p2_ref[...] + cw_ref[1:2, :] * p1_ref[...] + cw_ref[2:3, :] * u_ref[...]
    o_conv = gb_ref[...] * y_conv
    parts = []
    for k in range(N_Q_HEADS):
        parts.append(_group_norm(oa_ref[:, k * LANES:(k + 1) * LANES], gmat,
                                 og_ref[:, k * LANES:(k + 1) * LANES]))
    for k in range(CONV_DIM // LANES):
        parts.append(_group_norm(o_conv[:, k * LANES:(k + 1) * LANES], gmat,
                                 og_ref[:, Q_PAD + k * LANES:Q_PAD + (k + 1) * LANES]))
    mix = jnp.concatenate(parts, axis=1).astype(BF16)
    x1 = x_ref[...] + _dot(mix, w_ref[...])
    x1_ref[...] = x1
    ms = jnp.mean(x1 * x1, axis=-1, keepdims=True)
    h2_ref[...] = (x1 * lax.rsqrt(ms + RMS_EPS) * n2_ref[...]).T.astype(BF16)


def _out_proj(o_attn, gb, u, prev1, prev2, x, lw, tm):
    n = x.shape[0]
    row = lambda i: (i, 0)
    fixed = lambda i: (0, 0)
    return pl.pallas_call(
        _out_kernel,
        out_shape=(jax.ShapeDtypeStruct((n, D_MODEL), F32), jax.ShapeDtypeStruct((D_MODEL, n), BF16)),
        grid=(n // tm,),
        in_specs=[
            pl.BlockSpec((tm, Q_PAD), row),
            pl.BlockSpec((tm, CONV_DIM), row),
            pl.BlockSpec((tm, CONV_DIM), row),
            pl.BlockSpec((tm, CONV_DIM), row),
            pl.BlockSpec((tm, CONV_DIM), row),
            pl.BlockSpec((SUBLANES, CONV_DIM), fixed),
            pl.BlockSpec((tm, D_MODEL), row),
            pl.BlockSpec((LANES, LANES), fixed),
            pl.BlockSpec((1, MIX_PAD), fixed),
            pl.BlockSpec((MIX_PAD, D_MODEL), fixed),
            pl.BlockSpec((1, D_MODEL), fixed),
        ],
        out_specs=(pl.BlockSpec((tm, D_MODEL), row), pl.BlockSpec((D_MODEL, tm), lambda i: (0, i))),
        compiler_params=_cparams(("parallel",)),
        name="out_proj",
    )(o_attn, gb, u, prev1, prev2, lw["conv_w"], x, lw["gmat"], lw["out_g"], lw["w_out"], lw["norm2"])


_CAND_ROWS = 80
PEER_EB = 2048
PEER_SUB = 256
PEER_GATE_TILES = 2
ROUTE_TOKENS = 1024


def _route_kernel(h_ref, wq_ref, sk_ref, a1_ref, cnt_ref, b_ref, r2_ref,
                  qt_ref, s_ref, so_ref, rank_ref, v_ref, cand_ref, cs_ref, z_ref):
    tn = h_ref.shape[1]
    k = PEER_TOPK
    qt_ref[...] = _dot(wq_ref[...], h_ref[...])

    def twice_bf16(x):
        bits = pltpu.bitcast(x.astype(BF16).astype(F32), jnp.uint32)
        return bits | (bits >> 16)
    rowid = lax.broadcasted_iota(jnp.int32, (PEER_NKEYS, tn), 0).astype(F32)

    crow = lax.broadcasted_iota(jnp.int32, (_CAND_ROWS, tn), 0)
    cr = jnp.where(crow < 16, 0, jnp.where(crow < 72, 1 + (crow - 16) // 8, crow - 64))
    cj = jnp.where(crow < 16, crow, jnp.where(crow < 72, (crow - 16) % 8, 0))
    cvalid = (cr + 1) * (cj + 1) <= k
    cflat = (cr * k + cj).astype(F32)
    rid16 = lax.broadcasted_iota(jnp.int32, (k, tn), 0).astype(F32)
    neg = -jnp.inf

    def any_miscount(selected):
        count = jnp.sum(jnp.where(selected, 1.0, 0.0), axis=0, keepdims=True)
        return jnp.max(jnp.where(count != float(k), 1.0, 0.0))

    def reset_scores():
        for p in range(2):
            s_ref[p] = so_ref[p]
            rank_ref[p] = jnp.full((PEER_NKEYS, tn), float(k), F32)

    def topk_keys(exact):
        for r in range(k):
            for p in range(2):
                s = s_ref[p]
                m = jnp.max(s, axis=0, keepdims=True)
                if exact:
                    idx = jnp.min(jnp.where(s == m, rowid, float(PEER_NKEYS)), axis=0, keepdims=True)
                    hit = rowid == idx
                else:
                    hit = s == m
                s_ref[p] = jnp.where(hit, neg, s)
                rank_ref[p] = jnp.where(hit, float(r), rank_ref[p])
                v_ref[p, r:r + 1, :] = m

    def build_candidates():
        v1 = v_ref[0]
        v2 = v_ref[1]
        cand_ref[0:16, :] = v1[0:1] + v2
        for r in range(1, 8):
            cand_ref[8 + 8 * r:16 + 8 * r, :] = v1[r:r + 1] + v2[0:8]
        cand_ref[72:80, :] = v1[8:16] + v2[0:1]
        cand_ref[...] = jnp.where(cvalid, cand_ref[...], neg)

    def topk_pairs_quick():
        cmax = v_ref[0, 0:1, :] + v_ref[1, 0:1, :]
        zsum = jnp.zeros((1, tn), F32)
        for _ in range(k):
            c = cand_ref[...]
            m = jnp.max(c, axis=0, keepdims=True)
            cand_ref[...] = jnp.where(c == m, neg, c)
            zsum = zsum + jnp.exp(m - cmax)
        taken = cvalid & (cand_ref[...] == neg)
        tk = jnp.where(taken, 1.0, 0.0)
        cs_ref[0:1, :] = jnp.sum(tk[0:16], axis=0, keepdims=True)
        for r in range(1, 8):
            cs_ref[r:r + 1, :] = jnp.sum(tk[8 + 8 * r:16 + 8 * r], axis=0, keepdims=True)
        cs_ref[8:16, :] = tk[72:80]
        z_ref[0:1, :] = zsum
        return any_miscount(taken)

    def topk_pairs_exact():
        cmax = v_ref[0, 0:1, :] + v_ref[1, 0:1, :]
        cnt = jnp.zeros((k, tn), F32)
        zsum = jnp.zeros((1, tn), F32)
        for _ in range(k):
            c = cand_ref[...]
            m = jnp.max(c, axis=0, keepdims=True)
            f = jnp.min(jnp.where(c == m, cflat, 1e9), axis=0, keepdims=True)
            cand_ref[...] = jnp.where(cflat == f, neg, c)
            cnt = cnt + jnp.where(rid16 == jnp.floor(f * (1.0 / k)), 1.0, 0.0)
            zsum = zsum + jnp.exp(m - cmax)
        cs_ref[...] = cnt
        z_ref[0:1, :] = zsum

    def head_body(h, carry):
        for p in range(2):
            base = pl.multiple_of(h * PEER_QDIM + p * (PEER_QDIM // 2), PEER_QDIM // 2)
            so_ref[p] = _dot(sk_ref[2 * h + p], qt_ref[pl.ds(base, PEER_QDIM // 2), :].astype(BF16))
        reset_scores()
        topk_keys(exact=False)
        tied = jnp.maximum(any_miscount(rank_ref[0] < float(k)), any_miscount(rank_ref[1] < float(k)))

        @pl.when(tied > 0.5)
        def _():
            reset_scores()
            topk_keys(exact=True)

        build_candidates()
        tied_pairs = topk_pairs_quick()

        @pl.when(tied_pairs > 0.5)
        def _():
            build_candidates()
            topk_pairs_exact()

        v1 = v_ref[0]
        v2 = v_ref[1]
        cnt = cs_ref[...]
        zsum = z_ref[0:1, :]
        r1 = rank_ref[0]
        r2 = rank_ref[1]
        a1_ref[h] = twice_bf16(jnp.where(r1 < float(k), jnp.exp(so_ref[0] - v1[0:1]), 0.0))
        bval = jnp.where(r2 < float(k), jnp.exp(so_ref[1] - v2[0:1]), 0.0) / zsum
        b_ref[h] = pltpu.bitcast(bval.astype(BF16), jnp.uint32)
        r2_ref[h] = pltpu.bitcast(r2.astype(BF16), jnp.uint32)
        cdense = jnp.zeros((PEER_NKEYS, tn), F32)
        for r in range(k):
            cdense = jnp.where(r1 == float(r), cnt[r:r + 1], cdense)
        cnt_ref[h] = twice_bf16(cdense)
        return carry

    lax.fori_loop(0, PEER_HEADS, head_body, 0)


def _route(h2, lw, tn):
    n = h2.shape[1]
    shape = jax.ShapeDtypeStruct((PEER_HEADS, PEER_NKEYS, n), jnp.uint32)
    shape_b = jax.ShapeDtypeStruct((PEER_HEADS, PEER_NKEYS // 2, n), jnp.uint32)
    spec = pl.BlockSpec((PEER_HEADS, PEER_NKEYS, tn), lambda i: (0, 0, i))
    spec_b = pl.BlockSpec((PEER_HEADS, PEER_NKEYS // 2, tn), lambda i: (0, 0, i))
    return pl.pallas_call(
        _route_kernel,
        out_shape=(shape, shape, shape_b, shape_b),
        grid=(n // tn,),
        in_specs=[
            pl.BlockSpec((D_MODEL, tn), lambda i: (0, i)),
            pl.BlockSpec((PEER_HEADS * PEER_QDIM, D_MODEL), lambda i: (0, 0)),
            pl.BlockSpec((2 * PEER_HEADS, PEER_NKEYS, PEER_QDIM // 2), lambda i: (0, 0, 0)),
        ],
        out_specs=(spec, spec, spec_b, spec_b),
        scratch_shapes=[
            pltpu.VMEM((PEER_HEADS * PEER_QDIM, tn), F32),
            pltpu.VMEM((2, PEER_NKEYS, tn), F32),
            pltpu.VMEM((2, PEER_NKEYS, tn), F32),
            pltpu.VMEM((2, PEER_NKEYS, tn), F32),
            pltpu.VMEM((2, PEER_TOPK, tn), F32),
            pltpu.VMEM((_CAND_ROWS, tn), F32),
            pltpu.VMEM((PEER_TOPK, tn), F32),
            pltpu.VMEM((SUBLANES, tn), F32),
        ],
        compiler_params=_cparams(("parallel",)),
        name="peer_route",
    )(h2, lw["peer_wq_t"], lw["peer_sk"])


def _gelu_tanh(x):
    c = math.sqrt(2.0 / math.pi)
    hx = 0.5 * x
    return hx + hx * jnp.tanh(x * (c + (c * 0.044715) * (x * x)))


def _peer_kernel(h_ref, u_ref, vt_ref, a1_ref, cnt_ref, b_ref, r2_ref, x_ref, y_ref,
                 acc_ref, at_ref, *, eb, sub):
    e = pl.program_id(1)
    tn = h_ref.shape[1]
    pack = 2 * SUBLANES

    @pl.when(e == 0)
    def _():
        acc_ref[...] = jnp.zeros_like(acc_ref)

    h = h_ref[...]
    zero = jnp.zeros((pack, LANES), BF16)

    def row_pair(ref, hh, ai, cols):
        return pltpu.bitcast(jnp.broadcast_to(ref[hh, ai:ai + 1, cols], (SUBLANES, LANES)), BF16)

    def sub_block(sb):
        r0 = sb * sub
        st = _dot(u_ref[r0:r0 + sub, :], h)
        n_a = PEER_GATE_TILES
        n_k = PEER_NKEYS // pack
        for a0 in range(0, sub // PEER_NKEYS, n_a):
            ai0 = r0 // PEER_NKEYS + a0
            for c in range(tn // LANES):
                cols = slice(c * LANES, (c + 1) * LANES)
                w = [[None] * n_k for _ in range(n_a)]
                for hh in range(PEER_HEADS):
                    cnt = [row_pair(cnt_ref, hh, ai0 + a, cols) for a in range(n_a)]
                    a1 = [row_pair(a1_ref, hh, ai0 + a, cols) for a in range(n_a)]
                    for k in range(n_k):
                        rk = slice(k * SUBLANES, (k + 1) * SUBLANES)
                        r2 = pltpu.bitcast(r2_ref[hh, rk, cols], BF16)
                        bb = pltpu.bitcast(b_ref[hh, rk, cols], BF16)
                        for a in range(n_a):
                            t = a1[a] * jnp.where(r2 < cnt[a], bb, zero)
                            w[a][k] = t if w[a][k] is None else w[a][k] + t
                for a in range(n_a):
                    lo = (a0 + a) * PEER_NKEYS
                    g = _gelu_tanh(st[lo:lo + PEER_NKEYS, cols].astype(BF16))
                    at_ref[r0 + lo:r0 + lo + PEER_NKEYS, cols] = jnp.concatenate(w[a], axis=0) * g

    n_sb = eb // sub
    half = eb // 2
    tot = None
    for sb in range(n_sb):
        sub_block(sb)
        if (sb + 1) * sub % half == 0:
            k0 = (sb + 1) * sub - half
            d = lax.dot_general(vt_ref[k0:k0 + half, :], at_ref[k0:k0 + half, :],
                                (((0,), (0,)), ((), ())), preferred_element_type=F32)
            tot = d if tot is None else tot + d
    acc_ref[...] += tot

    @pl.when(e == pl.num_programs(1) - 1)
    def _():
        y_ref[...] = x_ref[...] + acc_ref[...].T


def _peer(h2, routed, x1, peer_u, peer_v, layer, tn, eb):
    n = h2.shape[1]
    n_exp = PEER_NKEYS * PEER_NKEYS
    a1, cnt, bb, r2 = routed
    tok = lambda i, e: (i, 0)
    key_rows = pl.BlockSpec((PEER_HEADS, eb // PEER_NKEYS, tn), lambda i, e: (0, e, i))
    key_full = pl.BlockSpec((PEER_HEADS, PEER_NKEYS // 2, tn), lambda i, e: (0, 0, i))
    return pl.pallas_call(
        functools.partial(_peer_kernel, eb=eb, sub=PEER_SUB),
        out_shape=jax.ShapeDtypeStruct((n, D_MODEL), F32),
        grid=(n // tn, n_exp // eb),
        in_specs=[
            pl.BlockSpec((D_MODEL, tn), lambda i, e: (0, i)),
            pl.BlockSpec((None, eb, D_MODEL), lambda i, e: (layer, e, 0)),
            pl.BlockSpec((None, eb, D_MODEL), lambda i, e: (layer, e, 0)),
            key_rows, key_rows, key_full, key_full,
            pl.BlockSpec((tn, D_MODEL), tok),
        ],
        out_specs=pl.BlockSpec((tn, D_MODEL), tok),
        scratch_shapes=[pltpu.VMEM((D_MODEL, tn), F32), pltpu.VMEM((eb, tn), BF16)],
        compiler_params=_cparams(("parallel", "arbitrary")),
        name="peer_experts",
    )(h2, peer_u, peer_v, a1, cnt, bb, r2, x1)


def _rope_tables(pos):
    inv_freq = ROPE_THETA ** (-jnp.arange(ROPE_HALF, dtype=F32) / ROPE_HALF)
    ang = pos.astype(F32)[:, None] * inv_freq[None, :]
    cos, sin = jnp.cos(ang), jnp.sin(ang)
    n = pos.shape[0]
    ones = jnp.ones((n, HEAD_DIM - ROPE_DIM), F32)
    zeros = jnp.zeros((n, HEAD_DIM - ROPE_DIM), F32)
    zh = jnp.zeros((n, ROPE_HALF), F32)
    c = jnp.concatenate([cos, cos, ones], axis=1)
    s_up = jnp.concatenate([-sin, zh, zeros], axis=1)
    s_dn = jnp.concatenate([zh, sin, zeros], axis=1)
    return tuple(jnp.tile(a, (1, 2)) for a in (c, s_up, s_dn))


def _attn_consts(n_cmp, nch, n_slc, slc_len):
    jn = -(-n_slc // LANES) * LANES
    sr = SLC_BLOCK // CMP_STRIDE
    r = CMP_LEN // CMP_STRIDE
    msel = np.zeros((nch, jn), np.float32)
    for j in range(n_slc):
        for m in range(sr):
            for n in range(r):
                i = sr * j + m - n
                if 0 <= i < n_cmp:
                    msel[i, j] += 1.0
    eexp = np.zeros((jn, slc_len), np.float32)
    keys = np.arange(slc_len)
    eexp[keys // SLC_BLOCK, keys] = 1.0
    return jnp.asarray(msel, BF16), jnp.asarray(eexp, BF16), n_slc


def _pad_heads(x, axis):
    shape = x.shape
    x = jnp.moveaxis(x, axis, -1).reshape(shape[:axis] + shape[axis + 1:] + (N_KV_HEADS, HEADS_PER_GROUP, HEAD_DIM))
    nd = x.ndim
    lo = jnp.pad(x[..., 0, :, :], [(0, 0)] * (nd - 2) + [(0, HEAD_DIM)])
    hi = jnp.pad(x[..., 1, :, :], [(0, 0)] * (nd - 2) + [(HEAD_DIM, 0)])
    out = jnp.concatenate([lo, hi], axis=-2)
    out = out.reshape(out.shape[:-2] + (Q_PAD,))
    return jnp.moveaxis(out, -1, axis)


def _prep_weights(norm1_g, w_in, q_norm_g, k_norm_g, cmp_pe, cmp_w1, cmp_w2, conv_w, out_norm_g,
                  w_out, norm2_g, peer_wq, peer_subkeys, peer_u, peer_v):
    depth = w_in.shape[0]
    c1 = NSA_WIDTH
    c2 = c1 + 6 * KV_WIDTH
    c3 = c2 + N_Q_HEADS * N_BRANCH
    gl_pad = jnp.zeros((depth, D_MODEL, LANES - N_Q_HEADS * N_BRANCH), F32)
    w_all = jnp.concatenate([_pad_heads(w_in[:, :, :c1], 2), w_in[:, :, c1:c2], w_in[:, :, c3:],
                             w_in[:, :, c2:c3], gl_pad], axis=2)

    blk = np.kron(np.eye(2, dtype=np.float32), np.full((HEAD_DIM, HEAD_DIM), 1.0 / HEAD_DIM, np.float32))
    kg = jnp.pad(jnp.tile(k_norm_g, (1, 1, 2)), ((0, 0), (0, SUBLANES - N_BRANCH), (0, 0)))

    r = CMP_LEN // CMP_STRIDE
    eye2 = jnp.eye(2, dtype=F32)
    w1 = cmp_w1.reshape(depth, 2, r, CMP_STRIDE, HEAD_DIM, HEAD_DIM)
    w1 = jnp.einsum('lxmsde,gk->lxmsgdke', w1, eye2).reshape(depth, 2, r, CMP_STRIDE // 2, 2 * LANES, LANES)
    pe = jnp.tile(cmp_pe.reshape(depth, 2, r, CMP_STRIDE, HEAD_DIM), (1, 1, 1, 1, 2))
    w2 = jnp.einsum('lxde,gk->lxgdke', cmp_w2, eye2).reshape(depth, 2, LANES, LANES)

    og_pad = jnp.concatenate([_pad_heads(out_norm_g[:, :NSA_WIDTH], 1), out_norm_g[:, NSA_WIDTH:]], axis=1)
    wo_pad = jnp.concatenate([_pad_heads(w_out[:, :NSA_WIDTH], 1), w_out[:, NSA_WIDTH:]], axis=1)
    return {
        "norm1": norm1_g[:, None, :],
        "w_in": w_all.astype(BF16),
        "gmat": jnp.broadcast_to(jnp.asarray(blk, BF16), (depth, LANES, LANES)),
        "q_g": jnp.tile(q_norm_g, (1, 2))[:, None, :],
        "k_g": kg,
        "cmp_w1": w1.astype(BF16),
        "cmp_pe": pe,
        "cmp_w2": w2.astype(BF16),
        "conv_w": jnp.pad(conv_w, ((0, 0), (0, SUBLANES - CONV_WIDTH), (0, 0))),
        "out_g": og_pad[:, None, :],
        "w_out": wo_pad.astype(BF16),
        "norm2": norm2_g[:, None, :],
        "peer_wq_t": jnp.swapaxes(peer_wq, 1, 2).astype(BF16),
        "peer_sk": peer_subkeys.reshape(depth, 2 * PEER_HEADS, PEER_NKEYS, PEER_QDIM // 2).astype(BF16),
        "peer_u": peer_u.astype(BF16),
        "peer_vt": peer_v.astype(BF16),
    }


def _conv_prev(u, prefix, b, t):
    ext = jnp.concatenate([prefix, u.reshape(b, t, CONV_DIM)], axis=1)
    return ext[:, 1:t + 1].reshape(b * t, CONV_DIM), ext[:, 0:t].reshape(b * t, CONV_DIM)


def _row_tile(n):
    for tm in (512, 256, 128):
        if n % tm == 0:
            return tm
    raise ValueError(f"token count {n} must be a multiple of 128")


def _route_tile(n):
    for tn in (ROUTE_TOKENS, ROUTE_TOKENS // 2, ROUTE_TOKENS // 4, LANES):
        if n % tn == 0:
            return tn
    raise ValueError(f"token count {n} must be a multiple of 128")


def _peer_tile(n):
    for tn in (512, 256, 128):
        if n % tn == 0:
            return tn
    raise ValueError(f"token count {n} must be a multiple of 128")


def kernel(x_prompt, x_sample, cache_nsa_kv, state_win_kv, state_conv, page_table,
           norm1_g, w_in, q_norm_g, k_norm_g, cmp_pe, cmp_w1, cmp_w2, conv_w, out_norm_g, w_out,
           norm2_g, peer_wq, peer_subkeys, peer_u, peer_v):
    bp, t, _ = x_prompt.shape
    bs, ts, _ = x_sample.shape
    depth = norm1_g.shape[0]
    n_pages = page_table.shape[1]
    past_len = n_pages * PAGE_SIZE
    assert t % SLC_KEY_STEP == 0 and t >= WINDOW + Q_BLOCK and ts <= SAMPLE_T
    assert state_win_kv.shape[2] == WINDOW and past_len % SLC_BLOCK == 0

    xp = x_prompt.reshape(bp * t, D_MODEL)
    xs = jnp.pad(x_sample, ((0, 0), (0, SAMPLE_T - ts), (0, 0))).reshape(bs * SAMPLE_T, D_MODEL)
    cache = jnp.transpose(cache_nsa_kv, (0, 2, 3, 4, 5, 1)).reshape(
        cache_nsa_kv.shape[0], depth * 4, LANES, PAGE_SIZE)
    state_win = state_win_kv.reshape(depth, bs, WINDOW, 2 * LANES)

    tabs_p = _rope_tables(jnp.arange(t))
    tabs_s = _rope_tables(jnp.tile(past_len + jnp.arange(SAMPLE_T), bs))
    n_cmp_p = (t - CMP_LEN) // CMP_STRIDE + 1
    consts_p = _attn_consts(n_cmp_p, t // CMP_STRIDE, t // SLC_BLOCK, t)
    t_all = past_len + ts
    n_cmp_s = (t_all - CMP_LEN) // CMP_STRIDE + 1
    assert n_cmp_s <= past_len // CMP_STRIDE
    consts_s = _attn_consts(n_cmp_s, past_len // CMP_STRIDE, -(-t_all // SLC_BLOCK), past_len + LANES)
    zero_prefix = jnp.zeros((bp, CONV_WIDTH - 1, CONV_DIM), F32)

    weights = _prep_weights(norm1_g, w_in, q_norm_g, k_norm_g, cmp_pe, cmp_w1, cmp_w2, conv_w, out_norm_g,
                            w_out, norm2_g, peer_wq, peer_subkeys, peer_u, peer_v)
    peer_u_b = weights.pop("peer_u")
    peer_v_b = weights.pop("peer_vt")
    rows_p, rows_s, win_p, win_s, conv_p, conv_s = [], [], [], [], [], []
    for l in range(depth):
        lw = {name: stacked[l] for name, stacked in weights.items()}
        q, rows, win, gates, gb, u = _in_proj(xp, lw, tabs_p, _row_tile(bp * t))
        kc, vc = _compress_prompt(rows, lw, bp, t)
        o_attn = _attn_prompt(q, rows, win, kc, vc, gates, consts_p, bp, t)
        p1, p2 = _conv_prev(u, zero_prefix, bp, t)
        x1, h2 = _out_proj(o_attn, gb, u, p1, p2, xp, lw, _row_tile(bp * t))
        routed = _route(h2, lw, _route_tile(h2.shape[1]))
        xp = _peer(h2, routed, x1, peer_u_b, peer_v_b, l, _peer_tile(bp * t), PEER_EB)
        rows_p.append(rows.reshape(bp, t, 4, N_KV_HEADS, HEAD_DIM))
        win_p.append(win.reshape(bp, t, 2, N_KV_HEADS, HEAD_DIM)[:, t - WINDOW:])
        conv_p.append(u.reshape(bp, t, CONV_DIM)[:, t - (CONV_WIDTH - 1):])

        q, rows, win, gates, gb, u = _in_proj(xs, lw, tabs_s, _row_tile(bs * SAMPLE_T))
        o_attn = _attn_sample(
            page_table, q.reshape(bs, SAMPLE_T, Q_PAD), rows.reshape(bs, SAMPLE_T, 4 * LANES),
            win.reshape(bs, SAMPLE_T, 2 * LANES), state_win, gates.reshape(bs, SAMPLE_T, LANES),
            cache, lw, consts_s, l).reshape(bs * SAMPLE_T, Q_PAD)
        p1, p2 = _conv_prev(u, state_conv[l], bs, SAMPLE_T)
        x1, h2 = _out_proj(o_attn, gb, u, p1, p2, xs, lw, _row_tile(bs * SAMPLE_T))
        routed = _route(h2, lw, _route_tile(h2.shape[1]))
        xs = _peer(h2, routed, x1, peer_u_b, peer_v_b, l, _peer_tile(bs * SAMPLE_T), PEER_EB)
        rows_s.append(rows.reshape(bs, SAMPLE_T, 4, N_KV_HEADS, HEAD_DIM)[:, :ts])
        new_win = win.reshape(bs, SAMPLE_T, 2, N_KV_HEADS, HEAD_DIM)[:, :ts]
        old_win = state_win_kv[l]
        win_s.append(jnp.concatenate([old_win, new_win], axis=1)[:, ts:])
        conv_s.append(u.reshape(bs, SAMPLE_T, CONV_DIM)[:, ts - (CONV_WIDTH - 1):ts])

    y_prompt = xp.reshape(bp, t, D_MODEL)
    y_sample = xs.reshape(bs, SAMPLE_T, D_MODEL)[:, :ts]
    return (y_prompt, y_sample,
            jnp.stack(rows_p, axis=2), jnp.stack(rows_s, axis=2),
            jnp.stack(win_p, axis=0), jnp.stack(win_s, axis=0),
            jnp.stack(conv_p, axis=0), jnp.stack(conv_s, axis=0))
```
